```python
import jax, jax.numpy as jnp
from jax import lax
import numpy as np


D_MODEL = 4096
BATCH = 2
SEQ = 8192
DEPTH = 1

HEAD_DIM = 64
N_Q_HEADS = 32
N_KV_HEADS = 4
Q_PER_KV = N_Q_HEADS // N_KV_HEADS
ATTN_WIDTH = N_Q_HEADS * HEAD_DIM
KV_WIDTH = N_KV_HEADS * HEAD_DIM
WINDOW = 128
ATTN_BLOCK = 128

POOL_WINDOWS = (2, 4, 8, 16)
N_POOL_GROUPS = 4
POOL_WIDTH = D_MODEL // 2
POOL_GROUP_DIM = POOL_WIDTH // N_POOL_GROUPS

IN_WIDTH = ATTN_WIDTH + 2 * KV_WIDTH + POOL_WIDTH + 2 * D_MODEL
SPLITS = (ATTN_WIDTH, ATTN_WIDTH + KV_WIDTH, ATTN_WIDTH + 2 * KV_WIDTH,
          ATTN_WIDTH + 2 * KV_WIDTH + POOL_WIDTH, ATTN_WIDTH + 2 * KV_WIDTH + POOL_WIDTH + D_MODEL)

N_GROUPS = 8
EXPERTS_PER_GROUP = 8
N_EXPERTS = N_GROUPS * EXPERTS_PER_GROUP
TOP_K = 2
EXPERT_FF = 512
MOE_BLOCK = 128

EPS = 1e-6
NEG_INF = -1e30

kernel_name = 'hybrid_swa_pool_hier_moe'


def rms_norm(x, w):
    x32 = x.astype(jnp.float32)
    y = x32 * lax.rsqrt(jnp.mean(x32 * x32, axis=-1, keepdims=True) + EPS)
    return (y * w.astype(jnp.float32)).astype(x.dtype)


def alibi_slopes(n):
    return 2.0 ** (-8.0 * jnp.arange(1, n + 1, dtype=jnp.float32) / n)


def sliding_window_attention(q, k, v, sinks):
    B, S = q.shape[0], q.shape[1]
    nb = S // ATTN_BLOCK
    qb = q.reshape(B, nb, ATTN_BLOCK, N_KV_HEADS, Q_PER_KV, HEAD_DIM)

    def banded(t):
        pad = jnp.zeros_like(t[:, :ATTN_BLOCK])
        prev = jnp.concatenate([pad, t[:, :-ATTN_BLOCK]], axis=1)
        prev = prev.reshape(B, nb, ATTN_BLOCK, N_KV_HEADS, HEAD_DIM)
        cur = t.reshape(B, nb, ATTN_BLOCK, N_KV_HEADS, HEAD_DIM)
        return jnp.concatenate([prev, cur], axis=2)

    kb, vb = banded(k), banded(v)
    scores = jnp.einsum('bnqkgd,bnjkd->bnkgqj', qb, kb).astype(jnp.float32) * (HEAD_DIM ** -0.5)
    i = jnp.arange(ATTN_BLOCK)[:, None]
    j = jnp.arange(2 * ATTN_BLOCK)[None, :]
    dist = i + ATTN_BLOCK - j
    key_pos = jnp.arange(nb)[:, None, None] * ATTN_BLOCK - ATTN_BLOCK + j[None]
    valid = (dist >= 0) & (dist < WINDOW) & (key_pos >= 0)
    slopes = alibi_slopes(N_Q_HEADS).reshape(N_KV_HEADS, Q_PER_KV)
    scores = scores - slopes[:, :, None, None] * dist.astype(jnp.float32)
    scores = jnp.where(valid[None, :, None, None], scores, NEG_INF)
    sink = sinks.astype(jnp.float32).reshape(N_KV_HEADS, Q_PER_KV)[:, :, None, None]
    m = jnp.maximum(scores.max(axis=-1, keepdims=True), sink)
    p = jnp.exp(scores - m)
    p = p / (p.sum(axis=-1, keepdims=True) + jnp.exp(sink - m))
    out = jnp.einsum('bnkgqj,bnjkd->bnqkgd', p.astype(v.dtype), vb)
    return out.reshape(B, S, ATTN_WIDTH)


def multiscale_pool(p, w_pool, pool_scale):
    B, S = p.shape[0], p.shape[1]
    pg = p.reshape(B, S, N_POOL_GROUPS, POOL_GROUP_DIM).astype(jnp.float32)
    cs = jnp.cumsum(pg, axis=1)
    t = jnp.arange(1, S + 1, dtype=jnp.float32)
    diffs = []
    for g, w in enumerate(POOL_WINDOWS):
        c_g = cs[:, :, g]
        lagged = jnp.pad(c_g, ((0, 0), (w, 0), (0, 0)))[:, :S]
        mean = (c_g - lagged) / jnp.minimum(t, float(w))[None, :, None]
        diffs.append(mean - pg[:, :, g])
    d = jnp.stack(diffs, axis=2).astype(p.dtype)
    y = jnp.einsum('bsgc,gcd->bsgd', d, w_pool).reshape(B, S, POOL_WIDTH)
    return y * pool_scale


def hierarchical_moe(h, w_router_group, b_router_group, w_router_expert, b_router_expert,
                     w_gate, w_up, w_down):
    B, S, D = h.shape
    T = B * S
    hf = h.reshape(T, D)
    g_logits = (hf @ w_router_group + b_router_group).astype(jnp.float32)
    g_prob = jax.nn.softmax(g_logits, axis=-1)
    g_sel = jnp.argmax(g_logits, axis=-1)
    p_group = jnp.take_along_axis(g_prob, g_sel[:, None], axis=-1)[:, 0]
    e_logits = (hf @ w_router_expert + b_router_expert).astype(jnp.float32)
    e_logits = e_logits.reshape(T, N_GROUPS, EXPERTS_PER_GROUP)
    e_in = jnp.take_along_axis(e_logits, g_sel[:, None, None], axis=1)[:, 0]
    top_v, top_i = lax.top_k(e_in, TOP_K)
    weights = p_group[:, None] * jax.nn.softmax(top_v, axis=-1)
    expert_id = g_sel[:, None] * EXPERTS_PER_GROUP + top_i

    A = T * TOP_K
    cap = A + N_EXPERTS * MOE_BLOCK
    nblk = cap // MOE_BLOCK
    flat_e = expert_id.reshape(A).astype(jnp.int32)
    flat_t = jnp.repeat(jnp.arange(T, dtype=jnp.int32), TOP_K)
    flat_w = weights.reshape(A)
    order = jnp.argsort(flat_e)
    se = flat_e[order]
    counts = jax.ops.segment_sum(jnp.ones_like(flat_e), flat_e, num_segments=N_EXPERTS)
    padded = ((counts + MOE_BLOCK - 1) // MOE_BLOCK) * MOE_BLOCK
    start = jnp.cumsum(counts) - counts
    pend = jnp.cumsum(padded)
    pstart = pend - padded
    dest = pstart[se] + (jnp.arange(A, dtype=jnp.int32) - start[se])
    buf_tok = jnp.full((cap,), T, dtype=jnp.int32).at[dest].set(flat_t[order])
    buf_w = jnp.zeros((cap,), jnp.float32).at[dest].set(flat_w[order])
    blk_start = jnp.arange(nblk, dtype=jnp.int32) * MOE_BLOCK
    blk_e = jnp.minimum(jnp.searchsorted(pend, blk_start, side='right'), N_EXPERTS - 1)

    h_pad = jnp.concatenate([hf, jnp.zeros((1, D), hf.dtype)], axis=0)
    xs = h_pad[buf_tok].reshape(nblk, MOE_BLOCK, D)

    def expert_block(args):
        xb, e = args
        a = xb @ w_gate[e]
        u = xb @ w_up[e]
        return (jax.nn.silu(a) * u) @ w_down[e]

    ys = lax.map(expert_block, (xs, blk_e)).reshape(cap, D)
    ys = ys * buf_w[:, None].astype(ys.dtype)
    out = jnp.zeros((T + 1, D), ys.dtype).at[buf_tok].add(ys)[:T]
    return out.reshape(B, S, D)


def hybrid_layer(x, c, w_ada, b_ada, norm1_w, w_in, q_norm_w, k_norm_w, sinks, w_pool, pool_scale,
                 w_attn_up, w_pool_up, w_out, norm2_w, w_router_group, b_router_group,
                 w_router_expert, b_router_expert, w_gate, w_up, w_down):
    B, S, D = x.shape
    ada = jax.nn.silu(c) @ w_ada + b_ada
    shift1, scale1, gate1, shift2, scale2, gate2 = [a[:, None, :] for a in jnp.split(ada, 6, axis=-1)]

    h = rms_norm(x, norm1_w) * (1.0 + scale1) + shift1
    proj = h @ w_in
    q, k, v, p, ga, gb = jnp.split(proj, SPLITS, axis=-1)
    q = rms_norm(q.reshape(B, S, N_Q_HEADS, HEAD_DIM), q_norm_w)
    k = rms_norm(k.reshape(B, S, N_KV_HEADS, HEAD_DIM), k_norm_w)
    v = v.reshape(B, S, N_KV_HEADS, HEAD_DIM)
    y_attn = sliding_window_attention(q, k, v, sinks) @ w_attn_up
    y_pool = multiscale_pool(p, w_pool, pool_scale) @ w_pool_up
    mixed = jax.nn.sigmoid(ga) * y_attn + jax.nn.sigmoid(gb) * y_pool
    x = x + gate1 * (mixed @ w_out)

    h2 = rms_norm(x, norm2_w) * (1.0 + scale2) + shift2
    y_ffn = hierarchical_moe(h2, w_router_group, b_router_group, w_router_expert, b_router_expert,
                             w_gate, w_up, w_down)
    return x + gate2 * y_ffn


def setup_inputs(seed: int = 0) -> dict:
    key = jax.random.key(seed)
    ks = jax.random.split(key, 22)
    L = DEPTH

    def nrm(k, shape, scale):
        return jax.random.normal(k, shape, jnp.float32) * scale

    return {
        'x': nrm(ks[0], (BATCH, SEQ, D_MODEL), 1.0),
        'c': nrm(ks[1], (BATCH, D_MODEL), 1.0),
        'w_ada': nrm(ks[2], (L, D_MODEL, 6 * D_MODEL), 0.2 * D_MODEL ** -0.5),
        'b_ada': nrm(ks[3], (L, 6 * D_MODEL), 0.02),
        'norm1_w': 1.0 + nrm(ks[4], (L, D_MODEL), 0.02),
        'w_in': nrm(ks[5], (L, D_MODEL, IN_WIDTH), D_MODEL ** -0.5),
        'q_norm_w': 1.0 + nrm(ks[6], (L, HEAD_DIM), 0.02),
        'k_norm_w': 1.0 + nrm(ks[7], (L, HEAD_DIM), 0.02),
        'sinks': nrm(ks[8], (L, N_Q_HEADS), 0.5),
        'w_pool': nrm(ks[9], (L, N_POOL_GROUPS, POOL_GROUP_DIM, POOL_GROUP_DIM), POOL_GROUP_DIM ** -0.5),
        'pool_scale': 1.0 + nrm(ks[10], (L, POOL_WIDTH), 0.02),
        'w_attn_up': nrm(ks[11], (L, ATTN_WIDTH, D_MODEL), ATTN_WIDTH ** -0.5),
        'w_pool_up': nrm(ks[12], (L, POOL_WIDTH, D_MODEL), POOL_WIDTH ** -0.5),
        'w_out': nrm(ks[13], (L, D_MODEL, D_MODEL), D_MODEL ** -0.5),
        'norm2_w': 1.0 + nrm(ks[14], (L, D_MODEL), 0.02),
        'w_router_group': nrm(ks[15], (L, D_MODEL, N_GROUPS), D_MODEL ** -0.5),
        'b_router_group': nrm(ks[16], (L, N_GROUPS), 0.01),
        'w_router_expert': nrm(ks[17], (L, D_MODEL, N_EXPERTS), D_MODEL ** -0.5),
        'b_router_expert': nrm(ks[18], (L, N_EXPERTS), 0.01),
        'w_gate': nrm(ks[19], (L, N_EXPERTS, D_MODEL, EXPERT_FF), D_MODEL ** -0.5),
        'w_up': nrm(ks[20], (L, N_EXPERTS, D_MODEL, EXPERT_FF), D_MODEL ** -0.5),
        'w_down': nrm(ks[21], (L, N_EXPERTS, EXPERT_FF, D_MODEL), EXPERT_FF ** -0.5),
    }


def reference(x, c, w_ada, b_ada, norm1_w, w_in, q_norm_w, k_norm_w, sinks, w_pool, pool_scale,
              w_attn_up, w_pool_up, w_out, norm2_w, w_router_group, b_router_group,
              w_router_expert, b_router_expert, w_gate, w_up, w_down):
    for l in range(DEPTH):
        x = hybrid_layer(x, c, w_ada[l], b_ada[l], norm1_w[l], w_in[l], q_norm_w[l], k_norm_w[l],
                         sinks[l], w_pool[l], pool_scale[l], w_attn_up[l], w_pool_up[l], w_out[l],
                         norm2_w[l], w_router_group[l], b_router_group[l], w_router_expert[l],
                         b_router_expert[l], w_gate[l], w_up[l], w_down[l])
    return x
```

```python
import functools

import jax
import jax.numpy as jnp
from jax import lax
from jax.experimental import pallas as pl
from jax.experimental.pallas import tpu as pltpu

F32 = jnp.float32
BF16 = jnp.bfloat16
U32 = jnp.uint32
I32 = jnp.int32

HEAD_DIM = 64
N_Q_HEADS = 32
N_KV_HEADS = 4
Q_PER_KV = N_Q_HEADS // N_KV_HEADS
ATTN_BLOCK = 128
WINDOW = 128
POOL_WINDOWS = (2, 4, 8, 16)
POOL_HALO = 16
N_GROUPS = 8
EXPERTS_PER_GROUP = 8
N_EXPERTS = N_GROUPS * EXPERTS_PER_GROUP
N_ROUTER = N_GROUPS + N_EXPERTS
TOP_K = 2
EPS = 1e-6
NEG_INF = -1e30

LANES = 128
EXPERT_BLOCK = 256
MIB = 1024 * 1024


def _params(semantics, vmem_mib):
    return pltpu.CompilerParams(dimension_semantics=semantics, vmem_limit_bytes=vmem_mib * MIB)


def _pick(n, candidates):
    for c in candidates:
        if n % c == 0:
            return c
    return n


def _ada_kernel(c_ref, w_ref, b_ref, o_ref):
    c = c_ref[...]
    s = (c * jax.nn.sigmoid(c)).astype(BF16)
    o_ref[...] = jnp.dot(s, w_ref[...].astype(BF16), preferred_element_type=F32) + b_ref[...]


def _ada(c8, w, b):
    d, n = w.shape
    tn = _pick(n, (1024, 512, 256, 128))
    return pl.pallas_call(
        _ada_kernel,
        out_shape=jax.ShapeDtypeStruct((8, n), F32),
        grid=(n // tn,),
        in_specs=[pl.BlockSpec((8, d), lambda j: (0, 0)),
                  pl.BlockSpec((d, tn), lambda j: (0, j)),
                  pl.BlockSpec((1, tn), lambda j: (0, j))],
        out_specs=pl.BlockSpec((8, tn), lambda j: (0, j)),
        compiler_params=_params(("arbitrary",), 56),
    )(c8, w, b)


def _modulated_norm(x, nw, scale, shift):
    ms = jnp.mean(x * x, axis=-1, keepdims=True)
    y = x * lax.rsqrt(ms + EPS) * nw
    return y * (1.0 + scale) + shift


NORM_ROWS = 64


def _in_kernel(x_ref, nw_ref, sc_ref, sh_ref, w_ref, o_ref, h_ref):
    @pl.when(pl.program_id(1) == 0)
    def _():
        def body(r, carry):
            rows = pl.ds(pl.multiple_of(r * NORM_ROWS, NORM_ROWS), NORM_ROWS)
            h_ref[rows, :] = _modulated_norm(x_ref[rows, :], nw_ref[...], sc_ref[0],
                                             sh_ref[0]).astype(BF16)
            return carry

        lax.fori_loop(0, x_ref.shape[0] // NORM_ROWS, body, 0)

    o_ref[...] = jnp.dot(h_ref[...], w_ref[...], preferred_element_type=F32).astype(o_ref.dtype)


def _in_proj(x2, nw, scale, shift, w, seq):
    t, d = x2.shape
    n = w.shape[1]
    tm = _pick(seq, (512, 256, 128))
    tn = _pick(n, (1280, 512, 256, 128))
    rows_per_batch = seq // tm
    return pl.pallas_call(
        _in_kernel,
        out_shape=jax.ShapeDtypeStruct((t, n), BF16),
        grid=(t // tm, n // tn),
        in_specs=[pl.BlockSpec((tm, d), lambda i, j: (i, 0)),
                  pl.BlockSpec((1, d), lambda i, j: (0, 0)),
                  pl.BlockSpec((1, 1, d), lambda i, j: (i // rows_per_batch, 0, 0)),
                  pl.BlockSpec((1, 1, d), lambda i, j: (i // rows_per_batch, 0, 0)),
                  pl.BlockSpec((d, tn), lambda i, j: (0, j))],
        out_specs=pl.BlockSpec((tm, tn), lambda i, j: (i, j)),
        scratch_shapes=[pltpu.VMEM((tm, d), BF16)],
        compiler_params=_params(("parallel", "arbitrary"), 56),
    )(x2, nw, scale, shift, w)


def _head_sumsq(v, ones_bd):
    sq = v * v
    hi = sq.astype(BF16)
    lo = (sq - hi.astype(F32)).astype(BF16)
    return (jnp.dot(hi, ones_bd, preferred_element_type=F32)
            + jnp.dot(lo, ones_bd, preferred_element_type=F32))


def _head_rms_norm(v, w2, ones_bd):
    ss = _head_sumsq(v, ones_bd)
    return v * lax.rsqrt(ss * (1.0 / HEAD_DIM) + EPS) * w2


def _attn_kernel(sink_ref, q_ref, kvc_ref, kvp_ref, bias_ref, qw_ref, kw_ref, o_ref):
    blk = ATTN_BLOCK
    kvw = N_KV_HEADS * HEAD_DIM
    lane = lax.broadcasted_iota(I32, (1, LANES), 1)
    low_half = lane < HEAD_DIM
    r = lax.broadcasted_iota(I32, (LANES, LANES), 0) // HEAD_DIM
    c = lax.broadcasted_iota(I32, (LANES, LANES), 1) // HEAD_DIM
    ones_bd = jnp.where(r == c, 1.0, 0.0).astype(BF16)

    kv = jnp.concatenate([kvp_ref[0], kvc_ref[0]], axis=0).astype(F32)
    kw2 = kw_ref[...]
    sum_a = jnp.broadcast_to(jnp.where(low_half, 1.0, 0.0), (2 * blk, LANES))
    sum_b = jnp.broadcast_to(jnp.where(low_half, 0.0, 1.0), (2 * blk, LANES))

    for g in range(N_KV_HEADS):
        chunk = g // 2
        kc = _head_rms_norm(kv[:, chunk * LANES:(chunk + 1) * LANES], kw2, ones_bd)
        vc = kv[:, kvw + chunk * LANES: kvw + (chunk + 1) * LANES]
        kr = pltpu.roll(kc, HEAD_DIM, axis=1)
        vr = pltpu.roll(vc, HEAD_DIM, axis=1)
        if g % 2 == 0:
            k_lo, k_hi, v_lo, v_hi = kc, kr, vc, vr
        else:
            k_lo, k_hi, v_lo, v_hi = kr, kc, vr, vc
        k_bd = jnp.concatenate([jnp.where(low_half, k_lo, 0.0),
                                jnp.where(low_half, 0.0, k_hi)], axis=0).astype(BF16)
        v_bd = jnp.concatenate(
            [jnp.concatenate([jnp.where(low_half, v_lo, 0.0), sum_a], axis=1),
             jnp.concatenate([jnp.where(low_half, 0.0, v_hi), sum_b], axis=1)],
            axis=0).astype(BF16)

        for jj in range(Q_PER_KV // 2):
            j = g * (Q_PER_KV // 2) + jj
            qp = q_ref[0, :, j * LANES:(j + 1) * LANES].astype(F32)
            qn = _head_rms_norm(qp, qw_ref[...], ones_bd).astype(BF16)
            s = lax.dot_general(qn, k_bd, (((1,), (1,)), ((), ())),
                                preferred_element_type=F32)
            s = s + bias_ref[0, j]
            sink_a = sink_ref[2 * j]
            sink_b = sink_ref[2 * j + 1]
            m_a = jnp.maximum(jnp.max(s[:, :2 * blk], axis=-1, keepdims=True), sink_a)
            m_b = jnp.maximum(jnp.max(s[:, 2 * blk:], axis=-1, keepdims=True), sink_b)
            p = jnp.concatenate([jnp.exp(s[:, :2 * blk] - m_a),
                                 jnp.exp(s[:, 2 * blk:] - m_b)], axis=1).astype(BF16)
            ol = jnp.dot(p, v_bd, preferred_element_type=F32)
            esink = jnp.where(low_half, jnp.exp(sink_a - m_a), jnp.exp(sink_b - m_b))
            o = ol[:, :LANES] / (ol[:, LANES:] + esink)
            o_ref[0, :, j * LANES:(j + 1) * LANES] = o.astype(o_ref.dtype)


def _attn_bias():
    blk = ATTN_BLOCK
    i = jnp.arange(blk)[:, None]
    j = jnp.arange(2 * blk)[None, :]
    dist = i + blk - j
    band = (dist >= 0) & (dist < WINDOW)
    slopes = 2.0 ** (-8.0 * jnp.arange(1, N_Q_HEADS + 1, dtype=F32) / N_Q_HEADS)
    alibi = -slopes[:, None, None] * dist.astype(F32)[None]
    later = jnp.where(band[None], alibi, NEG_INF)
    first = jnp.where((band & (j >= blk))[None], alibi, NEG_INF)
    tab = jnp.stack([first, later])
    tab = tab.reshape(2, N_Q_HEADS // 2, 2, blk, 2 * blk)
    return jnp.transpose(tab, (0, 1, 3, 2, 4)).reshape(2, N_Q_HEADS // 2, blk, 4 * blk)


def _attention(proj3, sinks, q_norm_w, k_norm_w, q_col, kv_col):
    b, s, _ = proj3.shape
    aw = N_Q_HEADS * HEAD_DIM
    kvw2 = 2 * N_KV_HEADS * HEAD_DIM
    nb = s // ATTN_BLOCK
    bias = _attn_bias()
    qw2 = (jnp.tile(q_norm_w, 2) * (HEAD_DIM ** -0.5)).reshape(1, LANES)
    kw2 = jnp.tile(k_norm_w, 2).reshape(1, LANES)
    grid_spec = pltpu.PrefetchScalarGridSpec(
        num_scalar_prefetch=1,
        grid=(b, nb),
        in_specs=[pl.BlockSpec((1, ATTN_BLOCK, aw), lambda bi, n, sk: (bi, n, q_col // aw)),
                  pl.BlockSpec((1, ATTN_BLOCK, kvw2), lambda bi, n, sk: (bi, n, kv_col // kvw2)),
                  pl.BlockSpec((1, ATTN_BLOCK, kvw2),
                               lambda bi, n, sk: (bi, jnp.maximum(n - 1, 0), kv_col // kvw2)),
                  pl.BlockSpec((1, N_Q_HEADS // 2, ATTN_BLOCK, 4 * ATTN_BLOCK),
                               lambda bi, n, sk: (jnp.minimum(n, 1), 0, 0, 0)),
                  pl.BlockSpec((1, LANES), lambda bi, n, sk: (0, 0)),
                  pl.BlockSpec((1, LANES), lambda bi, n, sk: (0, 0))],
        out_specs=pl.BlockSpec((1, ATTN_BLOCK, aw), lambda bi, n, sk: (bi, n, 0)),
    )
    return pl.pallas_call(
        _attn_kernel,
        out_shape=jax.ShapeDtypeStruct((b, s, aw), BF16),
        grid_spec=grid_spec,
        compiler_params=_params(("parallel", "arbitrary"), 40),
    )(sinks, proj3, proj3, proj3, bias, qw2, kw2)


def _pool_kernel(pc_ref, ph_ref, w_ref, ps_ref, o_ref):
    n = pl.program_id(1)
    ts = pc_ref.shape[1]
    cg = w_ref.shape[1]
    cur = pc_ref[0].astype(F32)
    halo = jnp.where(n > 0, ph_ref[0].astype(F32), 0.0)
    ext = jnp.concatenate([halo, cur], axis=0)
    t1 = (lax.broadcasted_iota(I32, (ts, 1), 0) + n * ts + 1).astype(F32)
    for g, win in enumerate(POOL_WINDOWS):
        acc = ext[:, g * cg:(g + 1) * cg]
        k = 1
        while k < win:
            acc = acc + pltpu.roll(acc, k, axis=0)
            k *= 2
        mean = acc[POOL_HALO:] / jnp.minimum(t1, float(win))
        d = (mean - cur[:, g * cg:(g + 1) * cg]).astype(BF16)
        y = jnp.dot(d, w_ref[g], preferred_element_type=F32) * ps_ref[:, g * cg:(g + 1) * cg]
        o_ref[0, :, g * cg:(g + 1) * cg] = y.astype(o_ref.dtype)


def _pool(proj3, w_pool, pool_scale, p_col):
    b, s, _ = proj3.shape
    ng, cg, _ = w_pool.shape
    pw = ng * cg
    ts = _pick(s, (512, 256, 128))
    return pl.pallas_call(
        _pool_kernel,
        out_shape=jax.ShapeDtypeStruct((b, s, pw), BF16),
        grid=(b, s // ts),
        in_specs=[pl.BlockSpec((1, ts, pw), lambda bi, n: (bi, n, p_col // pw)),
                  pl.BlockSpec((1, POOL_HALO, pw),
                               lambda bi, n: (bi, jnp.maximum(n * (ts // POOL_HALO) - 1, 0),
                                              p_col // pw)),
                  pl.BlockSpec((ng, cg, cg), lambda bi, n: (0, 0, 0)),
                  pl.BlockSpec((1, pw), lambda bi, n: (0, 0))],
        out_specs=pl.BlockSpec((1, ts, pw), lambda bi, n: (bi, n, 0)),
        compiler_params=_params(("parallel", "arbitrary"), 40),
    )(proj3, proj3, w_pool, pool_scale)


def _mix_kernel(a_ref, b_ref, wa_ref, wb_ref, ga_ref, gb_ref, o_ref):
    ya = jnp.dot(a_ref[...], wa_ref[...], preferred_element_type=F32)
    yb = jnp.dot(b_ref[...], wb_ref[...], preferred_element_type=F32)
    ga = jax.nn.sigmoid(ga_ref[...].astype(F32))
    gb = jax.nn.sigmoid(gb_ref[...].astype(F32))
    o_ref[...] = (ga * ya + gb * yb).astype(o_ref.dtype)


def _mix(y_attn, y_pool, wa, wb, proj, ga_col, gb_col):
    t, aw = y_attn.shape
    pw = y_pool.shape[1]
    d = wa.shape[1]
    tm = _pick(t, (512, 256, 128))
    tn = _pick(d, (1024, 512, 256, 128))
    return pl.pallas_call(
        _mix_kernel,
        out_shape=jax.ShapeDtypeStruct((t, d), BF16),
        grid=(t // tm, d // tn),
        in_specs=[pl.BlockSpec((tm, aw), lambda i, j: (i, 0)),
                  pl.BlockSpec((tm, pw), lambda i, j: (i, 0)),
                  pl.BlockSpec((aw, tn), lambda i, j: (0, j)),
                  pl.BlockSpec((pw, tn), lambda i, j: (0, j)),
                  pl.BlockSpec((tm, tn), lambda i, j: (i, ga_col // tn + j)),
                  pl.BlockSpec((tm, tn), lambda i, j: (i, gb_col // tn + j))],
        out_specs=pl.BlockSpec((tm, tn), lambda i, j: (i, j)),
        compiler_params=_params(("parallel", "arbitrary"), 56),
    )(y_attn, y_pool, wa, wb, proj, proj)


def _out_kernel(m_ref, w_ref, x_ref, g_ref, o_ref):
    y = jnp.dot(m_ref[...], w_ref[...], preferred_element_type=F32)
    o_ref[...] = x_ref[...] + g_ref[0] * y


def _out_proj(mixed, w, x2, gate, seq):
    t, d = x2.shape
    tm = _pick(seq, (512, 256, 128))
    tn = _pick(d, (1024, 512, 256, 128))
    rows_per_batch = seq // tm
    return pl.pallas_call(
        _out_kernel,
        out_shape=jax.ShapeDtypeStruct((t, d), F32),
        grid=(t // tm, d // tn),
        in_specs=[pl.BlockSpec((tm, d), lambda i, j: (i, 0)),
                  pl.BlockSpec((d, tn), lambda i, j: (0, j)),
                  pl.BlockSpec((tm, tn), lambda i, j: (i, j)),
                  pl.BlockSpec((1, 1, tn), lambda i, j: (i // rows_per_batch, 0, j))],
        out_specs=pl.BlockSpec((tm, tn), lambda i, j: (i, j)),
        compiler_params=_params(("parallel", "arbitrary"), 56),
    )(mixed, w, x2, gate)


def _first_argmax(v, iota, n):
    m = jnp.max(v, axis=0, keepdims=True)
    idx = jnp.min(jnp.where(v == m, iota, n), axis=0, keepdims=True)
    return m, idx


def _router_kernel(x_ref, nw_ref, sc_ref, sh_ref, wr_ref, br_ref,
                   hp_ref, mi_ref, mf_ref, cnt_ref, carry_ref, h_ref):
    step = pl.program_id(0)
    tm, d = x_ref.shape
    half = d // 2

    @pl.when(step == 0)
    def _():
        carry_ref[...] = jnp.zeros_like(carry_ref)

    def body(r, carry):
        rows = pl.ds(pl.multiple_of(r * NORM_ROWS, NORM_ROWS), NORM_ROWS)
        h = _modulated_norm(x_ref[rows, :], nw_ref[...], sc_ref[0], sh_ref[0])
        h_ref[rows, :] = h
        bits = lax.bitcast_convert_type(h.astype(BF16).astype(F32), U32)
        hp_ref[rows, :] = (bits[:, :half] >> 16) | (bits[:, half:] & jnp.uint32(0xFFFF0000))
        return carry

    lax.fori_loop(0, tm // NORM_ROWS, body, 0)

    logits = lax.dot_general(wr_ref[...], h_ref[...], (((1,), (1,)), ((), ())),
                             precision=lax.Precision.HIGHEST,
                             preferred_element_type=F32) + br_ref[...]
    iota8 = lax.broadcasted_iota(I32, (N_GROUPS, tm), 0)
    gl = logits[:N_GROUPS]
    g_max, g_sel = _first_argmax(gl, iota8, N_GROUPS)
    p_group = 1.0 / jnp.sum(jnp.exp(gl - g_max), axis=0, keepdims=True)
    e_in = jnp.zeros((EXPERTS_PER_GROUP, tm), F32)
    for g in range(N_GROUPS):
        lo = N_GROUPS + g * EXPERTS_PER_GROUP
        e_in = e_in + jnp.where(g_sel == g, logits[lo:lo + EXPERTS_PER_GROUP], 0.0)
    v0, i0 = _first_argmax(e_in, iota8, EXPERTS_PER_GROUP)
    rest = jnp.where(iota8 == i0, -jnp.inf, e_in)
    v1, i1 = _first_argmax(rest, iota8, EXPERTS_PER_GROUP)
    t = jnp.exp(v1 - v0)
    w0 = p_group / (1.0 + t)
    w1 = p_group * t / (1.0 + t)
    e0 = g_sel * EXPERTS_PER_GROUP + i0
    e1 = g_sel * EXPERTS_PER_GROUP + i1

    iota_e = lax.broadcasted_iota(I32, (N_EXPERTS, tm), 0)
    hit0 = iota_e == e0
    hit1 = iota_e == e1
    onehot = jnp.where(hit0 | hit1, 1.0, 0.0).astype(BF16)
    rr = lax.broadcasted_iota(I32, (tm, tm), 0)
    cc = lax.broadcasted_iota(I32, (tm, tm), 1)
    before = jnp.where(rr < cc, 1.0, 0.0).astype(BF16)
    prior = jnp.dot(onehot, before, preferred_element_type=F32) + carry_ref[:, :1]
    r0 = jnp.sum(jnp.where(hit0, prior, 0.0), axis=0, keepdims=True)
    r1 = jnp.sum(jnp.where(hit1, prior, 0.0), axis=0, keepdims=True)
    carry_ref[...] += jnp.dot(onehot, jnp.ones((tm, LANES), BF16), preferred_element_type=F32)

    zi = jnp.zeros((1, tm), I32)
    zf = jnp.zeros((1, tm), F32)
    mi_ref[...] = jnp.concatenate([e0, e1, r0.astype(I32), r1.astype(I32), zi, zi, zi, zi], axis=0)
    mf_ref[...] = jnp.concatenate([w0, w1, zf, zf, zf, zf, zf, zf], axis=0)
    cnt_ref[...] = carry_ref[...]


def _router(x1, nw, scale, shift, wr_t, br, seq):
    t, d = x1.shape
    tm = _pick(seq, (512, 256, 128))
    rows_per_batch = seq // tm
    return pl.pallas_call(
        _router_kernel,
        out_shape=(jax.ShapeDtypeStruct((t, d // 2), U32),
                   jax.ShapeDtypeStruct((8, t), I32),
                   jax.ShapeDtypeStruct((8, t), F32),
                   jax.ShapeDtypeStruct((N_EXPERTS, LANES), F32)),
        grid=(t // tm,),
        in_specs=[pl.BlockSpec((tm, d), lambda i: (i, 0)),
                  pl.BlockSpec((1, d), lambda i: (0, 0)),
                  pl.BlockSpec((1, 1, d), lambda i: (i // rows_per_batch, 0, 0)),
                  pl.BlockSpec((1, 1, d), lambda i: (i // rows_per_batch, 0, 0)),
                  pl.BlockSpec((N_ROUTER, d), lambda i: (0, 0)),
                  pl.BlockSpec((N_ROUTER, 1), lambda i: (0, 0))],
        out_specs=(pl.BlockSpec((tm, d // 2), lambda i: (i, 0)),
                   pl.BlockSpec((8, tm), lambda i: (0, i)),
                   pl.BlockSpec((8, tm), lambda i: (0, i)),
                   pl.BlockSpec((N_EXPERTS, LANES), lambda i: (0, 0))),
        scratch_shapes=[pltpu.VMEM((N_EXPERTS, LANES), F32), pltpu.VMEM((tm, d), F32)],
        compiler_params=_params(("arbitrary",), 56),
    )(x1, nw, scale, shift, wr_t, br)


def _dispatch_kernel(dest_ref, hp_ref, xs_ref, sem, *, chunk):
    base = pl.program_id(0) * chunk

    def issue(i, carry):
        tok = base + i
        for k in range(TOP_K):
            pltpu.make_async_copy(hp_ref.at[pl.ds(tok, 1)],
                                  xs_ref.at[pl.ds(dest_ref[TOP_K * tok + k], 1)], sem).start()
        return carry

    lax.fori_loop(0, chunk, issue, 0)
    pltpu.make_async_copy(hp_ref.at[pl.ds(0, TOP_K * chunk)],
                          xs_ref.at[pl.ds(0, TOP_K * chunk)], sem).wait()


def _dispatch(dest_flat, hp, cap):
    t, half = hp.shape
    chunk = _pick(t, (256, 128))
    grid_spec = pltpu.PrefetchScalarGridSpec(
        num_scalar_prefetch=1,
        grid=(t // chunk,),
        in_specs=[pl.BlockSpec(memory_space=pl.ANY)],
        out_specs=pl.BlockSpec(memory_space=pl.ANY),
        scratch_shapes=[pltpu.SemaphoreType.DMA(())],
    )
    return pl.pallas_call(
        functools.partial(_dispatch_kernel, chunk=chunk),
        out_shape=jax.ShapeDtypeStruct((cap, half), U32),
        grid_spec=grid_spec,
        compiler_params=pltpu.CompilerParams(dimension_semantics=("arbitrary",),
                                             has_side_effects=True),
    )(dest_flat, hp)


def _ffn_kernel(be_ref, nv_ref, nu_ref, xs_ref, wg_ref, wu_ref, wd_ref, o_ref):
    b = pl.program_id(0)

    @pl.when(b < nu_ref[0])
    def _():
        blk, half = xs_ref.shape
        row = lax.broadcasted_iota(I32, (blk, 1), 0)
        xp = jnp.where(row < nv_ref[b], xs_ref[...], jnp.uint32(0))
        lo = lax.bitcast_convert_type(xp << 16, F32).astype(BF16)
        hi = lax.bitcast_convert_type(xp & jnp.uint32(0xFFFF0000), F32).astype(BF16)
        a = (jnp.dot(lo, wg_ref[0, :half], preferred_element_type=F32)
             + jnp.dot(hi, wg_ref[0, half:], preferred_element_type=F32))
        u = (jnp.dot(lo, wu_ref[0, :half], preferred_element_type=F32)
             + jnp.dot(hi, wu_ref[0, half:], preferred_element_type=F32))
        mid = (a * jax.nn.sigmoid(a) * u).astype(BF16)
        o_ref[...] = jnp.dot(mid, wd_ref[0], preferred_element_type=F32)


def _experts(blk_e, blk_valid, n_used, xs, wg, wu, wd):
    cap, half = xs.shape
    d = 2 * half
    ff = wg.shape[2]
    nblk = cap // EXPERT_BLOCK

    def row_map(b, be, nv, nu):
        return (jnp.minimum(b, nu[0] - 1), 0)

    def w_map(b, be, nv, nu):
        return (be[jnp.minimum(b, nu[0] - 1)], 0, 0)

    grid_spec = pltpu.PrefetchScalarGridSpec(
        num_scalar_prefetch=3,
        grid=(nblk,),
        in_specs=[pl.BlockSpec((EXPERT_BLOCK, half), row_map),
                  pl.BlockSpec((1, d, ff), w_map),
                  pl.BlockSpec((1, d, ff), w_map),
                  pl.BlockSpec((1, ff, d), w_map)],
        out_specs=pl.BlockSpec((EXPERT_BLOCK, d), row_map),
    )
    return pl.pallas_call(
        _ffn_kernel,
        out_shape=jax.ShapeDtypeStruct((cap, d), F32),
        grid_spec=grid_spec,
        compiler_params=_params(("arbitrary",), 56),
    )(blk_e, blk_valid, n_used, xs, wg, wu, wd)


def _combine_kernel(dest_ref, x_ref, g_ref, w_ref, ys_ref, o_ref, ybuf, sem, *, chunk):
    i = pl.program_id(0)
    n = pl.num_programs(0)

    def issue(step, slot):
        base = step * chunk

        def body(r, carry):
            tok = base + r
            for k in range(TOP_K):
                pltpu.make_async_copy(ys_ref.at[pl.ds(dest_ref[TOP_K * tok + k], 1)],
                                      ybuf.at[slot, k, pl.ds(r, 1)], sem.at[slot]).start()
            return carry

        lax.fori_loop(0, chunk, body, 0)

    @pl.when(i == 0)
    def _():
        issue(0, 0)

    slot = i % 2

    @pl.when(i + 1 < n)
    def _():
        issue(i + 1, 1 - slot)

    for k in range(TOP_K):
        pltpu.make_async_copy(ys_ref.at[pl.ds(0, chunk)], ybuf.at[slot, k], sem.at[slot]).wait()
    w = w_ref[...]
    y = w[:, 0:1] * ybuf[slot, 0] + w[:, 1:2] * ybuf[slot, 1]
    o_ref[...] = x_ref[...] + g_ref[0] * y


def _combine(dest_flat, x1, gate, w_rows, ys, seq):
    t, d = x1.shape
    chunk = _pick(seq, (256, 128))
    rows_per_batch = seq // chunk
    grid_spec = pltpu.PrefetchScalarGridSpec(
        num_scalar_prefetch=1,
        grid=(t // chunk,),
        in_specs=[pl.BlockSpec((chunk, d), lambda i, ds: (i, 0)),
                  pl.BlockSpec((1, 1, d), lambda i, ds: (i // rows_per_batch, 0, 0)),
                  pl.BlockSpec((chunk, 8), lambda i, ds: (i, 0)),
                  pl.BlockSpec(memory_space=pl.ANY)],
        out_specs=pl.BlockSpec((chunk, d), lambda i, ds: (i, 0)),
        scratch_shapes=[pltpu.VMEM((2, TOP_K, chunk, d), F32),
                        pltpu.SemaphoreType.DMA((2,))],
    )
    return pl.pallas_call(
        functools.partial(_combine_kernel, chunk=chunk),
        out_shape=jax.ShapeDtypeStruct((t, d), F32),
        grid_spec=grid_spec,
        compiler_params=_params(("arbitrary",), 56),
    )(dest_flat, x1, gate, w_rows, ys)


def _layer(x, c, w_ada, b_ada, norm1_w, w_in, q_norm_w, k_norm_w, sinks, w_pool, pool_scale,
           w_attn_up, w_pool_up, w_out, norm2_w, w_router_group, b_router_group,
           w_router_expert, b_router_expert, w_gate, w_up, w_down):
    b, s, d = x.shape
    t = b * s
    aw = w_attn_up.shape[0]
    pw = w_pool_up.shape[0]
    kvw = N_KV_HEADS * HEAD_DIM
    x2 = x.reshape(t, d)

    c8 = jnp.zeros((8, d), F32).at[:b].set(c)
    ada = _ada(c8, w_ada, b_ada.reshape(1, 6 * d))[:b]
    shift1, scale1, gate1, shift2, scale2, gate2 = [a.reshape(b, 1, d) for a in jnp.split(ada, 6, axis=-1)]

    o_k = aw
    o_p = aw + 2 * kvw
    o_ga = o_p + pw
    o_gb = o_ga + d
    w_in_p = jnp.concatenate([w_in[:, o_ga:o_gb], w_in[:, o_gb:], w_in[:, :aw],
                              w_in[:, o_p:o_ga], w_in[:, o_k:o_p]], axis=1).astype(BF16)
    ga_col, gb_col, q_col, p_col, kv_col = 0, d, 2 * d, 2 * d + aw, 2 * d + aw + pw
    proj = _in_proj(x2, norm1_w.reshape(1, d), scale1, shift1, w_in_p, s)
    proj3 = proj.reshape(b, s, proj.shape[1])

    y_attn = _attention(proj3, sinks, q_norm_w, k_norm_w, q_col, kv_col).reshape(t, aw)
    y_pool = _pool(proj3, w_pool.astype(BF16), pool_scale.reshape(1, pw), p_col).reshape(t, pw)
    mixed = _mix(y_attn, y_pool, w_attn_up.astype(BF16), w_pool_up.astype(BF16), proj, ga_col, gb_col)
    x1 = _out_proj(mixed, w_out.astype(BF16), x2, gate1, s)

    wr_t = jnp.concatenate([w_router_group, w_router_expert], axis=1).T
    br = jnp.concatenate([b_router_group, b_router_expert]).reshape(N_ROUTER, 1)
    hp, meta_i, meta_f, cnt = _router(x1, norm2_w.reshape(1, d), scale2, shift2, wr_t, br, s)

    counts = cnt[:, 0].astype(I32)
    nblocks = (counts + EXPERT_BLOCK - 1) // EXPERT_BLOCK
    bend = jnp.cumsum(nblocks)
    bstart = bend - nblocks
    dest = bstart[meta_i[:TOP_K]] * EXPERT_BLOCK + meta_i[TOP_K:2 * TOP_K]
    dest_flat = dest.T.reshape(TOP_K * t)
    cap = TOP_K * t + N_EXPERTS * EXPERT_BLOCK
    nblk = cap // EXPERT_BLOCK
    blk_ids = jnp.arange(nblk, dtype=I32)
    blk_e = jnp.minimum(jnp.searchsorted(bend, blk_ids, side='right'), N_EXPERTS - 1).astype(I32)
    blk_valid = jnp.clip(counts[blk_e] - (blk_ids - bstart[blk_e]) * EXPERT_BLOCK, 0, EXPERT_BLOCK)
    n_used = bend[-1:].astype(I32)

    xs = _dispatch(dest_flat, hp, cap)
    ys = _experts(blk_e, blk_valid.astype(I32), n_used, xs,
                  w_gate.astype(BF16), w_up.astype(BF16), w_down.astype(BF16))
    out = _combine(dest_flat, x1, gate2, meta_f.T, ys, s)
    return out.reshape(b, s, d)


def kernel(x, c, w_ada, b_ada, norm1_w, w_in, q_norm_w, k_norm_w, sinks, w_pool, pool_scale,
           w_attn_up, w_pool_up, w_out, norm2_w, w_router_group, b_router_group,
           w_router_expert, b_router_expert, w_gate, w_up, w_down):
    for l in range(w_ada.shape[0]):
        x = _layer(x, c, w_ada[l], b_ada[l], norm1_w[l], w_in[l], q_norm_w[l], k_norm_w[l],
                   sinks[l], w_pool[l], pool_scale[l], w_attn_up[l], w_pool_up[l], w_out[l],
                   norm2_w[l], w_router_group[l], b_router_group[l], w_router_expert[l],
                   b_router_expert[l], w_gate[l], w_up[l], w_down[l])
    return x
```

```python
import functools

import jax
import jax.numpy as jnp
import numpy as np
from jax import lax
from jax.experimental import pallas as pl
from jax.experimental.pallas import tpu as pltpu

F32 = jnp.float32
BF16 = jnp.bfloat16
U32 = jnp.uint32
I32 = jnp.int32

HEAD_DIM = 64
N_Q_HEADS = 32
N_KV_HEADS = 4
Q_PER_KV = N_Q_HEADS // N_KV_HEADS
ATTN_BLOCK = 128
WINDOW = 128
POOL_WINDOWS = (2, 4, 8, 16)
POOL_HALO = 16
N_GROUPS = 8
EXPERTS_PER_GROUP = 8
N_EXPERTS = N_GROUPS * EXPERTS_PER_GROUP
N_ROUTER = N_GROUPS + N_EXPERTS
TOP_K = 2
EPS = 1e-6
NEG_INF = -1e30

LANES = 128
EXPERT_BLOCK = 256
MIB = 1024 * 1024


def _params(semantics, vmem_mib):
    return pltpu.CompilerParams(dimension_semantics=semantics, vmem_limit_bytes=vmem_mib * MIB)


def _pick(n, candidates):
    for c in candidates:
        if n % c == 0:
            return c
    return n


def _ada_kernel(c_ref, w_ref, b_ref, o_ref):
    c = c_ref[...]
    s = (c * jax.nn.sigmoid(c)).astype(BF16)
    o_ref[...] = jnp.dot(s, w_ref[...].astype(BF16), preferred_element_type=F32) + b_ref[...]


def _ada(c8, w, b):
    d, n = w.shape
    tn = _pick(n, (1024, 512, 256, 128))
    return pl.pallas_call(
        _ada_kernel,
        out_shape=jax.ShapeDtypeStruct((8, n), F32),
        grid=(n // tn,),
        in_specs=[pl.BlockSpec((8, d), lambda j: (0, 0)),
                  pl.BlockSpec((d, tn), lambda j: (0, j)),
                  pl.BlockSpec((1, tn), lambda j: (0, j))],
        out_specs=pl.BlockSpec((8, tn), lambda j: (0, j)),
        compiler_params=_params(("arbitrary",), 56),
    )(c8, w, b)


def _modulated_norm(x, nw, scale, shift):
    ms = jnp.mean(x * x, axis=-1, keepdims=True)
    y = x * lax.rsqrt(ms + EPS) * nw
    return y * (1.0 + scale) + shift


NORM_ROWS = 64


def _in_kernel(x_ref, nw_ref, sc_ref, sh_ref, w_ref, o_ref, h_ref):
    @pl.when(pl.program_id(1) == 0)
    def _():
        def body(r, carry):
            rows = pl.ds(pl.multiple_of(r * NORM_ROWS, NORM_ROWS), NORM_ROWS)
            h_ref[rows, :] = _modulated_norm(x_ref[rows, :], nw_ref[...], sc_ref[0],
                                             sh_ref[0]).astype(BF16)
            return carry

        lax.fori_loop(0, x_ref.shape[0] // NORM_ROWS, body, 0)

    o_ref[...] = jnp.dot(h_ref[...], w_ref[...], preferred_element_type=F32).astype(o_ref.dtype)


def _in_proj(x2, nw, scale, shift, w, seq):
    t, d = x2.shape
    n = w.shape[1]
    tm = _pick(seq, (512, 256, 128))
    tn = _pick(n, (1280, 512, 256, 128))
    rows_per_batch = seq // tm
    return pl.pallas_call(
        _in_kernel,
        out_shape=jax.ShapeDtypeStruct((t, n), BF16),
        grid=(t // tm, n // tn),
        in_specs=[pl.BlockSpec((tm, d), lambda i, j: (i, 0)),
                  pl.BlockSpec((1, d), lambda i, j: (0, 0)),
                  pl.BlockSpec((1, 1, d), lambda i, j: (i // rows_per_batch, 0, 0)),
                  pl.BlockSpec((1, 1, d), lambda i, j: (i // rows_per_batch, 0, 0)),
                  pl.BlockSpec((d, tn), lambda i, j: (0, j))],
        out_specs=pl.BlockSpec((tm, tn), lambda i, j: (i, j)),
        scratch_shapes=[pltpu.VMEM((tm, d), BF16)],
        compiler_params=_params(("parallel", "arbitrary"), 56),
    )(x2, nw, scale, shift, w)


def _head_sumsq(v, ones_bd):
    sq = v * v
    hi = sq.astype(BF16)
    lo = (sq - hi.astype(F32)).astype(BF16)
    return (jnp.dot(hi, ones_bd, preferred_element_type=F32)
            + jnp.dot(lo, ones_bd, preferred_element_type=F32))


def _head_rms_norm(v, w2, ones_bd):
    ss = _head_sumsq(v, ones_bd)
    return v * lax.rsqrt(ss * (1.0 / HEAD_DIM) + EPS) * w2


def _attn_kernel(sink_ref, q_ref, kvc_ref, kvp_ref, bias_ref, qw_ref, kw_ref, o_ref):
    blk = ATTN_BLOCK
    kvw = N_KV_HEADS * HEAD_DIM
    lane = lax.broadcasted_iota(I32, (1, LANES), 1)
    low_half = lane < HEAD_DIM
    r = lax.broadcasted_iota(I32, (LANES, LANES), 0) // HEAD_DIM
    c = lax.broadcasted_iota(I32, (LANES, LANES), 1) // HEAD_DIM
    ones_bd = jnp.where(r == c, 1.0, 0.0).astype(BF16)

    kv = jnp.concatenate([kvp_ref[0], kvc_ref[0]], axis=0).astype(F32)
    kw2 = kw_ref[...]
    sum_a = jnp.broadcast_to(jnp.where(low_half, 1.0, 0.0), (2 * blk, LANES))
    sum_b = jnp.broadcast_to(jnp.where(low_half, 0.0, 1.0), (2 * blk, LANES))

    for g in range(N_KV_HEADS):
        chunk = g // 2
        kc = _head_rms_norm(kv[:, chunk * LANES:(chunk + 1) * LANES], kw2, ones_bd)
        vc = kv[:, kvw + chunk * LANES: kvw + (chunk + 1) * LANES]
        kr = pltpu.roll(kc, HEAD_DIM, axis=1)
        vr = pltpu.roll(vc, HEAD_DIM, axis=1)
        if g % 2 == 0:
            k_lo, k_hi, v_lo, v_hi = kc, kr, vc, vr
        else:
            k_lo, k_hi, v_lo, v_hi = kr, kc, vr, vc
        k_bd = jnp.concatenate([jnp.where(low_half, k_lo, 0.0),
                                jnp.where(low_half, 0.0, k_hi)], axis=0).astype(BF16)
        v_bd = jnp.concatenate(
            [jnp.concatenate([jnp.where(low_half, v_lo, 0.0), sum_a], axis=1),
             jnp.concatenate([jnp.where(low_half, 0.0, v_hi), sum_b], axis=1)],
            axis=0).astype(BF16)

        for jj in range(Q_PER_KV // 2):
            j = g * (Q_PER_KV // 2) + jj
            qp = q_ref[0, :, j * LANES:(j + 1) * LANES].astype(F32)
            qn = _head_rms_norm(qp, qw_ref[...], ones_bd).astype(BF16)
            s = lax.dot_general(qn, k_bd, (((1,), (1,)), ((), ())),
                                preferred_element_type=F32)
            s = s + bias_ref[0, j]
            sink_a = sink_ref[2 * j]
            sink_b = sink_ref[2 * j + 1]
            m_a = jnp.maximum(jnp.max(s[:, :2 * blk], axis=-1, keepdims=True), sink_a)
            m_b = jnp.maximum(jnp.max(s[:, 2 * blk:], axis=-1, keepdims=True), sink_b)
            p = jnp.concatenate([jnp.exp(s[:, :2 * blk] - m_a),
                                 jnp.exp(s[:, 2 * blk:] - m_b)], axis=1).astype(BF16)
            ol = jnp.dot(p, v_bd, preferred_element_type=F32)
            esink = jnp.where(low_half, jnp.exp(sink_a - m_a), jnp.exp(sink_b - m_b))
            o = ol[:, :LANES] / (ol[:, LANES:] + esink)
            o_ref[0, :, j * LANES:(j + 1) * LANES] = o.astype(o_ref.dtype)


def _attn_bias():
    blk = ATTN_BLOCK
    i = np.arange(blk)[:, None]
    j = np.arange(2 * blk)[None, :]
    dist = i + blk - j
    band = (dist >= 0) & (dist < WINDOW)
    slopes = (2.0 ** (-8.0 * np.arange(1, N_Q_HEADS + 1, dtype=np.float32) / N_Q_HEADS)).astype(np.float32)
    alibi = -slopes[:, None, None] * dist.astype(np.float32)[None]
    later = np.where(band[None], alibi, np.float32(NEG_INF))
    first = np.where((band & (j >= blk))[None], alibi, np.float32(NEG_INF))
    tab = np.stack([first, later]).astype(np.float32)
    tab = tab.reshape(2, N_Q_HEADS // 2, 2, blk, 2 * blk)
    return np.transpose(tab, (0, 1, 3, 2, 4)).reshape(2, N_Q_HEADS // 2, blk, 4 * blk)


def _attention(proj3, sinks, q_norm_w, k_norm_w, q_col, kv_col):
    b, s, _ = proj3.shape
    aw = N_Q_HEADS * HEAD_DIM
    kvw2 = 2 * N_KV_HEADS * HEAD_DIM
    nb = s // ATTN_BLOCK
    bias = _attn_bias()
    qw2 = (jnp.tile(q_norm_w, 2) * (HEAD_DIM ** -0.5)).reshape(1, LANES)
    kw2 = jnp.tile(k_norm_w, 2).reshape(1, LANES)
    grid_spec = pltpu.PrefetchScalarGridSpec(
        num_scalar_prefetch=1,
        grid=(b, nb),
        in_specs=[pl.BlockSpec((1, ATTN_BLOCK, aw), lambda bi, n, sk: (bi, n, q_col // aw)),
                  pl.BlockSpec((1, ATTN_BLOCK, kvw2), lambda bi, n, sk: (bi, n, kv_col // kvw2)),
                  pl.BlockSpec((1, ATTN_BLOCK, kvw2),
                               lambda bi, n, sk: (bi, jnp.maximum(n - 1, 0), kv_col // kvw2)),
                  pl.BlockSpec((1, N_Q_HEADS // 2, ATTN_BLOCK, 4 * ATTN_BLOCK),
                               lambda bi, n, sk: (jnp.minimum(n, 1), 0, 0, 0)),
                  pl.BlockSpec((1, LANES), lambda bi, n, sk: (0, 0)),
                  pl.BlockSpec((1, LANES), lambda bi, n, sk: (0, 0))],
        out_specs=pl.BlockSpec((1, ATTN_BLOCK, aw), lambda bi, n, sk: (bi, n, 0)),
    )
    return pl.pallas_call(
        _attn_kernel,
        out_shape=jax.ShapeDtypeStruct((b, s, aw), BF16),
        grid_spec=grid_spec,
        compiler_params=_params(("parallel", "arbitrary"), 40),
    )(sinks, proj3, proj3, proj3, bias, qw2, kw2)


def _pool_kernel(pc_ref, ph_ref, w_ref, ps_ref, o_ref):
    n = pl.program_id(1)
    ts = pc_ref.shape[1]
    cg = w_ref.shape[1]
    cur = pc_ref[0].astype(F32)
    halo = jnp.where(n > 0, ph_ref[0].astype(F32), 0.0)
    ext = jnp.concatenate([halo, cur], axis=0)
    t1 = (lax.broadcasted_iota(I32, (ts, 1), 0) + n * ts + 1).astype(F32)
    for g, win in enumerate(POOL_WINDOWS):
        acc = ext[:, g * cg:(g + 1) * cg]
        k = 1
        while k < win:
            acc = acc + pltpu.roll(acc, k, axis=0)
            k *= 2
        mean = acc[POOL_HALO:] / jnp.minimum(t1, float(win))
        d = (mean - cur[:, g * cg:(g + 1) * cg]).astype(BF16)
        y = jnp.dot(d, w_ref[g], preferred_element_type=F32) * ps_ref[:, g * cg:(g + 1) * cg]
        o_ref[0, :, g * cg:(g + 1) * cg] = y.astype(o_ref.dtype)


def _pool(proj3, w_pool, pool_scale, p_col):
    b, s, _ = proj3.shape
    ng, cg, _ = w_pool.shape
    pw = ng * cg
    ts = _pick(s, (512, 256, 128))
    return pl.pallas_call(
        _pool_kernel,
        out_shape=jax.ShapeDtypeStruct((b, s, pw), BF16),
        grid=(b, s // ts),
        in_specs=[pl.BlockSpec((1, ts, pw), lambda bi, n: (bi, n, p_col // pw)),
                  pl.BlockSpec((1, POOL_HALO, pw),
                               lambda bi, n: (bi, jnp.maximum(n * (ts // POOL_HALO) - 1, 0),
                                              p_col // pw)),
                  pl.BlockSpec((ng, cg, cg), lambda bi, n: (0, 0, 0)),
                  pl.BlockSpec((1, pw), lambda bi, n: (0, 0))],
        out_specs=pl.BlockSpec((1, ts, pw), lambda bi, n: (bi, n, 0)),
        compiler_params=_params(("parallel", "arbitrary"), 40),
    )(proj3, proj3, w_pool, pool_scale)


def _mix_kernel(a_ref, b_ref, wa_ref, wb_ref, ga_ref, gb_ref, o_ref):
    ya = jnp.dot(a_ref[...], wa_ref[...], preferred_element_type=F32)
    yb = jnp.dot(b_ref[...], wb_ref[...], preferred_element_type=F32)
    ga = jax.nn.sigmoid(ga_ref[...].astype(F32))
    gb = jax.nn.sigmoid(gb_ref[...].astype(F32))
    o_ref[...] = (ga * ya + gb * yb).astype(o_ref.dtype)


def _mix(y_attn, y_pool, wa, wb, proj, ga_col, gb_col):
    t, aw = y_attn.shape
    pw = y_pool.shape[1]
    d = wa.shape[1]
    tm = _pick(t, (512, 256, 128))
    tn = _pick(d, (1024, 512, 256, 128))
    return pl.pallas_call(
        _mix_kernel,
        out_shape=jax.ShapeDtypeStruct((t, d), BF16),
        grid=(t // tm, d // tn),
        in_specs=[pl.BlockSpec((tm, aw), lambda i, j: (i, 0)),
                  pl.BlockSpec((tm, pw), lambda i, j: (i, 0)),
                  pl.BlockSpec((aw, tn), lambda i, j: (0, j)),
                  pl.BlockSpec((pw, tn), lambda i, j: (0, j)),
                  pl.BlockSpec((tm, tn), lambda i, j: (i, ga_col // tn + j)),
                  pl.BlockSpec((tm, tn), lambda i, j: (i, gb_col // tn + j))],
        out_specs=pl.BlockSpec((tm, tn), lambda i, j: (i, j)),
        compiler_params=_params(("parallel", "arbitrary"), 56),
    )(y_attn, y_pool, wa, wb, proj, proj)


def _out_kernel(m_ref, w_ref, x_ref, g_ref, o_ref):
    y = jnp.dot(m_ref[...], w_ref[...], preferred_element_type=F32)
    o_ref[...] = x_ref[...] + g_ref[0] * y


def _out_proj(mixed, w, x2, gate, seq):
    t, d = x2.shape
    tm = _pick(seq, (512, 256, 128))
    tn = _pick(d, (1024, 512, 256, 128))
    rows_per_batch = seq // tm
    return pl.pallas_call(
        _out_kernel,
        out_shape=jax.ShapeDtypeStruct((t, d), F32),
        grid=(t // tm, d // tn),
        in_specs=[pl.BlockSpec((tm, d), lambda i, j: (i, 0)),
                  pl.BlockSpec((d, tn), lambda i, j: (0, j)),
                  pl.BlockSpec((tm, tn), lambda i, j: (i, j)),
                  pl.BlockSpec((1, 1, tn), lambda i, j: (i // rows_per_batch, 0, j))],
        out_specs=pl.BlockSpec((tm, tn), lambda i, j: (i, j)),
        compiler_params=_params(("parallel", "arbitrary"), 56),
    )(mixed, w, x2, gate)


def _first_argmax(v, iota, n):
    m = jnp.max(v, axis=0, keepdims=True)
    idx = jnp.min(jnp.where(v == m, iota, n), axis=0, keepdims=True)
    return m, idx


def _router_kernel(x_ref, nw_ref, sc_ref, sh_ref, whi_ref, wlo_ref, br_ref,
                   hp_ref, mi_ref, mf_ref, cnt_ref, carry_ref, hhi_ref, hlo_ref):
    step = pl.program_id(0)
    tm, d = x_ref.shape
    half = d // 2

    @pl.when(step == 0)
    def _():
        carry_ref[...] = jnp.zeros_like(carry_ref)

    def body(r, carry):
        rows = pl.ds(pl.multiple_of(r * NORM_ROWS, NORM_ROWS), NORM_ROWS)
        h = _modulated_norm(x_ref[rows, :], nw_ref[...], sc_ref[0], sh_ref[0])
        hi = h.astype(BF16)
        hi32 = hi.astype(F32)
        hhi_ref[rows, :] = hi
        hlo_ref[rows, :] = (h - hi32).astype(BF16)
        bits = lax.bitcast_convert_type(hi32, U32)
        hp_ref[rows, :] = (bits[:, :half] >> 16) | (bits[:, half:] & jnp.uint32(0xFFFF0000))
        return carry

    lax.fori_loop(0, tm // NORM_ROWS, body, 0)

    lt = (jnp.dot(hhi_ref[...], whi_ref[...], preferred_element_type=F32)
          + jnp.dot(hlo_ref[...], whi_ref[...], preferred_element_type=F32)
          + jnp.dot(hhi_ref[...], wlo_ref[...], preferred_element_type=F32))
    logits = lt.T[:N_ROUTER] + br_ref[...]
    iota8 = lax.broadcasted_iota(I32, (N_GROUPS, tm), 0)
    gl = logits[:N_GROUPS]
    g_max, g_sel = _first_argmax(gl, iota8, N_GROUPS)
    p_group = 1.0 / jnp.sum(jnp.exp(gl - g_max), axis=0, keepdims=True)
    e_in = jnp.zeros((EXPERTS_PER_GROUP, tm), F32)
    for g in range(N_GROUPS):
        lo = N_GROUPS + g * EXPERTS_PER_GROUP
        e_in = e_in + jnp.where(g_sel == g, logits[lo:lo + EXPERTS_PER_GROUP], 0.0)
    v0, i0 = _first_argmax(e_in, iota8, EXPERTS_PER_GROUP)
    rest = jnp.where(iota8 == i0, -jnp.inf, e_in)
    v1, i1 = _first_argmax(rest, iota8, EXPERTS_PER_GROUP)
    t = jnp.exp(v1 - v0)
    w0 = p_group / (1.0 + t)
    w1 = p_group * t / (1.0 + t)
    e0 = g_sel * EXPERTS_PER_GROUP + i0
    e1 = g_sel * EXPERTS_PER_GROUP + i1

    iota_e = lax.broadcasted_iota(I32, (N_EXPERTS, tm), 0)
    hit0 = iota_e == e0
    hit1 = iota_e == e1
    onehot = jnp.where(hit0 | hit1, 1.0, 0.0).astype(BF16)
    rr = lax.broadcasted_iota(I32, (tm, tm), 0)
    cc = lax.broadcasted_iota(I32, (tm, tm), 1)
    before = jnp.where(rr < cc, 1.0, 0.0).astype(BF16)
    prior = jnp.dot(onehot, before, preferred_element_type=F32) + carry_ref[:, :1]
    r0 = jnp.sum(jnp.where(hit0, prior, 0.0), axis=0, keepdims=True)
    r1 = jnp.sum(jnp.where(hit1, prior, 0.0), axis=0, keepdims=True)
    carry_ref[...] += jnp.dot(onehot, jnp.ones((tm, LANES), BF16), preferred_element_type=F32)

    zi = jnp.zeros((1, tm), I32)
    zf = jnp.zeros((1, tm), F32)
    mi_ref[...] = jnp.concatenate([e0, e1, r0.astype(I32), r1.astype(I32), zi, zi, zi, zi], axis=0)
    mf_ref[...] = jnp.concatenate([w0, w1, zf, zf, zf, zf, zf, zf], axis=0)
    cnt_ref[...] = carry_ref[...]


def _router(x1, nw, scale, shift, w_hi, w_lo, br, seq):
    t, d = x1.shape
    tm = _pick(seq, (512, 256, 128))
    rows_per_batch = seq // tm
    return pl.pallas_call(
        _router_kernel,
        out_shape=(jax.ShapeDtypeStruct((t, d // 2), U32),
                   jax.ShapeDtypeStruct((8, t), I32),
                   jax.ShapeDtypeStruct((8, t), F32),
                   jax.ShapeDtypeStruct((N_EXPERTS, LANES), F32)),
        grid=(t // tm,),
        in_specs=[pl.BlockSpec((tm, d), lambda i: (i, 0)),
                  pl.BlockSpec((1, d), lambda i: (0, 0)),
                  pl.BlockSpec((1, 1, d), lambda i: (i // rows_per_batch, 0, 0)),
                  pl.BlockSpec((1, 1, d), lambda i: (i // rows_per_batch, 0, 0)),
                  pl.BlockSpec((d, LANES), lambda i: (0, 0)),
                  pl.BlockSpec((d, LANES), lambda i: (0, 0)),
                  pl.BlockSpec((N_ROUTER, 1), lambda i: (0, 0))],
        out_specs=(pl.BlockSpec((tm, d // 2), lambda i: (i, 0)),
                   pl.BlockSpec((8, tm), lambda i: (0, i)),
                   pl.BlockSpec((8, tm), lambda i: (0, i)),
                   pl.BlockSpec((N_EXPERTS, LANES), lambda i: (0, 0))),
        scratch_shapes=[pltpu.VMEM((N_EXPERTS, LANES), F32), pltpu.VMEM((tm, d), BF16),
                        pltpu.VMEM((tm, d), BF16)],
        compiler_params=_params(("arbitrary",), 56),
    )(x1, nw, scale, shift, w_hi, w_lo, br)


PLAN_LANES = 2048
DMA_UNROLL = 4


def _plan_kernel(mi_ref, cnt_ref, dest_ref, tab_ref):
    cnt = cnt_ref[...]
    nb = jnp.floor((cnt + (EXPERT_BLOCK - 1)) * (1.0 / EXPERT_BLOCK))
    r = lax.broadcasted_iota(I32, (N_EXPERTS, N_EXPERTS), 0)
    c = lax.broadcasted_iota(I32, (N_EXPERTS, N_EXPERTS), 1)
    incl = jnp.where(c <= r, 1.0, 0.0).astype(BF16)
    bend = jnp.dot(incl, nb.astype(BF16), preferred_element_type=F32)
    bstart = bend - nb
    first_row = bstart[:, :1] * EXPERT_BLOCK

    tl = mi_ref.shape[1]
    iota_e = lax.broadcasted_iota(I32, (N_EXPERTS, tl), 0)
    for k in range(TOP_K):
        hit = iota_e == mi_ref[k:k + 1, :]
        base = jnp.sum(jnp.where(hit, first_row, 0.0), axis=0, keepdims=True)
        dest_ref[k:k + 1, :] = base.astype(I32) + mi_ref[TOP_K + k:TOP_K + k + 1, :]

    nl = tab_ref.shape[1]
    bidx = lax.broadcasted_iota(I32, (1, nl), 1).astype(F32)
    blk_e = jnp.sum(jnp.where(bend[:, :1] <= bidx, 1.0, 0.0), axis=0, keepdims=True)
    blk_e = jnp.minimum(blk_e, N_EXPERTS - 1.0)
    hit = lax.broadcasted_iota(I32, (N_EXPERTS, nl), 0) == blk_e.astype(I32)
    cnt_b = jnp.sum(jnp.where(hit, cnt[:, :1], 0.0), axis=0, keepdims=True)
    start_b = jnp.sum(jnp.where(hit, bstart[:, :1], 0.0), axis=0, keepdims=True)
    valid = jnp.clip(cnt_b - (bidx - start_b) * EXPERT_BLOCK, 0.0, float(EXPERT_BLOCK))
    n_used = jnp.broadcast_to(jnp.max(bend[:, :1], axis=0, keepdims=True), (1, nl))
    zero = jnp.zeros((1, nl), F32)
    tab_ref[...] = jnp.concatenate([blk_e, valid, n_used, zero, zero, zero, zero, zero],
                                   axis=0).astype(I32)


def _plan(meta_i, cnt, nblk):
    t = meta_i.shape[1]
    tl = _pick(t, (PLAN_LANES, 1024, 512, 256, 128))
    nl = pl.cdiv(nblk, LANES) * LANES
    return pl.pallas_call(
        _plan_kernel,
        out_shape=(jax.ShapeDtypeStruct((TOP_K, t), I32), jax.ShapeDtypeStruct((8, nl), I32)),
        grid=(t // tl,),
        in_specs=[pl.BlockSpec((8, tl), lambda i: (0, i)),
                  pl.BlockSpec((N_EXPERTS, LANES), lambda i: (0, 0))],
        out_specs=(pl.BlockSpec((TOP_K, tl), lambda i: (0, i)),
                   pl.BlockSpec((8, nl), lambda i: (0, 0))),
        compiler_params=_params(("arbitrary",), 32),
    )(meta_i, cnt)


def _dispatch_kernel(dest_ref, hp_ref, xs_ref, sem, *, chunk, t):
    base = pl.program_id(0) * chunk

    def issue(i, carry):
        for k in range(TOP_K):
            pltpu.make_async_copy(hp_ref.at[pl.ds(i, 1)],
                                  xs_ref.at[pl.ds(dest_ref[k * t + base + i], 1)],
                                  sem).start(priority=k)
        return carry

    lax.fori_loop(0, chunk, issue, 0, unroll=DMA_UNROLL)
    for k in range(TOP_K):
        pltpu.make_async_copy(hp_ref, xs_ref.at[pl.ds(0, chunk)], sem).wait()


def _dispatch(dest_flat, hp, cap):
    t, half = hp.shape
    chunk = _pick(t, (512, 256, 128))
    grid_spec = pltpu.PrefetchScalarGridSpec(
        num_scalar_prefetch=1,
        grid=(t // chunk,),
        in_specs=[pl.BlockSpec((chunk, half), lambda i, ds: (i, 0))],
        out_specs=pl.BlockSpec(memory_space=pl.ANY),
        scratch_shapes=[pltpu.SemaphoreType.DMA(())],
    )
    return pl.pallas_call(
        functools.partial(_dispatch_kernel, chunk=chunk, t=t),
        out_shape=jax.ShapeDtypeStruct((cap, half), U32),
        grid_spec=grid_spec,
        compiler_params=pltpu.CompilerParams(dimension_semantics=("arbitrary",),
                                             has_side_effects=True, vmem_limit_bytes=32 * MIB),
    )(dest_flat, hp)


def _ffn_kernel(be_ref, nv_ref, nu_ref, xs_ref, wg_ref, wu_ref, wd_ref, o_ref):
    b = pl.program_id(0)

    @pl.when(b < nu_ref[0])
    def _():
        blk, half = xs_ref.shape
        row = lax.broadcasted_iota(I32, (blk, 1), 0)
        xp = jnp.where(row < nv_ref[b], xs_ref[...], jnp.uint32(0))
        lo = lax.bitcast_convert_type(xp << 16, F32).astype(BF16)
        hi = lax.bitcast_convert_type(xp & jnp.uint32(0xFFFF0000), F32).astype(BF16)
        a = (jnp.dot(lo, wg_ref[0, :half], preferred_element_type=F32)
             + jnp.dot(hi, wg_ref[0, half:], preferred_element_type=F32))
        u = (jnp.dot(lo, wu_ref[0, :half], preferred_element_type=F32)
             + jnp.dot(hi, wu_ref[0, half:], preferred_element_type=F32))
        mid = (a * jax.nn.sigmoid(a) * u).astype(BF16)
        o_ref[...] = jnp.dot(mid, wd_ref[0], preferred_element_type=F32)


def _experts(blk_e, blk_valid, n_used, xs, wg, wu, wd):
    cap, half = xs.shape
    d = 2 * half
    ff = wg.shape[2]
    nblk = cap // EXPERT_BLOCK

    def row_map(b, be, nv, nu):
        return (jnp.minimum(b, nu[0] - 1), 0)

    def w_map(b, be, nv, nu):
        return (be[jnp.minimum(b, nu[0] - 1)], 0, 0)

    grid_spec = pltpu.PrefetchScalarGridSpec(
        num_scalar_prefetch=3,
        grid=(nblk,),
        in_specs=[pl.BlockSpec((EXPERT_BLOCK, half), row_map),
                  pl.BlockSpec((1, d, ff), w_map),
                  pl.BlockSpec((1, d, ff), w_map),
                  pl.BlockSpec((1, ff, d), w_map)],
        out_specs=pl.BlockSpec((EXPERT_BLOCK, d), row_map),
    )
    return pl.pallas_call(
        _ffn_kernel,
        out_shape=jax.ShapeDtypeStruct((cap, d), F32),
        grid_spec=grid_spec,
        compiler_params=_params(("arbitrary",), 56),
    )(blk_e, blk_valid, n_used, xs, wg, wu, wd)


def _combine_kernel(dest_ref, x_ref, g_ref, w_ref, ys_ref, o_ref, ybuf, sem, *, chunk, t):
    i = pl.program_id(0)
    n = pl.num_programs(0)

    def issue(step, slot):
        base = step * chunk

        def body(r, carry):
            for k in range(TOP_K):
                pltpu.make_async_copy(ys_ref.at[pl.ds(dest_ref[k * t + base + r], 1)],
                                      ybuf.at[slot, k, pl.ds(r, 1)],
                                      sem.at[slot]).start(priority=k)
            return carry

        lax.fori_loop(0, chunk, body, 0, unroll=DMA_UNROLL)

    @pl.when(i == 0)
    def _():
        issue(0, 0)

    slot = i % 2

    @pl.when(i + 1 < n)
    def _():
        issue(i + 1, 1 - slot)

    for k in range(TOP_K):
        pltpu.make_async_copy(ys_ref.at[pl.ds(0, chunk)], ybuf.at[slot, k], sem.at[slot]).wait()
    w = w_ref[...]
    y = w[:, 0:1] * ybuf[slot, 0] + w[:, 1:2] * ybuf[slot, 1]
    o_ref[...] = x_ref[...] + g_ref[0] * y


def _combine(dest_flat, x1, gate, w_rows, ys, seq):
    t, d = x1.shape
    chunk = _pick(seq, (256, 128))
    rows_per_batch = seq // chunk
    grid_spec = pltpu.PrefetchScalarGridSpec(
        num_scalar_prefetch=1,
        grid=(t // chunk,),
        in_specs=[pl.BlockSpec((chunk, d), lambda i, ds: (i, 0)),
                  pl.BlockSpec((1, 1, d), lambda i, ds: (i // rows_per_batch, 0, 0)),
                  pl.BlockSpec((chunk, 8), lambda i, ds: (i, 0)),
                  pl.BlockSpec(memory_space=pl.ANY)],
        out_specs=pl.BlockSpec((chunk, d), lambda i, ds: (i, 0)),
        scratch_shapes=[pltpu.VMEM((2, TOP_K, chunk, d), F32),
                        pltpu.SemaphoreType.DMA((2,))],
    )
    return pl.pallas_call(
        functools.partial(_combine_kernel, chunk=chunk, t=t),
        out_shape=jax.ShapeDtypeStruct((t, d), F32),
        grid_spec=grid_spec,
        compiler_params=_params(("arbitrary",), 56),
    )(dest_flat, x1, gate, w_rows, ys)


def _layer(x, c, w_ada, b_ada, norm1_w, w_in, q_norm_w, k_norm_w, sinks, w_pool, pool_scale,
           w_attn_up, w_pool_up, w_out, norm2_w, w_router_group, b_router_group,
           w_router_expert, b_router_expert, w_gate, w_up, w_down):
    b, s, d = x.shape
    t = b * s
    aw = w_attn_up.shape[0]
    pw = w_pool_up.shape[0]
    kvw = N_KV_HEADS * HEAD_DIM
    x2 = x.reshape(t, d)

    c8 = jnp.zeros((8, d), F32).at[:b].set(c)
    ada = _ada(c8, w_ada, b_ada.reshape(1, 6 * d))[:b]
    shift1, scale1, gate1, shift2, scale2, gate2 = [a.reshape(b, 1, d) for a in jnp.split(ada, 6, axis=-1)]

    o_k = aw
    o_p = aw + 2 * kvw
    o_ga = o_p + pw
    o_gb = o_ga + d
    w_in_p = jnp.concatenate([w_in[:, o_ga:o_gb], w_in[:, o_gb:], w_in[:, :aw],
                              w_in[:, o_p:o_ga], w_in[:, o_k:o_p]], axis=1).astype(BF16)
    ga_col, gb_col, q_col, p_col, kv_col = 0, d, 2 * d, 2 * d + aw, 2 * d + aw + pw
    proj = _in_proj(x2, norm1_w.reshape(1, d), scale1, shift1, w_in_p, s)
    proj3 = proj.reshape(b, s, proj.shape[1])

    y_attn = _attention(proj3, sinks, q_norm_w, k_norm_w, q_col, kv_col).reshape(t, aw)
    y_pool = _pool(proj3, w_pool.astype(BF16), pool_scale.reshape(1, pw), p_col).reshape(t, pw)
    mixed = _mix(y_attn, y_pool, w_attn_up.astype(BF16), w_pool_up.astype(BF16), proj, ga_col, gb_col)
    x1 = _out_proj(mixed, w_out.astype(BF16), x2, gate1, s)

    wr = jnp.concatenate([w_router_group, w_router_expert,
                          jnp.zeros((d, LANES - N_ROUTER), F32)], axis=1)
    wr_hi = wr.astype(BF16)
    wr_lo = (wr - wr_hi.astype(F32)).astype(BF16)
    br = jnp.concatenate([b_router_group, b_router_expert]).reshape(N_ROUTER, 1)
    hp, meta_i, meta_f, cnt = _router(x1, norm2_w.reshape(1, d), scale2, shift2, wr_hi, wr_lo, br, s)

    cap = TOP_K * t + N_EXPERTS * EXPERT_BLOCK
    nblk = cap // EXPERT_BLOCK
    dest, tab = _plan(meta_i, cnt, nblk)
    dest_flat = dest.reshape(TOP_K * t)
    blk_e, blk_valid, n_used = tab[0, :nblk], tab[1, :nblk], tab[2, :1]

    xs = _dispatch(dest_flat, hp, cap)
    ys = _experts(blk_e, blk_valid, n_used, xs,
                  w_gate.astype(BF16), w_up.astype(BF16), w_down.astype(BF16))
    out = _combine(dest_flat, x1, gate2, meta_f.T, ys, s)
    return out.reshape(b, s, d)


def kernel(x, c, w_ada, b_ada, norm1_w, w_in, q_norm_w, k_norm_w, sinks, w_pool, pool_scale,
           w_attn_up, w_pool_up, w_out, norm2_w, w_router_group, b_router_group,
           w_router_expert, b_router_expert, w_gate, w_up, w_down):
    for l in range(w_ada.shape[0]):
        x = _layer(x, c, w_ada[l], b_ada[l], norm1_w[l], w_in[l], q_norm_w[l], k_norm_w[l],
                   sinks[l], w_pool[l], pool_scale[l], w_attn_up[l], w_pool_up[l], w_out[l],
                   norm2_w[l], w_router_group[l], b_router_group[l], w_router_expert[l],
                   b_router_expert[l], w_gate[l], w_up[l], w_down[l])
    return x
```

```python
import functools

import jax
import jax.numpy as jnp
import numpy as np
from jax import lax
from jax.experimental import pallas as pl
from jax.experimental.pallas import tpu as pltpu

F32 = jnp.float32
BF16 = jnp.bfloat16
U32 = jnp.uint32
I32 = jnp.int32

HEAD_DIM = 64
N_Q_HEADS = 32
N_KV_HEADS = 4
Q_PER_KV = N_Q_HEADS // N_KV_HEADS
ATTN_BLOCK = 128
WINDOW = 128
POOL_WINDOWS = (2, 4, 8, 16)
POOL_HALO = 16
N_GROUPS = 8
EXPERTS_PER_GROUP = 8
N_EXPERTS = N_GROUPS * EXPERTS_PER_GROUP
N_ROUTER = N_GROUPS + N_EXPERTS
TOP_K = 2
EPS = 1e-6
NEG_INF = -1e30

LANES = 128
EXPERT_BLOCK = 256
MIB = 1024 * 1024


def _params(semantics, vmem_mib):
    return pltpu.CompilerParams(dimension_semantics=semantics, vmem_limit_bytes=vmem_mib * MIB)


def _pick(n, candidates):
    for c in candidates:
        if n % c == 0:
            return c
    return n


def _ada_kernel(c_ref, w_ref, b_ref, o_ref):
    c = c_ref[...]
    s = (c * jax.nn.sigmoid(c)).astype(BF16)
    o_ref[...] = jnp.dot(s, w_ref[...].astype(BF16), preferred_element_type=F32) + b_ref[...]


def _ada(c8, w, b):
    d, n = w.shape
    tn = _pick(n, (1024, 512, 256, 128))
    return pl.pallas_call(
        _ada_kernel,
        out_shape=jax.ShapeDtypeStruct((8, n), F32),
        grid=(n // tn,),
        in_specs=[pl.BlockSpec((8, d), lambda j: (0, 0)),
                  pl.BlockSpec((d, tn), lambda j: (0, j)),
                  pl.BlockSpec((1, tn), lambda j: (0, j))],
        out_specs=pl.BlockSpec((8, tn), lambda j: (0, j)),
        compiler_params=_params(("arbitrary",), 56),
    )(c8, w, b)


def _modulated_norm(x, nw, scale, shift):
    ms = jnp.mean(x * x, axis=-1, keepdims=True)
    y = x * lax.rsqrt(ms + EPS) * nw
    return y * (1.0 + scale) + shift


NORM_ROWS = 64


def _norm_kernel(x_ref, nw_ref, sc_ref, sh_ref, o_ref):
    def body(r, carry):
        rows = pl.ds(pl.multiple_of(r * NORM_ROWS, NORM_ROWS), NORM_ROWS)
        o_ref[rows, :] = _modulated_norm(x_ref[rows, :], nw_ref[...], sc_ref[0],
                                         sh_ref[0]).astype(o_ref.dtype)
        return carry

    lax.fori_loop(0, x_ref.shape[0] // NORM_ROWS, body, 0)


def _norm(x2, nw, scale, shift, seq):
    t, d = x2.shape
    tm = _pick(seq, (512, 256, 128))
    rows_per_batch = seq // tm
    return pl.pallas_call(
        _norm_kernel,
        out_shape=jax.ShapeDtypeStruct((t, d), BF16),
        grid=(t // tm,),
        in_specs=[pl.BlockSpec((tm, d), lambda i: (i, 0)),
                  pl.BlockSpec((1, d), lambda i: (0, 0)),
                  pl.BlockSpec((1, 1, d), lambda i: (i // rows_per_batch, 0, 0)),
                  pl.BlockSpec((1, 1, d), lambda i: (i // rows_per_batch, 0, 0))],
        out_specs=pl.BlockSpec((tm, d), lambda i: (i, 0)),
        compiler_params=_params(("parallel",), 40),
    )(x2, nw, scale, shift)


def _matmul_kernel(a_ref, w_ref, o_ref):
    o_ref[...] = jnp.dot(a_ref[...], w_ref[...], preferred_element_type=F32).astype(o_ref.dtype)


def _in_proj(h, w):
    t, d = h.shape
    n = w.shape[1]
    tm = _pick(t, (1024, 512, 256, 128))
    tn = _pick(n, (1280, 512, 256, 128))
    return pl.pallas_call(
        _matmul_kernel,
        out_shape=jax.ShapeDtypeStruct((t, n), BF16),
        grid=(t // tm, n // tn),
        in_specs=[pl.BlockSpec((tm, d), lambda i, j: (i, 0)),
                  pl.BlockSpec((d, tn), lambda i, j: (0, j))],
        out_specs=pl.BlockSpec((tm, tn), lambda i, j: (i, j)),
        compiler_params=_params(("parallel", "arbitrary"), 56),
    )(h, w)


def _head_sumsq(v, ones_bd):
    sq = v * v
    hi = sq.astype(BF16)
    lo = (sq - hi.astype(F32)).astype(BF16)
    return (jnp.dot(hi, ones_bd, preferred_element_type=F32)
            + jnp.dot(lo, ones_bd, preferred_element_type=F32))


def _head_rms_norm(v, w2, ones_bd):
    ss = _head_sumsq(v, ones_bd)
    return v * lax.rsqrt(ss * (1.0 / HEAD_DIM) + EPS) * w2


def _attn_kernel(sink_ref, q_ref, kvc_ref, kvp_ref, bias_ref, qw_ref, kw_ref, wg_ref, wu_ref, wd_ref,
                 o_ref, wg_o, wu_o, wd_o):
    wg_o[...] = wg_ref[...].astype(BF16)
    wu_o[...] = wu_ref[...].astype(BF16)
    wd_o[...] = wd_ref[...].astype(BF16)

    blk = ATTN_BLOCK
    kvw = N_KV_HEADS * HEAD_DIM
    lane = lax.broadcasted_iota(I32, (1, LANES), 1)
    low_half = lane < HEAD_DIM
    r = lax.broadcasted_iota(I32, (LANES, LANES), 0) // HEAD_DIM
    c = lax.broadcasted_iota(I32, (LANES, LANES), 1) // HEAD_DIM
    ones_bd = jnp.where(r == c, 1.0, 0.0).astype(BF16)

    kv = jnp.concatenate([kvp_ref[0], kvc_ref[0]], axis=0).astype(F32)
    kw2 = kw_ref[...]
    sum_a = jnp.broadcast_to(jnp.where(low_half, 1.0, 0.0), (2 * blk, LANES))
    sum_b = jnp.broadcast_to(jnp.where(low_half, 0.0, 1.0), (2 * blk, LANES))

    for g in range(N_KV_HEADS):
        chunk = g // 2
        kc = _head_rms_norm(kv[:, chunk * LANES:(chunk + 1) * LANES], kw2, ones_bd)
        vc = kv[:, kvw + chunk * LANES: kvw + (chunk + 1) * LANES]
        kr = pltpu.roll(kc, HEAD_DIM, axis=1)
        vr = pltpu.roll(vc, HEAD_DIM, axis=1)
        if g % 2 == 0:
            k_lo, k_hi, v_lo, v_hi = kc, kr, vc, vr
        else:
            k_lo, k_hi, v_lo, v_hi = kr, kc, vr, vc
        k_bd = jnp.concatenate([jnp.where(low_half, k_lo, 0.0),
                                jnp.where(low_half, 0.0, k_hi)], axis=0).astype(BF16)
        v_bd = jnp.concatenate(
            [jnp.concatenate([jnp.where(low_half, v_lo, 0.0), sum_a], axis=1),
             jnp.concatenate([jnp.where(low_half, 0.0, v_hi), sum_b], axis=1)],
            axis=0).astype(BF16)

        pairs = [g * (Q_PER_KV // 2) + jj for jj in range(Q_PER_KV // 2)]
        scores = []
        for j in pairs:
            qp = q_ref[0, :, j * LANES:(j + 1) * LANES].astype(F32)
            qn = _head_rms_norm(qp, qw_ref[...], ones_bd).astype(BF16)
            s = lax.dot_general(qn, k_bd, (((1,), (1,)), ((), ())),
                                preferred_element_type=F32)
            scores.append(s + bias_ref[0, j])
        probs = []
        for j, s in zip(pairs, scores):
            sink_a = sink_ref[2 * j]
            sink_b = sink_ref[2 * j + 1]
            m_a = jnp.maximum(jnp.max(s[:, :2 * blk], axis=-1, keepdims=True), sink_a)
            m_b = jnp.maximum(jnp.max(s[:, 2 * blk:], axis=-1, keepdims=True), sink_b)
            p = jnp.concatenate([jnp.exp(s[:, :2 * blk] - m_a),
                                 jnp.exp(s[:, 2 * blk:] - m_b)], axis=1).astype(BF16)
            esink = jnp.where(low_half, jnp.exp(sink_a - m_a), jnp.exp(sink_b - m_b))
            probs.append((p, esink))
        for j, (p, esink) in zip(pairs, probs):
            ol = jnp.dot(p, v_bd, preferred_element_type=F32)
            o = ol[:, :LANES] / (ol[:, LANES:] + esink)
            o_ref[0, :, j * LANES:(j + 1) * LANES] = o.astype(o_ref.dtype)


def _attn_bias():
    blk = ATTN_BLOCK
    i = np.arange(blk)[:, None]
    j = np.arange(2 * blk)[None, :]
    dist = i + blk - j
    band = (dist >= 0) & (dist < WINDOW)
    slopes = (2.0 ** (-8.0 * np.arange(1, N_Q_HEADS + 1, dtype=np.float32) / N_Q_HEADS)).astype(np.float32)
    alibi = -slopes[:, None, None] * dist.astype(np.float32)[None]
    later = np.where(band[None], alibi, np.float32(NEG_INF))
    first = np.where((band & (j >= blk))[None], alibi, np.float32(NEG_INF))
    tab = np.stack([first, later]).astype(np.float32)
    tab = tab.reshape(2, N_Q_HEADS // 2, 2, blk, 2 * blk)
    return np.transpose(tab, (0, 1, 3, 2, 4)).reshape(2, N_Q_HEADS // 2, blk, 4 * blk)


def _attention(proj3, sinks, q_norm_w, k_norm_w, q_col, kv_col, w_gate, w_up, w_down):
    b, s, _ = proj3.shape
    aw = N_Q_HEADS * HEAD_DIM
    kvw2 = 2 * N_KV_HEADS * HEAD_DIM
    nb = s // ATTN_BLOCK
    bias = _attn_bias()
    qw2 = (jnp.tile(q_norm_w, 2) * (HEAD_DIM ** -0.5)).reshape(1, LANES)
    kw2 = jnp.tile(k_norm_w, 2).reshape(1, LANES)
    ne, d, ff = w_gate.shape
    steps = b * nb
    wg2 = w_gate.reshape(ne * d, ff)
    wu2 = w_up.reshape(ne * d, ff)
    wd2 = w_down.reshape(ne * ff, d)
    rows_in = ne * d // steps
    rows_dn = ne * ff // steps

    def w_map(bi, n, sk):
        return (bi * nb + n, 0)

    w_specs = [pl.BlockSpec((rows_in, ff), w_map), pl.BlockSpec((rows_in, ff), w_map),
               pl.BlockSpec((rows_dn, d), w_map)]
    grid_spec = pltpu.PrefetchScalarGridSpec(
        num_scalar_prefetch=1,
        grid=(b, nb),
        in_specs=[pl.BlockSpec((1, ATTN_BLOCK, aw), lambda bi, n, sk: (bi, n, q_col // aw)),
                  pl.BlockSpec((1, ATTN_BLOCK, kvw2), lambda bi, n, sk: (bi, n, kv_col // kvw2)),
                  pl.BlockSpec((1, ATTN_BLOCK, kvw2),
                               lambda bi, n, sk: (bi, jnp.maximum(n - 1, 0), kv_col // kvw2)),
                  pl.BlockSpec((1, N_Q_HEADS // 2, ATTN_BLOCK, 4 * ATTN_BLOCK),
                               lambda bi, n, sk: (jnp.minimum(n, 1), 0, 0, 0)),
                  pl.BlockSpec((1, LANES), lambda bi, n, sk: (0, 0)),
                  pl.BlockSpec((1, LANES), lambda bi, n, sk: (0, 0))] + w_specs,
        out_specs=tuple([pl.BlockSpec((1, ATTN_BLOCK, aw), lambda bi, n, sk: (bi, n, 0))] + w_specs),
    )
    y, wg_b, wu_b, wd_b = pl.pallas_call(
        _attn_kernel,
        out_shape=(jax.ShapeDtypeStruct((b, s, aw), BF16),
                   jax.ShapeDtypeStruct(wg2.shape, BF16),
                   jax.ShapeDtypeStruct(wu2.shape, BF16),
                   jax.ShapeDtypeStruct(wd2.shape, BF16)),
        grid_spec=grid_spec,
        compiler_params=_params(("arbitrary", "arbitrary"), 56),
    )(sinks, proj3, proj3, proj3, bias, qw2, kw2, wg2, wu2, wd2)
    return y, wg_b.reshape(ne, d, ff), wu_b.reshape(ne, d, ff), wd_b.reshape(ne, ff, d)


def _pool_kernel(pc_ref, ph_ref, w_ref, ps_ref, o_ref):
    n = pl.program_id(1)
    ts = pc_ref.shape[1]
    cg = w_ref.shape[1]
    cur = pc_ref[0].astype(F32)
    halo = jnp.where(n > 0, ph_ref[0].astype(F32), 0.0)
    ext = jnp.concatenate([halo, cur], axis=0)
    t1 = (lax.broadcasted_iota(I32, (ts, 1), 0) + n * ts + 1).astype(F32)
    for g, win in enumerate(POOL_WINDOWS):
        acc = ext[:, g * cg:(g + 1) * cg]
        k = 1
        while k < win:
            acc = acc + pltpu.roll(acc, k, axis=0)
            k *= 2
        mean = acc[POOL_HALO:] / jnp.minimum(t1, float(win))
        d = (mean - cur[:, g * cg:(g + 1) * cg]).astype(BF16)
        y = jnp.dot(d, w_ref[g], preferred_element_type=F32) * ps_ref[:, g * cg:(g + 1) * cg]
        o_ref[0, :, g * cg:(g + 1) * cg] = y.astype(o_ref.dtype)


def _pool(proj3, w_pool, pool_scale, p_col):
    b, s, _ = proj3.shape
    ng, cg, _ = w_pool.shape
    pw = ng * cg
    ts = _pick(s, (512, 256, 128))
    return pl.pallas_call(
        _pool_kernel,
        out_shape=jax.ShapeDtypeStruct((b, s, pw), BF16),
        grid=(b, s // ts),
        in_specs=[pl.BlockSpec((pl.Element(1), pl.Element(ts), pl.Element(pw)),
                               lambda bi, n: (bi, pl.multiple_of(n * ts, ts), p_col)),
                  pl.BlockSpec((pl.Element(1), pl.Element(POOL_HALO), pl.Element(pw)),
                               lambda bi, n: (bi, pl.multiple_of(jnp.maximum(n * ts - POOL_HALO, 0),
                                                                 POOL_HALO), p_col)),
                  pl.BlockSpec((ng, cg, cg), lambda bi, n: (0, 0, 0)),
                  pl.BlockSpec((1, pw), lambda bi, n: (0, 0))],
        out_specs=pl.BlockSpec((1, ts, pw), lambda bi, n: (bi, n, 0)),
        compiler_params=_params(("parallel", "arbitrary"), 40),
    )(proj3, proj3, w_pool, pool_scale)


def _mix_kernel(a_ref, b_ref, wa_ref, wb_ref, ga_ref, gb_ref, o_ref):
    ya = jnp.dot(a_ref[...], wa_ref[...], preferred_element_type=F32)
    yb = jnp.dot(b_ref[...], wb_ref[...], preferred_element_type=F32)
    ga = jax.nn.sigmoid(ga_ref[...].astype(F32))
    gb = jax.nn.sigmoid(gb_ref[...].astype(F32))
    o_ref[...] = (ga * ya + gb * yb).astype(o_ref.dtype)


def _mix(y_attn, y_pool, wa, wb, proj, ga_col, gb_col):
    t, aw = y_attn.shape
    pw = y_pool.shape[1]
    d = wa.shape[1]
    tm = _pick(t, (512, 256, 128))
    tn = _pick(d, (1024, 512, 256, 128))
    return pl.pallas_call(
        _mix_kernel,
        out_shape=jax.ShapeDtypeStruct((t, d), BF16),
        grid=(t // tm, d // tn),
        in_specs=[pl.BlockSpec((tm, aw), lambda i, j: (i, 0)),
                  pl.BlockSpec((tm, pw), lambda i, j: (i, 0)),
                  pl.BlockSpec((aw, tn), lambda i, j: (0, j)),
                  pl.BlockSpec((pw, tn), lambda i, j: (0, j)),
                  pl.BlockSpec((pl.Element(tm), pl.Element(tn)),
                               lambda i, j: (pl.multiple_of(i * tm, tm),
                                             pl.multiple_of(ga_col + j * tn, LANES))),
                  pl.BlockSpec((pl.Element(tm), pl.Element(tn)),
                               lambda i, j: (pl.multiple_of(i * tm, tm),
                                             pl.multiple_of(gb_col + j * tn, LANES)))],
        out_specs=pl.BlockSpec((tm, tn), lambda i, j: (i, j)),
        compiler_params=_params(("parallel", "arbitrary"), 56),
    )(y_attn, y_pool, wa, wb, proj, proj)


def _out_kernel(m_ref, w_ref, x_ref, g_ref, o_ref):
    y = jnp.dot(m_ref[...], w_ref[...], preferred_element_type=F32)
    o_ref[...] = x_ref[...] + g_ref[0] * y


def _out_proj(mixed, w, x2, gate, seq):
    t, d = x2.shape
    tm = _pick(seq, (512, 256, 128))
    tn = _pick(d, (1024, 512, 256, 128))
    rows_per_batch = seq // tm
    return pl.pallas_call(
        _out_kernel,
        out_shape=jax.ShapeDtypeStruct((t, d), F32),
        grid=(t // tm, d // tn),
        in_specs=[pl.BlockSpec((tm, d), lambda i, j: (i, 0)),
                  pl.BlockSpec((d, tn), lambda i, j: (0, j)),
                  pl.BlockSpec((tm, tn), lambda i, j: (i, j)),
                  pl.BlockSpec((1, 1, tn), lambda i, j: (i // rows_per_batch, 0, j))],
        out_specs=pl.BlockSpec((tm, tn), lambda i, j: (i, j)),
        compiler_params=_params(("parallel", "arbitrary"), 56),
    )(mixed, w, x2, gate)


def _first_argmax(v, iota, n):
    m = jnp.max(v, axis=0, keepdims=True)
    idx = jnp.min(jnp.where(v == m, iota, n), axis=0, keepdims=True)
    return m, idx


def _router_kernel(x_ref, nw_ref, sc_ref, sh_ref, whi_ref, wlo_ref, br_ref,
                   hp_ref, mi_ref, mf_ref, cnt_ref, carry_ref, hhi_ref, hlo_ref):
    step = pl.program_id(0)
    tm, d = x_ref.shape
    half = d // 2

    @pl.when(step == 0)
    def _():
        carry_ref[...] = jnp.zeros_like(carry_ref)

    def body(r, carry):
        rows = pl.ds(pl.multiple_of(r * NORM_ROWS, NORM_ROWS), NORM_ROWS)
        h = _modulated_norm(x_ref[rows, :], nw_ref[...], sc_ref[0], sh_ref[0])
        hi = h.astype(BF16)
        hi32 = hi.astype(F32)
        hhi_ref[rows, :] = hi
        hlo_ref[rows, :] = (h - hi32).astype(BF16)
        bits = lax.bitcast_convert_type(hi32, U32)
        hp_ref[rows, :] = (bits[:, :half] >> 16) | (bits[:, half:] & jnp.uint32(0xFFFF0000))
        return carry

    lax.fori_loop(0, tm // NORM_ROWS, body, 0)

    lt = (jnp.dot(hhi_ref[...], whi_ref[...], preferred_element_type=F32)
          + jnp.dot(hlo_ref[...], whi_ref[...], preferred_element_type=F32)
          + jnp.dot(hhi_ref[...], wlo_ref[...], preferred_element_type=F32))
    logits = lt.T[:N_ROUTER] + br_ref[...]
    iota8 = lax.broadcasted_iota(I32, (N_GROUPS, tm), 0)
    gl = logits[:N_GROUPS]
    g_max, g_sel = _first_argmax(gl, iota8, N_GROUPS)
    p_group = 1.0 / jnp.sum(jnp.exp(gl - g_max), axis=0, keepdims=True)
    e_in = jnp.zeros((EXPERTS_PER_GROUP, tm), F32)
    for g in range(N_GROUPS):
        lo = N_GROUPS + g * EXPERTS_PER_GROUP
        e_in = e_in + jnp.where(g_sel == g, logits[lo:lo + EXPERTS_PER_GROUP], 0.0)
    v0, i0 = _first_argmax(e_in, iota8, EXPERTS_PER_GROUP)
    rest = jnp.where(iota8 == i0, -jnp.inf, e_in)
    v1, i1 = _first_argmax(rest, iota8, EXPERTS_PER_GROUP)
    t = jnp.exp(v1 - v0)
    w0 = p_group / (1.0 + t)
    w1 = p_group * t / (1.0 + t)
    e0 = g_sel * EXPERTS_PER_GROUP + i0
    e1 = g_sel * EXPERTS_PER_GROUP + i1

    iota_e = lax.broadcasted_iota(I32, (N_EXPERTS, tm), 0)
    hit0 = iota_e == e0
    hit1 = iota_e == e1
    onehot = jnp.where(hit0 | hit1, 1.0, 0.0).astype(BF16)
    rr = lax.broadcasted_iota(I32, (tm, tm), 0)
    cc = lax.broadcasted_iota(I32, (tm, tm), 1)
    before = jnp.where(rr < cc, 1.0, 0.0).astype(BF16)
    prior = jnp.dot(onehot, before, preferred_element_type=F32) + carry_ref[:, :1]
    r0 = jnp.sum(jnp.where(hit0, prior, 0.0), axis=0, keepdims=True)
    r1 = jnp.sum(jnp.where(hit1, prior, 0.0), axis=0, keepdims=True)
    carry_ref[...] += jnp.dot(onehot, jnp.ones((tm, LANES), BF16), preferred_element_type=F32)

    zi = jnp.zeros((1, tm), I32)
    zf = jnp.zeros((1, tm), F32)
    mi_ref[...] = jnp.concatenate([e0, e1, r0.astype(I32), r1.astype(I32), zi, zi, zi, zi], axis=0)
    mf_ref[...] = jnp.concatenate([w0, w1, zf, zf, zf, zf, zf, zf], axis=0)
    cnt_ref[...] = carry_ref[...]


def _router(x1, nw, scale, shift, w_hi, w_lo, br, seq):
    t, d = x1.shape
    tm = _pick(seq, (512, 256, 128))
    rows_per_batch = seq // tm
    return pl.pallas_call(
        _router_kernel,
        out_shape=(jax.ShapeDtypeStruct((t, d // 2), U32),
                   jax.ShapeDtypeStruct((8, t), I32),
                   jax.ShapeDtypeStruct((8, t), F32),
                   jax.ShapeDtypeStruct((N_EXPERTS, LANES), F32)),
        grid=(t // tm,),
        in_specs=[pl.BlockSpec((tm, d), lambda i: (i, 0)),
                  pl.BlockSpec((1, d), lambda i: (0, 0)),
                  pl.BlockSpec((1, 1, d), lambda i: (i // rows_per_batch, 0, 0)),
                  pl.BlockSpec((1, 1, d), lambda i: (i // rows_per_batch, 0, 0)),
                  pl.BlockSpec((d, LANES), lambda i: (0, 0)),
                  pl.BlockSpec((d, LANES), lambda i: (0, 0)),
                  pl.BlockSpec((N_ROUTER, 1), lambda i: (0, 0))],
        out_specs=(pl.BlockSpec((tm, d // 2), lambda i: (i, 0)),
                   pl.BlockSpec((8, tm), lambda i: (0, i)),
                   pl.BlockSpec((8, tm), lambda i: (0, i)),
                   pl.BlockSpec((N_EXPERTS, LANES), lambda i: (0, 0))),
        scratch_shapes=[pltpu.VMEM((N_EXPERTS, LANES), F32), pltpu.VMEM((tm, d), BF16),
                        pltpu.VMEM((tm, d), BF16)],
        compiler_params=_params(("arbitrary",), 56),
    )(x1, nw, scale, shift, w_hi, w_lo, br)


PLAN_LANES = 2048
DMA_UNROLL = 4


def _plan_kernel(mi_ref, cnt_ref, dest_ref, tab_ref):
    cnt = cnt_ref[...]
    nb = jnp.floor((cnt + (EXPERT_BLOCK - 1)) * (1.0 / EXPERT_BLOCK))
    r = lax.broadcasted_iota(I32, (N_EXPERTS, N_EXPERTS), 0)
    c = lax.broadcasted_iota(I32, (N_EXPERTS, N_EXPERTS), 1)
    incl = jnp.where(c <= r, 1.0, 0.0).astype(BF16)
    bend = jnp.dot(incl, nb.astype(BF16), preferred_element_type=F32)
    bstart = bend - nb
    first_row = bstart[:, :1] * EXPERT_BLOCK

    tl = mi_ref.shape[1]
    iota_e = lax.broadcasted_iota(I32, (N_EXPERTS, tl), 0)
    for k in range(TOP_K):
        hit = iota_e == mi_ref[k:k + 1, :]
        base = jnp.sum(jnp.where(hit, first_row, 0.0), axis=0, keepdims=True)
        dest_ref[k:k + 1, :] = base.astype(I32) + mi_ref[TOP_K + k:TOP_K + k + 1, :]

    nl = tab_ref.shape[1]
    bidx = lax.broadcasted_iota(I32, (1, nl), 1).astype(F32)
    blk_e = jnp.sum(jnp.where(bend[:, :1] <= bidx, 1.0, 0.0), axis=0, keepdims=True)
    blk_e = jnp.minimum(blk_e, N_EXPERTS - 1.0)
    hit = lax.broadcasted_iota(I32, (N_EXPERTS, nl), 0) == blk_e.astype(I32)
    cnt_b = jnp.sum(jnp.where(hit, cnt[:, :1], 0.0), axis=0, keepdims=True)
    start_b = jnp.sum(jnp.where(hit, bstart[:, :1], 0.0), axis=0, keepdims=True)
    valid = jnp.clip(cnt_b - (bidx - start_b) * EXPERT_BLOCK, 0.0, float(EXPERT_BLOCK))
    n_used = jnp.broadcast_to(jnp.max(bend[:, :1], axis=0, keepdims=True), (1, nl))
    zero = jnp.zeros((1, nl), F32)
    tab_ref[...] = jnp.concatenate([blk_e, valid, n_used, zero, zero, zero, zero, zero],
                                   axis=0).astype(I32)


def _plan(meta_i, cnt, nblk):
    t = meta_i.shape[1]
    tl = _pick(t, (PLAN_LANES, 1024, 512, 256, 128))
    nl = pl.cdiv(nblk, LANES) * LANES
    return pl.pallas_call(
        _plan_kernel,
        out_shape=(jax.ShapeDtypeStruct((TOP_K, t), I32), jax.ShapeDtypeStruct((8, nl), I32)),
        grid=(t // tl,),
        in_specs=[pl.BlockSpec((8, tl), lambda i: (0, i)),
                  pl.BlockSpec((N_EXPERTS, LANES), lambda i: (0, 0))],
        out_specs=(pl.BlockSpec((TOP_K, tl), lambda i: (0, i)),
                   pl.BlockSpec((8, nl), lambda i: (0, 0))),
        compiler_params=_params(("arbitrary",), 32),
    )(meta_i, cnt)


def _dispatch_kernel(dest_ref, hp_ref, xs_ref, sem, *, chunk, t):
    base = pl.program_id(0) * chunk

    def issue(i, carry):
        for k in range(TOP_K):
            pltpu.make_async_copy(hp_ref.at[pl.ds(i, 1)],
                                  xs_ref.at[pl.ds(dest_ref[k * t + base + i], 1)],
                                  sem).start(priority=k)
        return carry

    lax.fori_loop(0, chunk, issue, 0, unroll=DMA_UNROLL)
    for k in range(TOP_K):
        pltpu.make_async_copy(hp_ref, xs_ref.at[pl.ds(0, chunk)], sem).wait()


def _dispatch(dest_flat, hp, cap):
    t, half = hp.shape
    chunk = _pick(t, (512, 256, 128))
    grid_spec = pltpu.PrefetchScalarGridSpec(
        num_scalar_prefetch=1,
        grid=(t // chunk,),
        in_specs=[pl.BlockSpec((chunk, half), lambda i, ds: (i, 0))],
        out_specs=pl.BlockSpec(memory_space=pl.ANY),
        scratch_shapes=[pltpu.SemaphoreType.DMA(())],
    )
    return pl.pallas_call(
        functools.partial(_dispatch_kernel, chunk=chunk, t=t),
        out_shape=jax.ShapeDtypeStruct((cap, half), U32),
        grid_spec=grid_spec,
        compiler_params=pltpu.CompilerParams(dimension_semantics=("arbitrary",),
                                             has_side_effects=True, vmem_limit_bytes=32 * MIB),
    )(dest_flat, hp)


def _ffn_kernel(be_ref, nv_ref, nu_ref, xs_ref, wg_ref, wu_ref, wd_ref, o_ref):
    b = pl.program_id(0)

    @pl.when(b < nu_ref[0])
    def _():
        blk, half = xs_ref.shape
        row = lax.broadcasted_iota(I32, (blk, 1), 0)
        xp = jnp.where(row < nv_ref[b], xs_ref[...], jnp.uint32(0))
        lo = lax.bitcast_convert_type(xp << 16, F32).astype(BF16)
        hi = lax.bitcast_convert_type(xp & jnp.uint32(0xFFFF0000), F32).astype(BF16)
        a = (jnp.dot(lo, wg_ref[0, :half], preferred_element_type=F32)
             + jnp.dot(hi, wg_ref[0, half:], preferred_element_type=F32))
        u = (jnp.dot(lo, wu_ref[0, :half], preferred_element_type=F32)
             + jnp.dot(hi, wu_ref[0, half:], preferred_element_type=F32))
        mid = (a * jax.nn.sigmoid(a) * u).astype(BF16)
        o_ref[...] = jnp.dot(mid, wd_ref[0], preferred_element_type=F32)


def _experts(blk_e, blk_valid, n_used, xs, wg, wu, wd):
    cap, half = xs.shape
    d = 2 * half
    ff = wg.shape[2]
    nblk = cap // EXPERT_BLOCK

    def row_map(b, be, nv, nu):
        return (jnp.minimum(b, nu[0] - 1), 0)

    def w_map(b, be, nv, nu):
        return (be[jnp.minimum(b, nu[0] - 1)], 0, 0)

    grid_spec = pltpu.PrefetchScalarGridSpec(
        num_scalar_prefetch=3,
        grid=(nblk,),
        in_specs=[pl.BlockSpec((EXPERT_BLOCK, half), row_map),
                  pl.BlockSpec((1, d, ff), w_map),
                  pl.BlockSpec((1, d, ff), w_map),
                  pl.BlockSpec((1, ff, d), w_map)],
        out_specs=pl.BlockSpec((EXPERT_BLOCK, d), row_map),
    )
    return pl.pallas_call(
        _ffn_kernel,
        out_shape=jax.ShapeDtypeStruct((cap, d), F32),
        grid_spec=grid_spec,
        compiler_params=_params(("arbitrary",), 56),
    )(blk_e, blk_valid, n_used, xs, wg, wu, wd)


def _combine_kernel(dest_ref, x_ref, g_ref, w_ref, ys_ref, o_ref, ybuf, sem, *, chunk, t):
    i = pl.program_id(0)
    n = pl.num_programs(0)

    def issue(step, slot):
        base = step * chunk

        def body(r, carry):
            for k in range(TOP_K):
                pltpu.make_async_copy(ys_ref.at[pl.ds(dest_ref[k * t + base + r], 1)],
                                      ybuf.at[slot, k, pl.ds(r, 1)],
                                      sem.at[slot]).start(priority=k)
            return carry

        lax.fori_loop(0, chunk, body, 0, unroll=DMA_UNROLL)

    @pl.when(i == 0)
    def _():
        issue(0, 0)

    slot = i % 2

    @pl.when(i + 1 < n)
    def _():
        issue(i + 1, 1 - slot)

    for k in range(TOP_K):
        pltpu.make_async_copy(ys_ref.at[pl.ds(0, chunk)], ybuf.at[slot, k], sem.at[slot]).wait()
    w = w_ref[...]
    y = w[:, 0:1] * ybuf[slot, 0] + w[:, 1:2] * ybuf[slot, 1]
    o_ref[...] = x_ref[...] + g_ref[0] * y


def _combine(dest_flat, x1, gate, w_rows, ys, seq):
    t, d = x1.shape
    chunk = _pick(seq, (256, 128))
    rows_per_batch = seq // chunk
    grid_spec = pltpu.PrefetchScalarGridSpec(
        num_scalar_prefetch=1,
        grid=(t // chunk,),
        in_specs=[pl.BlockSpec((chunk, d), lambda i, ds: (i, 0)),
                  pl.BlockSpec((1, 1, d), lambda i, ds: (i // rows_per_batch, 0, 0)),
                  pl.BlockSpec((chunk, 8), lambda i, ds: (i, 0)),
                  pl.BlockSpec(memory_space=pl.ANY)],
        out_specs=pl.BlockSpec((chunk, d), lambda i, ds: (i, 0)),
        scratch_shapes=[pltpu.VMEM((2, TOP_K, chunk, d), F32),
                        pltpu.SemaphoreType.DMA((2,))],
    )
    return pl.pallas_call(
        functools.partial(_combine_kernel, chunk=chunk, t=t),
        out_shape=jax.ShapeDtypeStruct((t, d), F32),
        grid_spec=grid_spec,
        compiler_params=_params(("arbitrary",), 56),
    )(dest_flat, x1, gate, w_rows, ys)


def _layer(x, c, w_ada, b_ada, norm1_w, w_in, q_norm_w, k_norm_w, sinks, w_pool, pool_scale,
           w_attn_up, w_pool_up, w_out, norm2_w, w_router_group, b_router_group,
           w_router_expert, b_router_expert, w_gate, w_up, w_down):
    b, s, d = x.shape
    t = b * s
    aw = w_attn_up.shape[0]
    pw = w_pool_up.shape[0]
    kvw = N_KV_HEADS * HEAD_DIM
    x2 = x.reshape(t, d)

    c8 = jnp.zeros((8, d), F32).at[:b].set(c)
    ada = _ada(c8, w_ada, b_ada.reshape(1, 6 * d))[:b]
    shift1, scale1, gate1, shift2, scale2, gate2 = [a.reshape(b, 1, d) for a in jnp.split(ada, 6, axis=-1)]

    q_col, kv_col = 0, aw
    p_col = aw + 2 * kvw
    ga_col = p_col + pw
    gb_col = ga_col + d
    h = _norm(x2, norm1_w.reshape(1, d), scale1, shift1, s)
    proj = _in_proj(h, w_in.astype(BF16))
    proj3 = proj.reshape(b, s, proj.shape[1])

    y_attn, wg_b, wu_b, wd_b = _attention(proj3, sinks, q_norm_w, k_norm_w, q_col, kv_col,
                                          w_gate, w_up, w_down)
    y_attn = y_attn.reshape(t, aw)
    y_pool = _pool(proj3, w_pool.astype(BF16), pool_scale.reshape(1, pw), p_col).reshape(t, pw)
    mixed = _mix(y_attn, y_pool, w_attn_up.astype(BF16), w_pool_up.astype(BF16), proj, ga_col, gb_col)
    x1 = _out_proj(mixed, w_out.astype(BF16), x2, gate1, s)

    wr = jnp.concatenate([w_router_group, w_router_expert,
                          jnp.zeros((d, LANES - N_ROUTER), F32)], axis=1)
    wr_hi = wr.astype(BF16)
    wr_lo = (wr - wr_hi.astype(F32)).astype(BF16)
    br = jnp.concatenate([b_router_group, b_router_expert]).reshape(N_ROUTER, 1)
    hp, meta_i, meta_f, cnt = _router(x1, norm2_w.reshape(1, d), scale2, shift2, wr_hi, wr_lo, br, s)

    cap = TOP_K * t + N_EXPERTS * EXPERT_BLOCK
    nblk = cap // EXPERT_BLOCK
    dest, tab = _plan(meta_i, cnt, nblk)
    dest_flat = dest.reshape(TOP_K * t)
    blk_e, blk_valid, n_used = tab[0, :nblk], tab[1, :nblk], tab[2, :1]

    xs = _dispatch(dest_flat, hp, cap)
    ys = _experts(blk_e, blk_valid, n_used, xs, wg_b, wu_b, wd_b)
    out = _combine(dest_flat, x1, gate2, meta_f.T, ys, s)
    return out.reshape(b, s, d)


def kernel(x, c, w_ada, b_ada, norm1_w, w_in, q_norm_w, k_norm_w, sinks, w_pool, pool_scale,
           w_attn_up, w_pool_up, w_out, norm2_w, w_router_group, b_router_group,
           w_router_expert, b_router_expert, w_gate, w_up, w_down):
    for l in range(w_ada.shape[0]):
        x = _layer(x, c, w_ada[l], b_ada[l], norm1_w[l], w_in[l], q_norm_w[l], k_norm_w[l],
                   sinks[l], w_pool[l], pool_scale[l], w_attn_up[l], w_pool_up[l], w_out[l],
                   norm2_w[l], w_router_group[l], b_router_group[l], w_router_expert[l],
                   b_router_expert[l], w_gate[l], w_up[l], w_down[l])
    return x
```

```python
import functools

import jax
import jax.numpy as jnp
import numpy as np
from jax import lax
from jax.experimental import pallas as pl
from jax.experimental.pallas import tpu as pltpu

F32 = jnp.float32
BF16 = jnp.bfloat16
U32 = jnp.uint32
I32 = jnp.int32

HEAD_DIM = 64
N_Q_HEADS = 32
N_KV_HEADS = 4
Q_PER_KV = N_Q_HEADS // N_KV_HEADS
ATTN_BLOCK = 128
WINDOW = 128
POOL_WINDOWS = (2, 4, 8, 16)
POOL_HALO = 16
N_GROUPS = 8
EXPERTS_PER_GROUP = 8
N_EXPERTS = N_GROUPS * EXPERTS_PER_GROUP
N_ROUTER = N_GROUPS + N_EXPERTS
TOP_K = 2
EPS = 1e-6
NEG_INF = -1e30

LANES = 128
EXPERT_BLOCK = 256
MIB = 1024 * 1024


def _params(semantics, vmem_mib):
    return pltpu.CompilerParams(dimension_semantics=semantics, vmem_limit_bytes=vmem_mib * MIB)


def _pick(n, candidates):
    for c in candidates:
        if n % c == 0:
            return c
    return n


def _ada_kernel(c_ref, w_ref, b_ref, o_ref):
    c = c_ref[...]
    s = (c * jax.nn.sigmoid(c)).astype(BF16)
    o_ref[...] = jnp.dot(s, w_ref[...].astype(BF16), preferred_element_type=F32) + b_ref[...]


def _ada(c8, w, b):
    d, n = w.shape
    tn = _pick(n, (1024, 512, 256, 128))
    return pl.pallas_call(
        _ada_kernel,
        out_shape=jax.ShapeDtypeStruct((8, n), F32),
        grid=(n // tn,),
        in_specs=[pl.BlockSpec((8, d), lambda j: (0, 0)),
                  pl.BlockSpec((d, tn), lambda j: (0, j)),
                  pl.BlockSpec((1, tn), lambda j: (0, j))],
        out_specs=pl.BlockSpec((8, tn), lambda j: (0, j)),
        compiler_params=_params(("arbitrary",), 56),
    )(c8, w, b)


def _modulated_norm(x, nw, scale, shift):
    ms = jnp.mean(x * x, axis=-1, keepdims=True)
    y = x * lax.rsqrt(ms + EPS) * nw
    return y * (1.0 + scale) + shift


NORM_ROWS = 16


def _pack_halves(v):
    n = v.shape[1] // 2
    bits = lax.bitcast_convert_type(v.astype(BF16).astype(F32), U32)
    return (bits[:, :n] >> 16) | (bits[:, n:] & jnp.uint32(0xFFFF0000))


def _unpack_halves(p):
    lo = lax.bitcast_convert_type(p << 16, F32)
    hi = lax.bitcast_convert_type(p & jnp.uint32(0xFFFF0000), F32)
    return lo, hi


def _norm_kernel(x_ref, nw_ref, sc_ref, sh_ref, o_ref):
    def body(r, carry):
        rows = pl.ds(pl.multiple_of(r * NORM_ROWS, NORM_ROWS), NORM_ROWS)
        o_ref[rows, :] = _modulated_norm(x_ref[rows, :], nw_ref[...], sc_ref[0],
                                         sh_ref[0]).astype(o_ref.dtype)
        return carry

    lax.fori_loop(0, x_ref.shape[0] // NORM_ROWS, body, 0)


def _norm(x2, nw, scale, shift, seq):
    t, d = x2.shape
    tm = _pick(seq, (512, 256, 128))
    rows_per_batch = seq // tm
    return pl.pallas_call(
        _norm_kernel,
        out_shape=jax.ShapeDtypeStruct((t, d), BF16),
        grid=(t // tm,),
        in_specs=[pl.BlockSpec((tm, d), lambda i: (i, 0)),
                  pl.BlockSpec((1, d), lambda i: (0, 0)),
                  pl.BlockSpec((1, 1, d), lambda i: (i // rows_per_batch, 0, 0)),
                  pl.BlockSpec((1, 1, d), lambda i: (i // rows_per_batch, 0, 0))],
        out_specs=pl.BlockSpec((tm, d), lambda i: (i, 0)),
        compiler_params=_params(("parallel",), 40),
    )(x2, nw, scale, shift)


def _in_kernel(a_ref, w_ref, wd_ref, o_ref, wd_o):
    wd_o[...] = wd_ref[...].astype(BF16)
    o_ref[...] = jnp.dot(a_ref[...], w_ref[...], preferred_element_type=F32).astype(o_ref.dtype)


def _in_proj(h, w, w_down):
    t, d = h.shape
    n = w.shape[1]
    ne, ff, _ = w_down.shape
    tm = _pick(t, (1024, 512, 256, 128))
    tn = _pick(n, (512, 256, 128))
    nj = n // tn
    steps = (t // tm) * nj
    conv_steps = 1 << (steps.bit_length() - 1)
    rows = ne * ff // conv_steps
    wd2 = w_down.reshape(ne * ff, d)

    def wd_map(i, j):
        return (jnp.minimum(i * nj + j, conv_steps - 1), 0)

    proj, wd_b = pl.pallas_call(
        _in_kernel,
        out_shape=(jax.ShapeDtypeStruct((t, n), BF16), jax.ShapeDtypeStruct(wd2.shape, BF16)),
        grid=(t // tm, nj),
        in_specs=[pl.BlockSpec((tm, d), lambda i, j: (i, 0)),
                  pl.BlockSpec((d, tn), lambda i, j: (0, j)),
                  pl.BlockSpec((rows, d), wd_map)],
        out_specs=(pl.BlockSpec((tm, tn), lambda i, j: (i, j)),
                   pl.BlockSpec((rows, d), wd_map)),
        compiler_params=_params(("arbitrary", "arbitrary"), 56),
    )(h, w, wd2)
    return proj, wd_b.reshape(ne, ff, d)


def _head_sumsq(v, ones_bd):
    sq = v * v
    hi = sq.astype(BF16)
    lo = (sq - hi.astype(F32)).astype(BF16)
    return (jnp.dot(hi, ones_bd, preferred_element_type=F32)
            + jnp.dot(lo, ones_bd, preferred_element_type=F32))


def _head_rms_norm(v, w2, ones_bd):
    ss = _head_sumsq(v, ones_bd)
    return v * lax.rsqrt(ss * (1.0 / HEAD_DIM) + EPS) * w2


def _attn_kernel(sink_ref, q_ref, kvc_ref, kvp_ref, bias_ref, qw_ref, kw_ref, wg_ref, wu_ref,
                 o_ref, wg_o, wu_o):
    wg_o[...] = wg_ref[...].astype(BF16)
    wu_o[...] = wu_ref[...].astype(BF16)

    blk = ATTN_BLOCK
    kvw = N_KV_HEADS * HEAD_DIM
    lane = lax.broadcasted_iota(I32, (1, LANES), 1)
    low_half = lane < HEAD_DIM
    r = lax.broadcasted_iota(I32, (LANES, LANES), 0) // HEAD_DIM
    c = lax.broadcasted_iota(I32, (LANES, LANES), 1) // HEAD_DIM
    ones_bd = jnp.where(r == c, 1.0, 0.0).astype(BF16)

    kv = jnp.concatenate([kvp_ref[0], kvc_ref[0]], axis=0).astype(F32)
    kw2 = kw_ref[...]
    sum_a = jnp.broadcast_to(jnp.where(low_half, 1.0, 0.0), (2 * blk, LANES))
    sum_b = jnp.broadcast_to(jnp.where(low_half, 0.0, 1.0), (2 * blk, LANES))

    for g in range(N_KV_HEADS):
        chunk = g // 2
        kc = _head_rms_norm(kv[:, chunk * LANES:(chunk + 1) * LANES], kw2, ones_bd)
        vc = kv[:, kvw + chunk * LANES: kvw + (chunk + 1) * LANES]
        kr = pltpu.roll(kc, HEAD_DIM, axis=1)
        vr = pltpu.roll(vc, HEAD_DIM, axis=1)
        if g % 2 == 0:
            k_lo, k_hi, v_lo, v_hi = kc, kr, vc, vr
        else:
            k_lo, k_hi, v_lo, v_hi = kr, kc, vr, vc
        k_bd = jnp.concatenate([jnp.where(low_half, k_lo, 0.0),
                                jnp.where(low_half, 0.0, k_hi)], axis=0).astype(BF16)
        v_bd = jnp.concatenate(
            [jnp.concatenate([jnp.where(low_half, v_lo, 0.0), sum_a], axis=1),
             jnp.concatenate([jnp.where(low_half, 0.0, v_hi), sum_b], axis=1)],
            axis=0).astype(BF16)

        pairs = [g * (Q_PER_KV // 2) + jj for jj in range(Q_PER_KV // 2)]
        scores = []
        for j in pairs:
            qp = q_ref[0, :, j * LANES:(j + 1) * LANES].astype(F32)
            qn = _head_rms_norm(qp, qw_ref[...], ones_bd).astype(BF16)
            s = lax.dot_general(qn, k_bd, (((1,), (1,)), ((), ())),
                                preferred_element_type=F32)
            scores.append(s + bias_ref[0, j])
        probs = []
        for j, s in zip(pairs, scores):
            sink_a = sink_ref[2 * j]
            sink_b = sink_ref[2 * j + 1]
            m_a = jnp.maximum(jnp.max(s[:, :2 * blk], axis=-1, keepdims=True), sink_a)
            m_b = jnp.maximum(jnp.max(s[:, 2 * blk:], axis=-1, keepdims=True), sink_b)
            p = jnp.concatenate([jnp.exp(s[:, :2 * blk] - m_a),
                                 jnp.exp(s[:, 2 * blk:] - m_b)], axis=1).astype(BF16)
            esink = jnp.where(low_half, jnp.exp(sink_a - m_a), jnp.exp(sink_b - m_b))
            probs.append((p, esink))
        for j, (p, esink) in zip(pairs, probs):
            ol = jnp.dot(p, v_bd, preferred_element_type=F32)
            o = ol[:, :LANES] / (ol[:, LANES:] + esink)
            o_ref[0, :, j * LANES:(j + 1) * LANES] = o.astype(o_ref.dtype)


def _attn_bias():
    blk = ATTN_BLOCK
    i = np.arange(blk)[:, None]
    j = np.arange(2 * blk)[None, :]
    dist = i + blk - j
    band = (dist >= 0) & (dist < WINDOW)
    slopes = (2.0 ** (-8.0 * np.arange(1, N_Q_HEADS + 1, dtype=np.float32) / N_Q_HEADS)).astype(np.float32)
    alibi = -slopes[:, None, None] * dist.astype(np.float32)[None]
    later = np.where(band[None], alibi, np.float32(NEG_INF))
    first = np.where((band & (j >= blk))[None], alibi, np.float32(NEG_INF))
    tab = np.stack([first, later]).astype(np.float32)
    tab = tab.reshape(2, N_Q_HEADS // 2, 2, blk, 2 * blk)
    return np.transpose(tab, (0, 1, 3, 2, 4)).reshape(2, N_Q_HEADS // 2, blk, 4 * blk)


def _attention(proj3, sinks, q_norm_w, k_norm_w, q_col, kv_col, w_gate, w_up):
    b, s, _ = proj3.shape
    aw = N_Q_HEADS * HEAD_DIM
    kvw2 = 2 * N_KV_HEADS * HEAD_DIM
    nb = s // ATTN_BLOCK
    bias = _attn_bias()
    qw2 = (jnp.tile(q_norm_w, 2) * (HEAD_DIM ** -0.5)).reshape(1, LANES)
    kw2 = jnp.tile(k_norm_w, 2).reshape(1, LANES)
    ne, d, ff = w_gate.shape
    steps = b * nb
    wg2 = w_gate.reshape(ne * d, ff)
    wu2 = w_up.reshape(ne * d, ff)
    rows_in = ne * d // steps

    def w_map(bi, n, sk):
        return (bi * nb + n, 0)

    w_specs = [pl.BlockSpec((rows_in, ff), w_map), pl.BlockSpec((rows_in, ff), w_map)]
    grid_spec = pltpu.PrefetchScalarGridSpec(
        num_scalar_prefetch=1,
        grid=(b, nb),
        in_specs=[pl.BlockSpec((1, ATTN_BLOCK, aw), lambda bi, n, sk: (bi, n, q_col // aw)),
                  pl.BlockSpec((1, ATTN_BLOCK, kvw2), lambda bi, n, sk: (bi, n, kv_col // kvw2)),
                  pl.BlockSpec((1, ATTN_BLOCK, kvw2),
                               lambda bi, n, sk: (bi, jnp.maximum(n - 1, 0), kv_col // kvw2)),
                  pl.BlockSpec((1, N_Q_HEADS // 2, ATTN_BLOCK, 4 * ATTN_BLOCK),
                               lambda bi, n, sk: (jnp.minimum(n, 1), 0, 0, 0)),
                  pl.BlockSpec((1, LANES), lambda bi, n, sk: (0, 0)),
                  pl.BlockSpec((1, LANES), lambda bi, n, sk: (0, 0))] + w_specs,
        out_specs=tuple([pl.BlockSpec((1, ATTN_BLOCK, aw), lambda bi, n, sk: (bi, n, 0))] + w_specs),
    )
    y, wg_b, wu_b = pl.pallas_call(
        _attn_kernel,
        out_shape=(jax.ShapeDtypeStruct((b, s, aw), BF16),
                   jax.ShapeDtypeStruct(wg2.shape, BF16),
                   jax.ShapeDtypeStruct(wu2.shape, BF16)),
        grid_spec=grid_spec,
        compiler_params=_params(("arbitrary", "arbitrary"), 56),
    )(sinks, proj3, proj3, proj3, bias, qw2, kw2, wg2, wu2)
    return y, wg_b.reshape(ne, d, ff), wu_b.reshape(ne, d, ff)


def _pool_kernel(pc_ref, ph_ref, w_ref, ps_ref, o_ref):
    n = pl.program_id(1)
    ts = pc_ref.shape[1]
    cg = w_ref.shape[1]
    cur = pc_ref[0].astype(F32)
    halo = jnp.where(n > 0, ph_ref[0].astype(F32), 0.0)
    ext = jnp.concatenate([halo, cur], axis=0)
    t1 = (lax.broadcasted_iota(I32, (ts, 1), 0) + n * ts + 1).astype(F32)
    for g, win in enumerate(POOL_WINDOWS):
        acc = ext[:, g * cg:(g + 1) * cg]
        k = 1
        while k < win:
            acc = acc + pltpu.roll(acc, k, axis=0)
            k *= 2
        mean = acc[POOL_HALO:] / jnp.minimum(t1, float(win))
        d = (mean - cur[:, g * cg:(g + 1) * cg]).astype(BF16)
        y = jnp.dot(d, w_ref[g], preferred_element_type=F32) * ps_ref[:, g * cg:(g + 1) * cg]
        o_ref[0, :, g * cg:(g + 1) * cg] = y.astype(o_ref.dtype)


def _pool(proj3, w_pool, pool_scale, p_col):
    b, s, _ = proj3.shape
    ng, cg, _ = w_pool.shape
    pw = ng * cg
    ts = _pick(s, (512, 256, 128))
    return pl.pallas_call(
        _pool_kernel,
        out_shape=jax.ShapeDtypeStruct((b, s, pw), BF16),
        grid=(b, s // ts),
        in_specs=[pl.BlockSpec((pl.Element(1), pl.Element(ts), pl.Element(pw)),
                               lambda bi, n: (bi, pl.multiple_of(n * ts, ts), p_col)),
                  pl.BlockSpec((pl.Element(1), pl.Element(POOL_HALO), pl.Element(pw)),
                               lambda bi, n: (bi, pl.multiple_of(jnp.maximum(n * ts - POOL_HALO, 0),
                                                                 POOL_HALO), p_col)),
                  pl.BlockSpec((ng, cg, cg), lambda bi, n: (0, 0, 0)),
                  pl.BlockSpec((1, pw), lambda bi, n: (0, 0))],
        out_specs=pl.BlockSpec((1, ts, pw), lambda bi, n: (bi, n, 0)),
        compiler_params=_params(("parallel", "arbitrary"), 40),
    )(proj3, proj3, w_pool, pool_scale)


def _mix_kernel(a_ref, b_ref, wa_ref, wb_ref, ga_ref, gb_ref, o_ref):
    ya = jnp.dot(a_ref[...], wa_ref[...], preferred_element_type=F32)
    yb = jnp.dot(b_ref[...], wb_ref[...], preferred_element_type=F32)
    ga = jax.nn.sigmoid(ga_ref[...].astype(F32))
    gb = jax.nn.sigmoid(gb_ref[...].astype(F32))
    o_ref[...] = (ga * ya + gb * yb).astype(o_ref.dtype)


def _mix(y_attn, y_pool, wa, wb, proj, ga_col, gb_col):
    t, aw = y_attn.shape
    pw = y_pool.shape[1]
    d = wa.shape[1]
    tm = _pick(t, (512, 256, 128))
    tn = _pick(d, (1024, 512, 256, 128))
    return pl.pallas_call(
        _mix_kernel,
        out_shape=jax.ShapeDtypeStruct((t, d), BF16),
        grid=(t // tm, d // tn),
        in_specs=[pl.BlockSpec((tm, aw), lambda i, j: (i, 0)),
                  pl.BlockSpec((tm, pw), lambda i, j: (i, 0)),
                  pl.BlockSpec((aw, tn), lambda i, j: (0, j)),
                  pl.BlockSpec((pw, tn), lambda i, j: (0, j)),
                  pl.BlockSpec((pl.Element(tm), pl.Element(tn)),
                               lambda i, j: (pl.multiple_of(i * tm, tm),
                                             pl.multiple_of(ga_col + j * tn, LANES))),
                  pl.BlockSpec((pl.Element(tm), pl.Element(tn)),
                               lambda i, j: (pl.multiple_of(i * tm, tm),
                                             pl.multiple_of(gb_col + j * tn, LANES)))],
        out_specs=pl.BlockSpec((tm, tn), lambda i, j: (i, j)),
        compiler_params=_params(("parallel", "arbitrary"), 56),
    )(y_attn, y_pool, wa, wb, proj, proj)


def _out_kernel(m_ref, w_ref, x_ref, g_ref, o_ref):
    y = jnp.dot(m_ref[...], w_ref[...], preferred_element_type=F32)
    o_ref[...] = x_ref[...] + g_ref[0] * y


def _out_proj(mixed, w, x2, gate, seq):
    t, d = x2.shape
    tm = _pick(seq, (512, 256, 128))
    tn = _pick(d, (1024, 512, 256, 128))
    rows_per_batch = seq // tm
    return pl.pallas_call(
        _out_kernel,
        out_shape=jax.ShapeDtypeStruct((t, d), F32),
        grid=(t // tm, d // tn),
        in_specs=[pl.BlockSpec((tm, d), lambda i, j: (i, 0)),
                  pl.BlockSpec((d, tn), lambda i, j: (0, j)),
                  pl.BlockSpec((tm, tn), lambda i, j: (i, j)),
                  pl.BlockSpec((1, 1, tn), lambda i, j: (i // rows_per_batch, 0, j))],
        out_specs=pl.BlockSpec((tm, tn), lambda i, j: (i, j)),
        compiler_params=_params(("parallel", "arbitrary"), 56),
    )(mixed, w, x2, gate)


def _first_argmax(v, iota, n):
    m = jnp.max(v, axis=0, keepdims=True)
    idx = jnp.min(jnp.where(v == m, iota, n), axis=0, keepdims=True)
    return m, idx


def _router_kernel(x_ref, nw_ref, sc_ref, sh_ref, w2_ref, br_ref,
                   hp_ref, mi_ref, mf_ref, cnt_ref, carry_ref, hhi_ref, hlo_ref):
    step = pl.program_id(0)
    tm, d = x_ref.shape

    @pl.when(step == 0)
    def _():
        carry_ref[...] = jnp.zeros_like(carry_ref)

    def body(r, carry):
        rows = pl.ds(pl.multiple_of(r * NORM_ROWS, NORM_ROWS), NORM_ROWS)
        h = _modulated_norm(x_ref[rows, :], nw_ref[...], sc_ref[0], sh_ref[0])
        hi = h.astype(BF16)
        hhi_ref[rows, :] = hi
        hlo_ref[rows, :] = (h - hi.astype(F32)).astype(BF16)
        hp_ref[rows, :] = _pack_halves(h)
        return carry

    lax.fori_loop(0, tm // NORM_ROWS, body, 0)

    l2 = (jnp.dot(hhi_ref[...], w2_ref[...], preferred_element_type=F32)
          + jnp.dot(hlo_ref[...], w2_ref[...], preferred_element_type=F32))
    lt = l2[:, :LANES] + l2[:, LANES:]
    logits = lt.T[:N_ROUTER] + br_ref[...]
    iota8 = lax.broadcasted_iota(I32, (N_GROUPS, tm), 0)
    gl = logits[:N_GROUPS]
    g_max, g_sel = _first_argmax(gl, iota8, N_GROUPS)
    p_group = 1.0 / jnp.sum(jnp.exp(gl - g_max), axis=0, keepdims=True)
    e_in = jnp.zeros((EXPERTS_PER_GROUP, tm), F32)
    for g in range(N_GROUPS):
        lo = N_GROUPS + g * EXPERTS_PER_GROUP
        e_in = e_in + jnp.where(g_sel == g, logits[lo:lo + EXPERTS_PER_GROUP], 0.0)
    v0, i0 = _first_argmax(e_in, iota8, EXPERTS_PER_GROUP)
    rest = jnp.where(iota8 == i0, -jnp.inf, e_in)
    v1, i1 = _first_argmax(rest, iota8, EXPERTS_PER_GROUP)
    t = jnp.exp(v1 - v0)
    w0 = p_group / (1.0 + t)
    w1 = p_group * t / (1.0 + t)
    e0 = g_sel * EXPERTS_PER_GROUP + i0
    e1 = g_sel * EXPERTS_PER_GROUP + i1

    iota_e = lax.broadcasted_iota(I32, (N_EXPERTS, tm), 0)
    hit0 = iota_e == e0
    hit1 = iota_e == e1
    onehot = jnp.where(hit0 | hit1, 1.0, 0.0).astype(BF16)
    rr = lax.broadcasted_iota(I32, (tm, tm), 0)
    cc = lax.broadcasted_iota(I32, (tm, tm), 1)
    before = jnp.where(rr < cc, 1.0, 0.0).astype(BF16)
    prior = jnp.dot(onehot, before, preferred_element_type=F32) + carry_ref[:, :1]
    r0 = jnp.sum(jnp.where(hit0, prior, 0.0), axis=0, keepdims=True)
    r1 = jnp.sum(jnp.where(hit1, prior, 0.0), axis=0, keepdims=True)
    carry_ref[...] += jnp.dot(onehot, jnp.ones((tm, LANES), BF16), preferred_element_type=F32)

    zi = jnp.zeros((1, tm), I32)
    zf = jnp.zeros((1, tm), F32)
    mi_ref[...] = jnp.concatenate([e0, e1, r0.astype(I32), r1.astype(I32), zi, zi, zi, zi], axis=0)
    mf_ref[...] = jnp.concatenate([w0, w1, zf, zf, zf, zf, zf, zf], axis=0)
    cnt_ref[...] = carry_ref[...]


def _router(x1, nw, scale, shift, w2, br, seq):
    t, d = x1.shape
    tm = _pick(seq, (512, 256, 128))
    rows_per_batch = seq // tm
    return pl.pallas_call(
        _router_kernel,
        out_shape=(jax.ShapeDtypeStruct((t, d // 2), U32),
                   jax.ShapeDtypeStruct((8, t), I32),
                   jax.ShapeDtypeStruct((8, t), F32),
                   jax.ShapeDtypeStruct((N_EXPERTS, LANES), F32)),
        grid=(t // tm,),
        in_specs=[pl.BlockSpec((tm, d), lambda i: (i, 0)),
                  pl.BlockSpec((1, d), lambda i: (0, 0)),
                  pl.BlockSpec((1, 1, d), lambda i: (i // rows_per_batch, 0, 0)),
                  pl.BlockSpec((1, 1, d), lambda i: (i // rows_per_batch, 0, 0)),
                  pl.BlockSpec((d, 2 * LANES), lambda i: (0, 0)),
                  pl.BlockSpec((N_ROUTER, 1), lambda i: (0, 0))],
        out_specs=(pl.BlockSpec((tm, d // 2), lambda i: (i, 0)),
                   pl.BlockSpec((8, tm), lambda i: (0, i)),
                   pl.BlockSpec((8, tm), lambda i: (0, i)),
                   pl.BlockSpec((N_EXPERTS, LANES), lambda i: (0, 0))),
        scratch_shapes=[pltpu.VMEM((N_EXPERTS, LANES), F32), pltpu.VMEM((tm, d), BF16),
                        pltpu.VMEM((tm, d), BF16)],
        compiler_params=_params(("arbitrary",), 56),
    )(x1, nw, scale, shift, w2, br)


PLAN_LANES = 2048
DMA_UNROLL = 4


def _plan_kernel(mi_ref, cnt_ref, dest_ref, tab_ref):
    cnt = cnt_ref[...]
    nb = jnp.floor((cnt + (EXPERT_BLOCK - 1)) * (1.0 / EXPERT_BLOCK))
    r = lax.broadcasted_iota(I32, (N_EXPERTS, N_EXPERTS), 0)
    c = lax.broadcasted_iota(I32, (N_EXPERTS, N_EXPERTS), 1)
    incl = jnp.where(c <= r, 1.0, 0.0).astype(BF16)
    bend = jnp.dot(incl, nb.astype(BF16), preferred_element_type=F32)
    bstart = bend - nb
    first_row = bstart[:, :1] * EXPERT_BLOCK

    tl = mi_ref.shape[1]
    iota_e = lax.broadcasted_iota(I32, (N_EXPERTS, tl), 0)
    for k in range(TOP_K):
        hit = iota_e == mi_ref[k:k + 1, :]
        base = jnp.sum(jnp.where(hit, first_row, 0.0), axis=0, keepdims=True)
        dest_ref[k:k + 1, :] = base.astype(I32) + mi_ref[TOP_K + k:TOP_K + k + 1, :]

    nl = tab_ref.shape[1]
    bidx = lax.broadcasted_iota(I32, (1, nl), 1).astype(F32)
    blk_e = jnp.sum(jnp.where(bend[:, :1] <= bidx, 1.0, 0.0), axis=0, keepdims=True)
    blk_e = jnp.minimum(blk_e, N_EXPERTS - 1.0)
    hit = lax.broadcasted_iota(I32, (N_EXPERTS, nl), 0) == blk_e.astype(I32)
    cnt_b = jnp.sum(jnp.where(hit, cnt[:, :1], 0.0), axis=0, keepdims=True)
    start_b = jnp.sum(jnp.where(hit, bstart[:, :1], 0.0), axis=0, keepdims=True)
    valid = jnp.clip(cnt_b - (bidx - start_b) * EXPERT_BLOCK, 0.0, float(EXPERT_BLOCK))
    n_used = jnp.broadcast_to(jnp.max(bend[:, :1], axis=0, keepdims=True), (1, nl))
    zero = jnp.zeros((1, nl), F32)
    tab_ref[...] = jnp.concatenate([blk_e, valid, n_used, zero, zero, zero, zero, zero],
                                   axis=0).astype(I32)


def _plan(meta_i, cnt, nblk):
    t = meta_i.shape[1]
    tl = _pick(t, (PLAN_LANES, 1024, 512, 256, 128))
    nl = pl.cdiv(nblk, LANES) * LANES
    return pl.pallas_call(
        _plan_kernel,
        out_shape=(jax.ShapeDtypeStruct((TOP_K, t), I32), jax.ShapeDtypeStruct((8, nl), I32)),
        grid=(t // tl,),
        in_specs=[pl.BlockSpec((8, tl), lambda i: (0, i)),
                  pl.BlockSpec((N_EXPERTS, LANES), lambda i: (0, 0))],
        out_specs=(pl.BlockSpec((TOP_K, tl), lambda i: (0, i)),
                   pl.BlockSpec((8, nl), lambda i: (0, 0))),
        compiler_params=_params(("arbitrary",), 32),
    )(meta_i, cnt)


def _dispatch_kernel(dest_ref, hp_ref, xs_ref, sem, *, chunk, t):
    base = pl.program_id(0) * chunk

    def issue(i, carry):
        for k in range(TOP_K):
            pltpu.make_async_copy(hp_ref.at[pl.ds(i, 1)],
                                  xs_ref.at[pl.ds(dest_ref[k * t + base + i], 1)],
                                  sem).start(priority=k)
        return carry

    lax.fori_loop(0, chunk, issue, 0, unroll=DMA_UNROLL)
    for k in range(TOP_K):
        pltpu.make_async_copy(hp_ref, xs_ref.at[pl.ds(0, chunk)], sem).wait()


def _dispatch(dest_flat, hp, cap):
    t, half = hp.shape
    chunk = _pick(t, (512, 256, 128))
    grid_spec = pltpu.PrefetchScalarGridSpec(
        num_scalar_prefetch=1,
        grid=(t // chunk,),
        in_specs=[pl.BlockSpec((chunk, half), lambda i, ds: (i, 0))],
        out_specs=pl.BlockSpec(memory_space=pl.ANY),
        scratch_shapes=[pltpu.SemaphoreType.DMA(())],
    )
    return pl.pallas_call(
        functools.partial(_dispatch_kernel, chunk=chunk, t=t),
        out_shape=jax.ShapeDtypeStruct((cap, half), U32),
        grid_spec=grid_spec,
        compiler_params=pltpu.CompilerParams(dimension_semantics=("arbitrary",),
                                             has_side_effects=True, vmem_limit_bytes=32 * MIB),
    )(dest_flat, hp)


def _ffn_kernel(be_ref, nv_ref, nu_ref, xs_ref, wg_ref, wu_ref, wd_ref, o_ref):
    b = pl.program_id(0)

    @pl.when(b < nu_ref[0])
    def _():
        blk, half = xs_ref.shape
        row = lax.broadcasted_iota(I32, (blk, 1), 0)
        xp = jnp.where(row < nv_ref[b], xs_ref[...], jnp.uint32(0))
        lo, hi = _unpack_halves(xp)
        lo = lo.astype(BF16)
        hi = hi.astype(BF16)
        a = (jnp.dot(lo, wg_ref[0, :half], preferred_element_type=F32)
             + jnp.dot(hi, wg_ref[0, half:], preferred_element_type=F32))
        u = (jnp.dot(lo, wu_ref[0, :half], preferred_element_type=F32)
             + jnp.dot(hi, wu_ref[0, half:], preferred_element_type=F32))
        mid = (a * jax.nn.sigmoid(a) * u).astype(BF16)
        o_ref[...] = _pack_halves(jnp.dot(mid, wd_ref[0], preferred_element_type=F32))


def _experts(blk_e, blk_valid, n_used, xs, wg, wu, wd):
    cap, half = xs.shape
    d = 2 * half
    ff = wg.shape[2]
    nblk = cap // EXPERT_BLOCK

    def row_map(b, be, nv, nu):
        return (jnp.minimum(b, nu[0] - 1), 0)

    def w_map(b, be, nv, nu):
        return (be[jnp.minimum(b, nu[0] - 1)], 0, 0)

    grid_spec = pltpu.PrefetchScalarGridSpec(
        num_scalar_prefetch=3,
        grid=(nblk,),
        in_specs=[pl.BlockSpec((EXPERT_BLOCK, half), row_map),
                  pl.BlockSpec((1, d, ff), w_map),
                  pl.BlockSpec((1, d, ff), w_map),
                  pl.BlockSpec((1, ff, d), w_map)],
        out_specs=pl.BlockSpec((EXPERT_BLOCK, half), row_map),
    )
    return pl.pallas_call(
        _ffn_kernel,
        out_shape=jax.ShapeDtypeStruct((cap, half), U32),
        grid_spec=grid_spec,
        compiler_params=_params(("arbitrary",), 56),
    )(blk_e, blk_valid, n_used, xs, wg, wu, wd)


def _combine_kernel(dest_ref, x_ref, g_ref, w_ref, ys_ref, o_ref, ybuf, sem, *, chunk, t):
    i = pl.program_id(0)
    n = pl.num_programs(0)

    def issue(step, slot):
        base = step * chunk

        def body(r, carry):
            for k in range(TOP_K):
                pltpu.make_async_copy(ys_ref.at[pl.ds(dest_ref[k * t + base + r], 1)],
                                      ybuf.at[slot, k, pl.ds(r, 1)],
                                      sem.at[slot]).start(priority=k)
            return carry

        lax.fori_loop(0, chunk, body, 0, unroll=DMA_UNROLL)

    @pl.when(i == 0)
    def _():
        issue(0, 0)

    slot = i % 2

    @pl.when(i + 1 < n)
    def _():
        issue(i + 1, 1 - slot)

    for k in range(TOP_K):
        pltpu.make_async_copy(ys_ref.at[pl.ds(0, chunk)], ybuf.at[slot, k], sem.at[slot]).wait()
    w = w_ref[...]
    half = ybuf.shape[-1]
    lo0, hi0 = _unpack_halves(ybuf[slot, 0])
    lo1, hi1 = _unpack_halves(ybuf[slot, 1])
    g = g_ref[0]
    o_ref[:, :half] = x_ref[:, :half] + g[:, :half] * (w[:, 0:1] * lo0 + w[:, 1:2] * lo1)
    o_ref[:, half:] = x_ref[:, half:] + g[:, half:] * (w[:, 0:1] * hi0 + w[:, 1:2] * hi1)


def _combine(dest_flat, x1, gate, w_rows, ys, seq):
    t, d = x1.shape
    chunk = _pick(seq, (256, 128))
    rows_per_batch = seq // chunk
    grid_spec = pltpu.PrefetchScalarGridSpec(
        num_scalar_prefetch=1,
        grid=(t // chunk,),
        in_specs=[pl.BlockSpec((chunk, d), lambda i, ds: (i, 0)),
                  pl.BlockSpec((1, 1, d), lambda i, ds: (i // rows_per_batch, 0, 0)),
                  pl.BlockSpec((chunk, 8), lambda i, ds: (i, 0)),
                  pl.BlockSpec(memory_space=pl.ANY)],
        out_specs=pl.BlockSpec((chunk, d), lambda i, ds: (i, 0)),
        scratch_shapes=[pltpu.VMEM((2, TOP_K, chunk, d // 2), U32),
                        pltpu.SemaphoreType.DMA((2,))],
    )
    return pl.pallas_call(
        functools.partial(_combine_kernel, chunk=chunk, t=t),
        out_shape=jax.ShapeDtypeStruct((t, d), F32),
        grid_spec=grid_spec,
        compiler_params=_params(("arbitrary",), 56),
    )(dest_flat, x1, gate, w_rows, ys)


def _layer(x, c, w_ada, b_ada, norm1_w, w_in, q_norm_w, k_norm_w, sinks, w_pool, pool_scale,
           w_attn_up, w_pool_up, w_out, norm2_w, w_router_group, b_router_group,
           w_router_expert, b_router_expert, w_gate, w_up, w_down):
    b, s, d = x.shape
    t = b * s
    aw = w_attn_up.shape[0]
    pw = w_pool_up.shape[0]
    kvw = N_KV_HEADS * HEAD_DIM
    x2 = x.reshape(t, d)

    c8 = jnp.zeros((8, d), F32).at[:b].set(c)
    ada = _ada(c8, w_ada, b_ada.reshape(1, 6 * d))[:b]
    shift1, scale1, gate1, shift2, scale2, gate2 = [a.reshape(b, 1, d) for a in jnp.split(ada, 6, axis=-1)]

    q_col, kv_col = 0, aw
    p_col = aw + 2 * kvw
    ga_col = p_col + pw
    gb_col = ga_col + d
    h = _norm(x2, norm1_w.reshape(1, d), scale1, shift1, s)
    proj, wd_b = _in_proj(h, w_in.astype(BF16), w_down)
    proj3 = proj.reshape(b, s, proj.shape[1])

    y_attn, wg_b, wu_b = _attention(proj3, sinks, q_norm_w, k_norm_w, q_col, kv_col, w_gate, w_up)
    y_attn = y_attn.reshape(t, aw)
    y_pool = _pool(proj3, w_pool.astype(BF16), pool_scale.reshape(1, pw), p_col).reshape(t, pw)
    mixed = _mix(y_attn, y_pool, w_attn_up.astype(BF16), w_pool_up.astype(BF16), proj, ga_col, gb_col)
    x1 = _out_proj(mixed, w_out.astype(BF16), x2, gate1, s)

    wr = jnp.concatenate([w_router_group, w_router_expert,
                          jnp.zeros((d, LANES - N_ROUTER), F32)], axis=1)
    wr_hi = wr.astype(BF16)
    wr_lo = (wr - wr_hi.astype(F32)).astype(BF16)
    wr2 = jnp.concatenate([wr_hi, wr_lo], axis=1)
    br = jnp.concatenate([b_router_group, b_router_expert]).reshape(N_ROUTER, 1)
    hp, meta_i, meta_f, cnt = _router(x1, norm2_w.reshape(1, d), scale2, shift2, wr2, br, s)

    cap = TOP_K * t + N_EXPERTS * EXPERT_BLOCK
    nblk = cap // EXPERT_BLOCK
    dest, tab = _plan(meta_i, cnt, nblk)
    dest_flat = dest.reshape(TOP_K * t)
    blk_e, blk_valid, n_used = tab[0, :nblk], tab[1, :nblk], tab[2, :1]

    xs = _dispatch(dest_flat, hp, cap)
    ys = _experts(blk_e, blk_valid, n_used, xs, wg_b, wu_b, wd_b)
    out = _combine(dest_flat, x1, gate2, meta_f.T, ys, s)
    return out.reshape(b, s, d)


def kernel(x, c, w_ada, b_ada, norm1_w, w_in, q_norm_w, k_norm_w, sinks, w_pool, pool_scale,
           w_attn_up, w_pool_up, w_out, norm2_w, w_router_group, b_router_group,
           w_router_expert, b_router_expert, w_gate, w_up, w_down):
    for l in range(w_ada.shape[0]):
        x = _layer(x, c, w_ada[l], b_ada[l], norm1_w[l], w_in[l], q_norm_w[l], k_norm_w[l],
                   sinks[l], w_pool[l], pool_scale[l], w_attn_up[l], w_pool_up[l], w_out[l],
                   norm2_w[l], w_router_group[l], b_router_group[l], w_router_expert[l],
                   b_router_expert[l], w_gate[l], w_up[l], w_down[l])
    return x
```

```python
import functools

import jax
import jax.numpy as jnp
import numpy as np
from jax import lax
from jax.experimental import pallas as pl
from jax.experimental.pallas import tpu as pltpu

F32 = jnp.float32
BF16 = jnp.bfloat16
U32 = jnp.uint32
I32 = jnp.int32

HEAD_DIM = 64
N_Q_HEADS = 32
N_KV_HEADS = 4
Q_PER_KV = N_Q_HEADS // N_KV_HEADS
ATTN_BLOCK = 128
WINDOW = 128
POOL_WINDOWS = (2, 4, 8, 16)
POOL_HALO = 16
N_GROUPS = 8
EXPERTS_PER_GROUP = 8
N_EXPERTS = N_GROUPS * EXPERTS_PER_GROUP
N_ROUTER = N_GROUPS + N_EXPERTS
TOP_K = 2
EPS = 1e-6
NEG_INF = -1e30
LOG2E = 1.4426950408889634

LANES = 128
EXPERT_BLOCK = 256
MIB = 1024 * 1024


def _params(semantics, vmem_mib):
    return pltpu.CompilerParams(dimension_semantics=semantics, vmem_limit_bytes=vmem_mib * MIB)


def _pick(n, candidates):
    for c in candidates:
        if n % c == 0:
            return c
    return n


def _ada_kernel(c_ref, w_ref, b_ref, o_ref):
    c = c_ref[...]
    s = (c * jax.nn.sigmoid(c)).astype(BF16)
    o_ref[...] = jnp.dot(s, w_ref[...].astype(BF16), preferred_element_type=F32) + b_ref[...]


def _ada(c8, w, b):
    d, n = w.shape
    tn = _pick(n, (1024, 512, 256, 128))
    return pl.pallas_call(
        _ada_kernel,
        out_shape=jax.ShapeDtypeStruct((8, n), F32),
        grid=(n // tn,),
        in_specs=[pl.BlockSpec((8, d), lambda j: (0, 0)),
                  pl.BlockSpec((d, tn), lambda j: (0, j)),
                  pl.BlockSpec((1, tn), lambda j: (0, j))],
        out_specs=pl.BlockSpec((8, tn), lambda j: (0, j)),
        compiler_params=_params(("arbitrary",), 56),
    )(c8, w, b)


def _modulated_norm(x, gain, shift):
    ms = jnp.mean(x * x, axis=-1, keepdims=True)
    return x * lax.rsqrt(ms + EPS) * gain + shift


NORM_ROWS = 16
NORM_UNROLL = 4


def _pack_halves(v):
    n = v.shape[1] // 2
    bits = lax.bitcast_convert_type(v.astype(BF16).astype(F32), U32)
    return (bits[:, :n] >> 16) | (bits[:, n:] & jnp.uint32(0xFFFF0000))


def _unpack_halves(p):
    lo = lax.bitcast_convert_type(p << 16, F32)
    hi = lax.bitcast_convert_type(p & jnp.uint32(0xFFFF0000), F32)
    return lo, hi


def _norm_kernel(x_ref, nw_ref, sc_ref, sh_ref, o_ref):
    gain = nw_ref[...] * (1.0 + sc_ref[0])
    shift = sh_ref[0]

    def body(r, carry):
        rows = pl.ds(pl.multiple_of(r * NORM_ROWS, NORM_ROWS), NORM_ROWS)
        o_ref[rows, :] = _modulated_norm(x_ref[rows, :], gain, shift).astype(o_ref.dtype)
        return carry

    lax.fori_loop(0, x_ref.shape[0] // NORM_ROWS, body, 0, unroll=NORM_UNROLL)


def _norm(x2, nw, scale, shift, seq):
    t, d = x2.shape
    tm = _pick(seq, (512, 256, 128))
    rows_per_batch = seq // tm
    return pl.pallas_call(
        _norm_kernel,
        out_shape=jax.ShapeDtypeStruct((t, d), BF16),
        grid=(t // tm,),
        in_specs=[pl.BlockSpec((tm, d), lambda i: (i, 0)),
                  pl.BlockSpec((1, d), lambda i: (0, 0)),
                  pl.BlockSpec((1, 1, d), lambda i: (i // rows_per_batch, 0, 0)),
                  pl.BlockSpec((1, 1, d), lambda i: (i // rows_per_batch, 0, 0))],
        out_specs=pl.BlockSpec((tm, d), lambda i: (i, 0)),
        compiler_params=_params(("parallel",), 40),
    )(x2, nw, scale, shift)


def _in_kernel(a_ref, w_ref, wd_ref, o_ref, wd_o):
    wd_o[...] = wd_ref[...].astype(BF16)
    o_ref[...] = jnp.dot(a_ref[...], w_ref[...], preferred_element_type=F32).astype(o_ref.dtype)


def _in_proj(h, w, w_down):
    t, d = h.shape
    n = w.shape[1]
    ne, ff, _ = w_down.shape
    tm = _pick(t, (2048, 1024, 512, 256, 128))
    tn = _pick(n, (512, 256, 128))
    nj = n // tn
    steps = (t // tm) * nj
    conv_steps = 1 << (steps.bit_length() - 1)
    rows = ne * ff // conv_steps
    wd2 = w_down.reshape(ne * ff, d)

    def wd_map(i, j):
        return (jnp.minimum(i * nj + j, conv_steps - 1), 0)

    proj, wd_b = pl.pallas_call(
        _in_kernel,
        out_shape=(jax.ShapeDtypeStruct((t, n), BF16), jax.ShapeDtypeStruct(wd2.shape, BF16)),
        grid=(t // tm, nj),
        in_specs=[pl.BlockSpec((tm, d), lambda i, j: (i, 0), pipeline_mode=pl.Buffered(1)),
                  pl.BlockSpec((d, tn), lambda i, j: (0, j)),
                  pl.BlockSpec((rows, d), wd_map)],
        out_specs=(pl.BlockSpec((tm, tn), lambda i, j: (i, j)),
                   pl.BlockSpec((rows, d), wd_map)),
        compiler_params=_params(("arbitrary", "arbitrary"), 56),
    )(h, w, wd2)
    return proj, wd_b.reshape(ne, ff, d)


def _head_sumsq(v, ones_bd):
    sq = v * v
    hi = sq.astype(BF16)
    lo = (sq - hi.astype(F32)).astype(BF16)
    return (jnp.dot(hi, ones_bd, preferred_element_type=F32)
            + jnp.dot(lo, ones_bd, preferred_element_type=F32))


def _head_rms_norm(v, w2, ones_bd):
    ss = _head_sumsq(v, ones_bd)
    return v * lax.rsqrt(ss * (1.0 / HEAD_DIM) + EPS) * w2


def _attn_kernel(sink_ref, q_ref, kvc_ref, kvp_ref, bias_ref, qw_ref, kw_ref, wg_ref, wu_ref,
                 o_ref, wg_o, wu_o):
    wg_o[...] = wg_ref[...].astype(BF16)
    wu_o[...] = wu_ref[...].astype(BF16)

    blk = ATTN_BLOCK
    kvw = N_KV_HEADS * HEAD_DIM
    lane = lax.broadcasted_iota(I32, (1, LANES), 1)
    low_half = lane < HEAD_DIM
    r = lax.broadcasted_iota(I32, (LANES, LANES), 0) // HEAD_DIM
    c = lax.broadcasted_iota(I32, (LANES, LANES), 1) // HEAD_DIM
    ones_bd = jnp.where(r == c, 1.0, 0.0).astype(BF16)

    kv = jnp.concatenate([kvp_ref[0], kvc_ref[0]], axis=0).astype(F32)
    kw2 = kw_ref[...]
    sum_a = jnp.broadcast_to(jnp.where(low_half, 1.0, 0.0), (2 * blk, LANES))
    sum_b = jnp.broadcast_to(jnp.where(low_half, 0.0, 1.0), (2 * blk, LANES))

    for g in range(N_KV_HEADS):
        chunk = g // 2
        kc = _head_rms_norm(kv[:, chunk * LANES:(chunk + 1) * LANES], kw2, ones_bd)
        vc = kv[:, kvw + chunk * LANES: kvw + (chunk + 1) * LANES]
        kr = pltpu.roll(kc, HEAD_DIM, axis=1)
        vr = pltpu.roll(vc, HEAD_DIM, axis=1)
        if g % 2 == 0:
            k_lo, k_hi, v_lo, v_hi = kc, kr, vc, vr
        else:
            k_lo, k_hi, v_lo, v_hi = kr, kc, vr, vc
        k_bd = jnp.concatenate([jnp.where(low_half, k_lo, 0.0),
                                jnp.where(low_half, 0.0, k_hi)], axis=0).astype(BF16)
        v_bd = jnp.concatenate(
            [jnp.concatenate([jnp.where(low_half, v_lo, 0.0), sum_a], axis=1),
             jnp.concatenate([jnp.where(low_half, 0.0, v_hi), sum_b], axis=1)],
            axis=0).astype(BF16)

        pairs = [g * (Q_PER_KV // 2) + jj for jj in range(Q_PER_KV // 2)]
        scores = []
        for j in pairs:
            qp = q_ref[0, :, j * LANES:(j + 1) * LANES].astype(F32)
            qn = _head_rms_norm(qp, qw_ref[...], ones_bd).astype(BF16)
            s = lax.dot_general(qn, k_bd, (((1,), (1,)), ((), ())),
                                preferred_element_type=F32)
            scores.append(s + bias_ref[0, j])
        probs = []
        for j, s in zip(pairs, scores):
            sink_a = sink_ref[2 * j]
            sink_b = sink_ref[2 * j + 1]
            m_a = jnp.maximum(jnp.max(s[:, :2 * blk], axis=-1, keepdims=True), sink_a)
            m_b = jnp.maximum(jnp.max(s[:, 2 * blk:], axis=-1, keepdims=True), sink_b)
            p = jnp.concatenate([jnp.exp2(s[:, :2 * blk] - m_a),
                                 jnp.exp2(s[:, 2 * blk:] - m_b)], axis=1).astype(BF16)
            esink = jnp.where(low_half, jnp.exp2(sink_a - m_a), jnp.exp2(sink_b - m_b))
            probs.append((p, esink))
        for j, (p, esink) in zip(pairs, probs):
            ol = jnp.dot(p, v_bd, preferred_element_type=F32)
            o = ol[:, :LANES] / (ol[:, LANES:] + esink)
            o_ref[0, :, j * LANES:(j + 1) * LANES] = o.astype(o_ref.dtype)


def _attn_bias():
    blk = ATTN_BLOCK
    i = np.arange(blk)[:, None]
    j = np.arange(2 * blk)[None, :]
    dist = i + blk - j
    band = (dist >= 0) & (dist < WINDOW)
    slopes = (2.0 ** (-8.0 * np.arange(1, N_Q_HEADS + 1, dtype=np.float32) / N_Q_HEADS)).astype(np.float32)
    alibi = -slopes[:, None, None] * dist.astype(np.float32)[None] * np.float32(LOG2E)
    later = np.where(band[None], alibi, np.float32(NEG_INF))
    first = np.where((band & (j >= blk))[None], alibi, np.float32(NEG_INF))
    tab = np.stack([first, later]).astype(np.float32)
    tab = tab.reshape(2, N_Q_HEADS // 2, 2, blk, 2 * blk)
    return np.transpose(tab, (0, 1, 3, 2, 4)).reshape(2, N_Q_HEADS // 2, blk, 4 * blk)


def _attention(proj3, sinks, q_norm_w, k_norm_w, q_col, kv_col, w_gate, w_up):
    b, s, _ = proj3.shape
    aw = N_Q_HEADS * HEAD_DIM
    kvw2 = 2 * N_KV_HEADS * HEAD_DIM
    nb = s // ATTN_BLOCK
    bias = _attn_bias()
    qw2 = (jnp.tile(q_norm_w, 2) * (HEAD_DIM ** -0.5 * LOG2E)).reshape(1, LANES)
    sinks = sinks * LOG2E
    kw2 = jnp.tile(k_norm_w, 2).reshape(1, LANES)
    ne, d, ff = w_gate.shape
    steps = b * nb
    wg2 = w_gate.reshape(ne * d, ff)
    wu2 = w_up.reshape(ne * d, ff)
    rows_in = ne * d // steps

    def w_map(bi, n, sk):
        return (bi * nb + n, 0)

    w_specs = [pl.BlockSpec((rows_in, ff), w_map), pl.BlockSpec((rows_in, ff), w_map)]
    grid_spec = pltpu.PrefetchScalarGridSpec(
        num_scalar_prefetch=1,
        grid=(b, nb),
        in_specs=[pl.BlockSpec((1, ATTN_BLOCK, aw), lambda bi, n, sk: (bi, n, q_col // aw)),
                  pl.BlockSpec((1, ATTN_BLOCK, kvw2), lambda bi, n, sk: (bi, n, kv_col // kvw2)),
                  pl.BlockSpec((1, ATTN_BLOCK, kvw2),
                               lambda bi, n, sk: (bi, jnp.maximum(n - 1, 0), kv_col // kvw2)),
                  pl.BlockSpec((1, N_Q_HEADS // 2, ATTN_BLOCK, 4 * ATTN_BLOCK),
                               lambda bi, n, sk: (jnp.minimum(n, 1), 0, 0, 0)),
                  pl.BlockSpec((1, LANES), lambda bi, n, sk: (0, 0)),
                  pl.BlockSpec((1, LANES), lambda bi, n, sk: (0, 0))] + w_specs,
        out_specs=tuple([pl.BlockSpec((1, ATTN_BLOCK, aw), lambda bi, n, sk: (bi, n, 0))] + w_specs),
    )
    y, wg_b, wu_b = pl.pallas_call(
        _attn_kernel,
        out_shape=(jax.ShapeDtypeStruct((b, s, aw), BF16),
                   jax.ShapeDtypeStruct(wg2.shape, BF16),
                   jax.ShapeDtypeStruct(wu2.shape, BF16)),
        grid_spec=grid_spec,
        compiler_params=_params(("arbitrary", "arbitrary"), 56),
    )(sinks, proj3, proj3, proj3, bias, qw2, kw2, wg2, wu2)
    return y, wg_b.reshape(ne, d, ff), wu_b.reshape(ne, d, ff)


def _pool_kernel(pc_ref, ph_ref, w_ref, ps_ref, o_ref):
    n = pl.program_id(1)
    ts = pc_ref.shape[1]
    cg = w_ref.shape[1]
    cur = pc_ref[0].astype(F32)
    halo = jnp.where(n > 0, ph_ref[0].astype(F32), 0.0)
    ext = jnp.concatenate([halo, cur], axis=0)
    t1 = (lax.broadcasted_iota(I32, (ts, 1), 0) + n * ts + 1).astype(F32)
    for g, win in enumerate(POOL_WINDOWS):
        acc = ext[:, g * cg:(g + 1) * cg]
        k = 1
        while k < win:
            acc = acc + pltpu.roll(acc, k, axis=0)
            k *= 2
        mean = acc[POOL_HALO:] / jnp.minimum(t1, float(win))
        d = (mean - cur[:, g * cg:(g + 1) * cg]).astype(BF16)
        y = jnp.dot(d, w_ref[g], preferred_element_type=F32) * ps_ref[:, g * cg:(g + 1) * cg]
        o_ref[0, :, g * cg:(g + 1) * cg] = y.astype(o_ref.dtype)


def _pool(proj3, w_pool, pool_scale, p_col):
    b, s, _ = proj3.shape
    ng, cg, _ = w_pool.shape
    pw = ng * cg
    ts = _pick(s, (512, 256, 128))
    return pl.pallas_call(
        _pool_kernel,
        out_shape=jax.ShapeDtypeStruct((b, s, pw), BF16),
        grid=(b, s // ts),
        in_specs=[pl.BlockSpec((pl.Element(1), pl.Element(ts), pl.Element(pw)),
                               lambda bi, n: (bi, pl.multiple_of(n * ts, ts), p_col)),
                  pl.BlockSpec((pl.Element(1), pl.Element(POOL_HALO), pl.Element(pw)),
                               lambda bi, n: (bi, pl.multiple_of(jnp.maximum(n * ts - POOL_HALO, 0),
                                                                 POOL_HALO), p_col)),
                  pl.BlockSpec((ng, cg, cg), lambda bi, n: (0, 0, 0)),
                  pl.BlockSpec((1, pw), lambda bi, n: (0, 0))],
        out_specs=pl.BlockSpec((1, ts, pw), lambda bi, n: (bi, n, 0)),
        compiler_params=_params(("parallel", "arbitrary"), 40),
    )(proj3, proj3, w_pool, pool_scale)


def _mix_kernel(a_ref, b_ref, wa_ref, wb_ref, ga_ref, gb_ref, o_ref):
    ya = jnp.dot(a_ref[...], wa_ref[...], preferred_element_type=F32)
    yb = jnp.dot(b_ref[...], wb_ref[...], preferred_element_type=F32)
    ga = jax.nn.sigmoid(ga_ref[...].astype(F32))
    gb = jax.nn.sigmoid(gb_ref[...].astype(F32))
    o_ref[...] = (ga * ya + gb * yb).astype(o_ref.dtype)


def _mix(y_attn, y_pool, wa, wb, proj, ga_col, gb_col):
    t, aw = y_attn.shape
    pw = y_pool.shape[1]
    d = wa.shape[1]
    tm = _pick(t, (512, 256, 128))
    tn = _pick(d, (1024, 512, 256, 128))
    return pl.pallas_call(
        _mix_kernel,
        out_shape=jax.ShapeDtypeStruct((t, d), BF16),
        grid=(t // tm, d // tn),
        in_specs=[pl.BlockSpec((tm, aw), lambda i, j: (i, 0)),
                  pl.BlockSpec((tm, pw), lambda i, j: (i, 0)),
                  pl.BlockSpec((aw, tn), lambda i, j: (0, j)),
                  pl.BlockSpec((pw, tn), lambda i, j: (0, j)),
                  pl.BlockSpec((pl.Element(tm), pl.Element(tn)),
                               lambda i, j: (pl.multiple_of(i * tm, tm),
                                             pl.multiple_of(ga_col + j * tn, LANES))),
                  pl.BlockSpec((pl.Element(tm), pl.Element(tn)),
                               lambda i, j: (pl.multiple_of(i * tm, tm),
                                             pl.multiple_of(gb_col + j * tn, LANES)))],
        out_specs=pl.BlockSpec((tm, tn), lambda i, j: (i, j)),
        compiler_params=_params(("parallel", "arbitrary"), 56),
    )(y_attn, y_pool, wa, wb, proj, proj)


def _out_kernel(m_ref, w_ref, x_ref, g_ref, o_ref):
    y = jnp.dot(m_ref[...], w_ref[...], preferred_element_type=F32)
    o_ref[...] = x_ref[...] + g_ref[0] * y


def _out_proj(mixed, w, x2, gate, seq):
    t, d = x2.shape
    tm = _pick(seq, (512, 256, 128))
    tn = _pick(d, (1024, 512, 256, 128))
    rows_per_batch = seq // tm
    return pl.pallas_call(
        _out_kernel,
        out_shape=jax.ShapeDtypeStruct((t, d), F32),
        grid=(t // tm, d // tn),
        in_specs=[pl.BlockSpec((tm, d), lambda i, j: (i, 0)),
                  pl.BlockSpec((d, tn), lambda i, j: (0, j)),
                  pl.BlockSpec((tm, tn), lambda i, j: (i, j)),
                  pl.BlockSpec((1, 1, tn), lambda i, j: (i // rows_per_batch, 0, j))],
        out_specs=pl.BlockSpec((tm, tn), lambda i, j: (i, j)),
        compiler_params=_params(("parallel", "arbitrary"), 56),
    )(mixed, w, x2, gate)


def _first_argmax(v, iota, n):
    m = jnp.max(v, axis=0, keepdims=True)
    idx = jnp.min(jnp.where(v == m, iota, n), axis=0, keepdims=True)
    return m, idx


def _router_kernel(x_ref, nw_ref, sc_ref, sh_ref, w2_ref, br_ref,
                   hp_ref, mi_ref, mf_ref, cnt_ref, carry_ref, hhi_ref, hlo_ref):
    step = pl.program_id(0)
    tm, d = x_ref.shape

    @pl.when(step == 0)
    def _():
        carry_ref[...] = jnp.zeros_like(carry_ref)

    gain = nw_ref[...] * (1.0 + sc_ref[0])
    shift = sh_ref[0]

    def body(r, carry):
        rows = pl.ds(pl.multiple_of(r * NORM_ROWS, NORM_ROWS), NORM_ROWS)
        h = _modulated_norm(x_ref[rows, :], gain, shift)
        hi = h.astype(BF16)
        hhi_ref[rows, :] = hi
        hlo_ref[rows, :] = (h - hi.astype(F32)).astype(BF16)
        hp_ref[rows, :] = _pack_halves(h)
        return carry

    lax.fori_loop(0, tm // NORM_ROWS, body, 0, unroll=NORM_UNROLL)

    l2 = (jnp.dot(hhi_ref[...], w2_ref[...], preferred_element_type=F32)
          + jnp.dot(hlo_ref[...], w2_ref[...], preferred_element_type=F32))
    lt = l2[:, :LANES] + l2[:, LANES:]
    logits = lt.T[:N_ROUTER] + br_ref[...]
    iota8 = lax.broadcasted_iota(I32, (N_GROUPS, tm), 0)
    gl = logits[:N_GROUPS]
    g_max, g_sel = _first_argmax(gl, iota8, N_GROUPS)
    p_group = 1.0 / jnp.sum(jnp.exp(gl - g_max), axis=0, keepdims=True)
    e_in = jnp.zeros((EXPERTS_PER_GROUP, tm), F32)
    for g in range(N_GROUPS):
        lo = N_GROUPS + g * EXPERTS_PER_GROUP
        e_in = e_in + jnp.where(g_sel == g, logits[lo:lo + EXPERTS_PER_GROUP], 0.0)
    v0, i0 = _first_argmax(e_in, iota8, EXPERTS_PER_GROUP)
    rest = jnp.where(iota8 == i0, -jnp.inf, e_in)
    v1, i1 = _first_argmax(rest, iota8, EXPERTS_PER_GROUP)
    t = jnp.exp(v1 - v0)
    w0 = p_group / (1.0 + t)
    w1 = p_group * t / (1.0 + t)
    e0 = g_sel * EXPERTS_PER_GROUP + i0
    e1 = g_sel * EXPERTS_PER_GROUP + i1

    iota_e = lax.broadcasted_iota(I32, (N_EXPERTS, tm), 0)
    hit0 = iota_e == e0
    hit1 = iota_e == e1
    onehot = jnp.where(hit0 | hit1, 1.0, 0.0).astype(BF16)
    rr = lax.broadcasted_iota(I32, (tm, tm), 0)
    cc = lax.broadcasted_iota(I32, (tm, tm), 1)
    before = jnp.where(rr < cc, 1.0, 0.0).astype(BF16)
    prior = jnp.dot(onehot, before, preferred_element_type=F32) + carry_ref[:, :1]
    r0 = jnp.sum(jnp.where(hit0, prior, 0.0), axis=0, keepdims=True)
    r1 = jnp.sum(jnp.where(hit1, prior, 0.0), axis=0, keepdims=True)
    carry_ref[...] += jnp.dot(onehot, jnp.ones((tm, LANES), BF16), preferred_element_type=F32)

    zi = jnp.zeros((1, tm), I32)
    zf = jnp.zeros((1, tm), F32)
    mi_ref[...] = jnp.concatenate([e0, e1, r0.astype(I32), r1.astype(I32), zi, zi, zi, zi], axis=0)
    mf_ref[...] = jnp.concatenate([w0, w1, zf, zf, zf, zf, zf, zf], axis=0)
    cnt_ref[...] = carry_ref[...]


def _router(x1, nw, scale, shift, w2, br, seq):
    t, d = x1.shape
    tm = _pick(seq, (512, 256, 128))
    rows_per_batch = seq // tm
    return pl.pallas_call(
        _router_kernel,
        out_shape=(jax.ShapeDtypeStruct((t, d // 2), U32),
                   jax.ShapeDtypeStruct((8, t), I32),
                   jax.ShapeDtypeStruct((8, t), F32),
                   jax.ShapeDtypeStruct((N_EXPERTS, LANES), F32)),
        grid=(t // tm,),
        in_specs=[pl.BlockSpec((tm, d), lambda i: (i, 0)),
                  pl.BlockSpec((1, d), lambda i: (0, 0)),
                  pl.BlockSpec((1, 1, d), lambda i: (i // rows_per_batch, 0, 0)),
                  pl.BlockSpec((1, 1, d), lambda i: (i // rows_per_batch, 0, 0)),
                  pl.BlockSpec((d, 2 * LANES), lambda i: (0, 0)),
                  pl.BlockSpec((N_ROUTER, 1), lambda i: (0, 0))],
        out_specs=(pl.BlockSpec((tm, d // 2), lambda i: (i, 0)),
                   pl.BlockSpec((8, tm), lambda i: (0, i)),
                   pl.BlockSpec((8, tm), lambda i: (0, i)),
                   pl.BlockSpec((N_EXPERTS, LANES), lambda i: (0, 0))),
        scratch_shapes=[pltpu.VMEM((N_EXPERTS, LANES), F32), pltpu.VMEM((tm, d), BF16),
                        pltpu.VMEM((tm, d), BF16)],
        compiler_params=_params(("arbitrary",), 56),
    )(x1, nw, scale, shift, w2, br)


PLAN_LANES = 2048
DMA_UNROLL = 4


def _plan_kernel(mi_ref, cnt_ref, dest_ref, tab_ref):
    cnt = cnt_ref[...]
    nb = jnp.floor((cnt + (EXPERT_BLOCK - 1)) * (1.0 / EXPERT_BLOCK))
    r = lax.broadcasted_iota(I32, (N_EXPERTS, N_EXPERTS), 0)
    c = lax.broadcasted_iota(I32, (N_EXPERTS, N_EXPERTS), 1)
    incl = jnp.where(c <= r, 1.0, 0.0).astype(BF16)
    bend = jnp.dot(incl, nb.astype(BF16), preferred_element_type=F32)
    bstart = bend - nb
    first_row = bstart[:, :1] * EXPERT_BLOCK

    tl = mi_ref.shape[1]
    iota_e = lax.broadcasted_iota(I32, (N_EXPERTS, tl), 0)
    for k in range(TOP_K):
        hit = iota_e == mi_ref[k:k + 1, :]
        base = jnp.sum(jnp.where(hit, first_row, 0.0), axis=0, keepdims=True)
        dest_ref[k:k + 1, :] = base.astype(I32) + mi_ref[TOP_K + k:TOP_K + k + 1, :]

    nl = tab_ref.shape[1]
    bidx = lax.broadcasted_iota(I32, (1, nl), 1).astype(F32)
    blk_e = jnp.sum(jnp.where(bend[:, :1] <= bidx, 1.0, 0.0), axis=0, keepdims=True)
    blk_e = jnp.minimum(blk_e, N_EXPERTS - 1.0)
    hit = lax.broadcasted_iota(I32, (N_EXPERTS, nl), 0) == blk_e.astype(I32)
    cnt_b = jnp.sum(jnp.where(hit, cnt[:, :1], 0.0), axis=0, keepdims=True)
    start_b = jnp.sum(jnp.where(hit, bstart[:, :1], 0.0), axis=0, keepdims=True)
    valid = jnp.clip(cnt_b - (bidx - start_b) * EXPERT_BLOCK, 0.0, float(EXPERT_BLOCK))
    n_used = jnp.broadcast_to(jnp.max(bend[:, :1], axis=0, keepdims=True), (1, nl))
    run_end = jnp.sum(jnp.where(hit, bend[:, :1], 0.0), axis=0, keepdims=True)
    zero = jnp.zeros((1, nl), F32)
    tab_ref[...] = jnp.concatenate([blk_e, valid, n_used, run_end, zero, zero, zero, zero],
                                   axis=0).astype(I32)


def _plan(meta_i, cnt, nblk):
    t = meta_i.shape[1]
    tl = _pick(t, (PLAN_LANES, 1024, 512, 256, 128))
    nl = pl.cdiv(nblk, LANES) * LANES
    return pl.pallas_call(
        _plan_kernel,
        out_shape=(jax.ShapeDtypeStruct((TOP_K, t), I32), jax.ShapeDtypeStruct((8, nl), I32)),
        grid=(t // tl,),
        in_specs=[pl.BlockSpec((8, tl), lambda i: (0, i)),
                  pl.BlockSpec((N_EXPERTS, LANES), lambda i: (0, 0))],
        out_specs=(pl.BlockSpec((TOP_K, tl), lambda i: (0, i)),
                   pl.BlockSpec((8, nl), lambda i: (0, 0))),
        compiler_params=_params(("arbitrary",), 32),
    )(meta_i, cnt)


def _dispatch_kernel(dest_ref, hp_ref, xs_ref, sem, *, chunk, t):
    base = pl.program_id(0) * chunk

    def issue(i, carry):
        for k in range(TOP_K):
            pltpu.make_async_copy(hp_ref.at[pl.ds(i, 1)],
                                  xs_ref.at[pl.ds(dest_ref[k * t + base + i], 1)],
                                  sem).start(priority=k)
        return carry

    lax.fori_loop(0, chunk, issue, 0, unroll=DMA_UNROLL)
    for k in range(TOP_K):
        pltpu.make_async_copy(hp_ref, xs_ref.at[pl.ds(0, chunk)], sem).wait()


def _dispatch(dest_flat, hp, cap):
    t, half = hp.shape
    chunk = _pick(t, (512, 256, 128))
    grid_spec = pltpu.PrefetchScalarGridSpec(
        num_scalar_prefetch=1,
        grid=(t // chunk,),
        in_specs=[pl.BlockSpec((chunk, half), lambda i, ds: (i, 0))],
        out_specs=pl.BlockSpec(memory_space=pl.ANY),
        scratch_shapes=[pltpu.SemaphoreType.DMA(())],
    )
    return pl.pallas_call(
        functools.partial(_dispatch_kernel, chunk=chunk, t=t),
        out_shape=jax.ShapeDtypeStruct((cap, half), U32),
        grid_spec=grid_spec,
        compiler_params=pltpu.CompilerParams(dimension_semantics=("arbitrary",),
                                             has_side_effects=True, vmem_limit_bytes=32 * MIB),
    )(dest_flat, hp)


def _ffn_kernel(be_ref, nv_ref, nu_ref, re_ref, xs_ref, wg_hbm, wu_hbm, wd_hbm, o_ref,
                wg_s, wu_s, wd_s, slot_ref, sem):
    b = pl.program_id(0)

    def weight_copies(e, slot):
        return (pltpu.make_async_copy(wg_hbm.at[e], wg_s.at[slot], sem.at[slot]),
                pltpu.make_async_copy(wu_hbm.at[e], wu_s.at[slot], sem.at[slot]),
                pltpu.make_async_copy(wd_hbm.at[e], wd_s.at[slot], sem.at[slot]))

    @pl.when(b < nu_ref[0])
    def _():
        e = be_ref[b]

        @pl.when(b == 0)
        def _():
            slot_ref[0] = 0
            for c in weight_copies(e, 0):
                c.start()

        first = jnp.logical_or(b == 0, be_ref[jnp.maximum(b - 1, 0)] != e)

        @pl.when(first)
        def _():
            slot = jnp.where(b == 0, 0, 1 - slot_ref[0])
            slot_ref[0] = slot
            for c in weight_copies(e, slot):
                c.wait()
            nxt = re_ref[b]

            @pl.when(nxt < nu_ref[0])
            def _():
                for c in weight_copies(be_ref[nxt], 1 - slot):
                    c.start()

        slot = slot_ref[0]
        blk, half = xs_ref.shape
        row = lax.broadcasted_iota(I32, (blk, 1), 0)
        xp = jnp.where(row < nv_ref[b], xs_ref[...], jnp.uint32(0))
        lo, hi = _unpack_halves(xp)
        lo = lo.astype(BF16)
        hi = hi.astype(BF16)
        a = (jnp.dot(lo, wg_s[slot, :half], preferred_element_type=F32)
             + jnp.dot(hi, wg_s[slot, half:], preferred_element_type=F32))
        u = (jnp.dot(lo, wu_s[slot, :half], preferred_element_type=F32)
             + jnp.dot(hi, wu_s[slot, half:], preferred_element_type=F32))
        mid = (a * jax.nn.sigmoid(a) * u).astype(BF16)
        o_ref[...] = _pack_halves(jnp.dot(mid, wd_s[slot], preferred_element_type=F32))


def _experts(blk_e, blk_valid, n_used, run_end, xs, wg, wu, wd):
    cap, half = xs.shape
    d = 2 * half
    ff = wg.shape[2]
    nblk = cap // EXPERT_BLOCK

    def row_map(b, be, nv, nu, re):
        return (jnp.minimum(b, nu[0] - 1), 0)

    grid_spec = pltpu.PrefetchScalarGridSpec(
        num_scalar_prefetch=4,
        grid=(nblk,),
        in_specs=[pl.BlockSpec((EXPERT_BLOCK, half), row_map),
                  pl.BlockSpec(memory_space=pl.ANY),
                  pl.BlockSpec(memory_space=pl.ANY),
                  pl.BlockSpec(memory_space=pl.ANY)],
        out_specs=pl.BlockSpec((EXPERT_BLOCK, half), row_map),
        scratch_shapes=[pltpu.VMEM((2, d, ff), BF16), pltpu.VMEM((2, d, ff), BF16),
                        pltpu.VMEM((2, ff, d), BF16), pltpu.SMEM((1,), I32),
                        pltpu.SemaphoreType.DMA((2,))],
    )
    return pl.pallas_call(
        _ffn_kernel,
        out_shape=jax.ShapeDtypeStruct((cap, half), U32),
        grid_spec=grid_spec,
        compiler_params=_params(("arbitrary",), 56),
    )(blk_e, blk_valid, n_used, run_end, xs, wg, wu, wd)


def _combine_kernel(dest_ref, x_ref, g_ref, w_ref, ys_ref, o_ref, ybuf, sem, *, chunk, t):
    i = pl.program_id(0)
    n = pl.num_programs(0)

    def issue(step, slot):
        base = step * chunk

        def body(r, carry):
            for k in range(TOP_K):
                pltpu.make_async_copy(ys_ref.at[pl.ds(dest_ref[k * t + base + r], 1)],
                                      ybuf.at[slot, k, pl.ds(r, 1)],
                                      sem.at[slot]).start(priority=k)
            return carry

        lax.fori_loop(0, chunk, body, 0, unroll=DMA_UNROLL)

    @pl.when(i == 0)
    def _():
        issue(0, 0)

    slot = i % 2

    @pl.when(i + 1 < n)
    def _():
        issue(i + 1, 1 - slot)

    for k in range(TOP_K):
        pltpu.make_async_copy(ys_ref.at[pl.ds(0, chunk)], ybuf.at[slot, k], sem.at[slot]).wait()
    w = w_ref[...]
    half = ybuf.shape[-1]
    lo0, hi0 = _unpack_halves(ybuf[slot, 0])
    lo1, hi1 = _unpack_halves(ybuf[slot, 1])
    g = g_ref[0]
    o_ref[:, :half] = x_ref[:, :half] + g[:, :half] * (w[:, 0:1] * lo0 + w[:, 1:2] * lo1)
    o_ref[:, half:] = x_ref[:, half:] + g[:, half:] * (w[:, 0:1] * hi0 + w[:, 1:2] * hi1)


def _combine(dest_flat, x1, gate, w_rows, ys, seq):
    t, d = x1.shape
    chunk = _pick(seq, (256, 128))
    rows_per_batch = seq // chunk
    grid_spec = pltpu.PrefetchScalarGridSpec(
        num_scalar_prefetch=1,
        grid=(t // chunk,),
        in_specs=[pl.BlockSpec((chunk, d), lambda i, ds: (i, 0)),
                  pl.BlockSpec((1, 1, d), lambda i, ds: (i // rows_per_batch, 0, 0)),
                  pl.BlockSpec((chunk, 8), lambda i, ds: (i, 0)),
                  pl.BlockSpec(memory_space=pl.ANY)],
        out_specs=pl.BlockSpec((chunk, d), lambda i, ds: (i, 0)),
        scratch_shapes=[pltpu.VMEM((2, TOP_K, chunk, d // 2), U32),
                        pltpu.SemaphoreType.DMA((2,))],
    )
    return pl.pallas_call(
        functools.partial(_combine_kernel, chunk=chunk, t=t),
        out_shape=jax.ShapeDtypeStruct((t, d), F32),
        grid_spec=grid_spec,
        compiler_params=_params(("arbitrary",), 56),
    )(dest_flat, x1, gate, w_rows, ys)


def _layer(x, c, w_ada, b_ada, norm1_w, w_in, q_norm_w, k_norm_w, sinks, w_pool, pool_scale,
           w_attn_up, w_pool_up, w_out, norm2_w, w_router_group, b_router_group,
           w_router_expert, b_router_expert, w_gate, w_up, w_down):
    b, s, d = x.shape
    t = b * s
    aw = w_attn_up.shape[0]
    pw = w_pool_up.shape[0]
    kvw = N_KV_HEADS * HEAD_DIM
    x2 = x.reshape(t, d)

    c8 = jnp.zeros((8, d), F32).at[:b].set(c)
    ada = _ada(c8, w_ada, b_ada.reshape(1, 6 * d))[:b]
    shift1, scale1, gate1, shift2, scale2, gate2 = [a.reshape(b, 1, d) for a in jnp.split(ada, 6, axis=-1)]

    q_col, kv_col = 0, aw
    p_col = aw + 2 * kvw
    ga_col = p_col + pw
    gb_col = ga_col + d
    h = _norm(x2, norm1_w.reshape(1, d), scale1, shift1, s)
    proj, wd_b = _in_proj(h, w_in.astype(BF16), w_down)
    proj3 = proj.reshape(b, s, proj.shape[1])

    y_attn, wg_b, wu_b = _attention(proj3, sinks, q_norm_w, k_norm_w, q_col, kv_col, w_gate, w_up)
    y_attn = y_attn.reshape(t, aw)
    y_pool = _pool(proj3, w_pool.astype(BF16), pool_scale.reshape(1, pw), p_col).reshape(t, pw)
    mixed = _mix(y_attn, y_pool, w_attn_up.astype(BF16), w_pool_up.astype(BF16), proj, ga_col, gb_col)
    x1 = _out_proj(mixed, w_out.astype(BF16), x2, gate1, s)

    wr = jnp.concatenate([w_router_group, w_router_expert,
                          jnp.zeros((d, LANES - N_ROUTER), F32)], axis=1)
    wr_hi = wr.astype(BF16)
    wr_lo = (wr - wr_hi.astype(F32)).astype(BF16)
    wr2 = jnp.concatenate([wr_hi, wr_lo], axis=1)
    br = jnp.concatenate([b_router_group, b_router_expert]).reshape(N_ROUTER, 1)
    hp, meta_i, meta_f, cnt = _router(x1, norm2_w.reshape(1, d), scale2, shift2, wr2, br, s)

    cap = TOP_K * t + N_EXPERTS * EXPERT_BLOCK
    nblk = cap // EXPERT_BLOCK
    dest, tab = _plan(meta_i, cnt, nblk)
    dest_flat = dest.reshape(TOP_K * t)
    blk_e, blk_valid, n_used, run_end = tab[0, :nblk], tab[1, :nblk], tab[2, :1], tab[3, :nblk]

    xs = _dispatch(dest_flat, hp, cap)
    ys = _experts(blk_e, blk_valid, n_used, run_end, xs, wg_b, wu_b, wd_b)
    out = _combine(dest_flat, x1, gate2, meta_f.T, ys, s)
    return out.reshape(b, s, d)


def kernel(x, c, w_ada, b_ada, norm1_w, w_in, q_norm_w, k_norm_w, sinks, w_pool, pool_scale,
           w_attn_up, w_pool_up, w_out, norm2_w, w_router_group, b_router_group,
           w_router_expert, b_router_expert, w_gate, w_up, w_down):
    for l in range(w_ada.shape[0]):
        x = _layer(x, c, w_ada[l], b_ada[l], norm1_w[l], w_in[l], q_norm_w[l], k_norm_w[l],
                   sinks[l], w_pool[l], pool_scale[l], w_attn_up[l], w_pool_up[l], w_out[l],
                   norm2_w[l], w_router_group[l], b_router_group[l], w_router_expert[l],
                   b_router_expert[l], w_gate[l], w_up[l], w_down[l])
    return x
```

```python
import functools

import jax
import jax.numpy as jnp
import numpy as np
from jax import lax
from jax.experimental import pallas as pl
from jax.experimental.pallas import tpu as pltpu

F32 = jnp.float32
BF16 = jnp.bfloat16
U32 = jnp.uint32
I32 = jnp.int32

HEAD_DIM = 64
N_Q_HEADS = 32
N_KV_HEADS = 4
Q_PER_KV = N_Q_HEADS // N_KV_HEADS
ATTN_BLOCK = 128
WINDOW = 128
POOL_WINDOWS = (2, 4, 8, 16)
POOL_HALO = 16
N_GROUPS = 8
EXPERTS_PER_GROUP = 8
N_EXPERTS = N_GROUPS * EXPERTS_PER_GROUP
N_ROUTER = N_GROUPS + N_EXPERTS
TOP_K = 2
EPS = 1e-6
NEG_INF = -1e30
LOG2E = 1.4426950408889634
ATTN_STAGE_PAIRS = 8

LANES = 128
EXPERT_BLOCK = 256
MIB = 1024 * 1024


def _params(semantics, vmem_mib):
    return pltpu.CompilerParams(dimension_semantics=semantics, vmem_limit_bytes=vmem_mib * MIB)


def _pick(n, candidates):
    for c in candidates:
        if n % c == 0:
            return c
    return n


def _ada_kernel(c_ref, w_ref, b_ref, o_ref):
    c = c_ref[...]
    s = (c * jax.nn.sigmoid(c)).astype(BF16)
    o_ref[...] = jnp.dot(s, w_ref[...].astype(BF16), preferred_element_type=F32) + b_ref[...]


def _ada(c8, w, b):
    d, n = w.shape
    tn = _pick(n, (1024, 512, 256, 128))
    return pl.pallas_call(
        _ada_kernel,
        out_shape=jax.ShapeDtypeStruct((8, n), F32),
        grid=(n // tn,),
        in_specs=[pl.BlockSpec((8, d), lambda j: (0, 0)),
                  pl.BlockSpec((d, tn), lambda j: (0, j)),
                  pl.BlockSpec((1, tn), lambda j: (0, j))],
        out_specs=pl.BlockSpec((8, tn), lambda j: (0, j)),
        compiler_params=_params(("arbitrary",), 56),
    )(c8, w, b)


def _modulated_norm(x, gain, shift):
    ms = jnp.mean(x * x, axis=-1, keepdims=True)
    return x * lax.rsqrt(ms + EPS) * gain + shift


NORM_ROWS = 16
NORM_UNROLL = 4


def _pack_halves(v):
    n = v.shape[1] // 2
    bits = lax.bitcast_convert_type(v.astype(BF16).astype(F32), U32)
    return (bits[:, :n] >> 16) | (bits[:, n:] & jnp.uint32(0xFFFF0000))


def _unpack_halves(p):
    lo = lax.bitcast_convert_type(p << 16, F32)
    hi = lax.bitcast_convert_type(p & jnp.uint32(0xFFFF0000), F32)
    return lo, hi


def _norm_kernel(x_ref, nw_ref, sc_ref, sh_ref, o_ref):
    gain = nw_ref[...] * (1.0 + sc_ref[0])
    shift = sh_ref[0]

    def body(r, carry):
        rows = pl.ds(pl.multiple_of(r * NORM_ROWS, NORM_ROWS), NORM_ROWS)
        o_ref[rows, :] = _modulated_norm(x_ref[rows, :], gain, shift).astype(o_ref.dtype)
        return carry

    lax.fori_loop(0, x_ref.shape[0] // NORM_ROWS, body, 0, unroll=NORM_UNROLL)


def _norm(x2, nw, scale, shift, seq):
    t, d = x2.shape
    tm = _pick(seq, (512, 256, 128))
    rows_per_batch = seq // tm
    return pl.pallas_call(
        _norm_kernel,
        out_shape=jax.ShapeDtypeStruct((t, d), BF16),
        grid=(t // tm,),
        in_specs=[pl.BlockSpec((tm, d), lambda i: (i, 0)),
                  pl.BlockSpec((1, d), lambda i: (0, 0)),
                  pl.BlockSpec((1, 1, d), lambda i: (i // rows_per_batch, 0, 0)),
                  pl.BlockSpec((1, 1, d), lambda i: (i // rows_per_batch, 0, 0))],
        out_specs=pl.BlockSpec((tm, d), lambda i: (i, 0)),
        compiler_params=_params(("parallel",), 40),
    )(x2, nw, scale, shift)


def _in_kernel(a_ref, w_ref, *refs):
    n_riders = (len(refs) - 1) // 2
    o_ref = refs[n_riders]
    o_ref[...] = jnp.dot(a_ref[...], w_ref[...], preferred_element_type=F32).astype(o_ref.dtype)
    for src, dst in zip(refs[:n_riders], refs[n_riders + 1:]):
        dst[...] = src[...].astype(BF16)


def _rider_steps(nrows, steps):
    n = 1 << (steps.bit_length() - 1)
    while nrows % n or (nrows // n) % 16:
        n //= 2
    return n


def _in_proj(h, w, riders):
    t, d = h.shape
    n = w.shape[1]
    tm = _pick(t, (1024, 512, 256, 128))
    tn = _pick(n, (512, 256, 128))
    nj = n // tn
    steps = (t // tm) * nj

    def rider_spec(arr):
        nsteps = _rider_steps(arr.shape[0], steps)
        return pl.BlockSpec((arr.shape[0] // nsteps, arr.shape[1]),
                            lambda i, j: (jnp.minimum(i * nj + j, nsteps - 1), 0))

    rider_specs = [rider_spec(r) for r in riders]
    outs = pl.pallas_call(
        _in_kernel,
        out_shape=tuple([jax.ShapeDtypeStruct((t, n), BF16)]
                        + [jax.ShapeDtypeStruct(r.shape, BF16) for r in riders]),
        grid=(t // tm, nj),
        in_specs=[pl.BlockSpec((tm, d), lambda i, j: (i, 0)),
                  pl.BlockSpec((d, tn), lambda i, j: (0, j))] + rider_specs,
        out_specs=tuple([pl.BlockSpec((tm, tn), lambda i, j: (i, j))] + rider_specs),
        compiler_params=_params(("arbitrary", "arbitrary"), 56),
    )(h, w, *riders)
    return outs[0], outs[1:]


def _head_sumsq(v, ones_bd):
    sq = v * v
    hi = sq.astype(BF16)
    lo = (sq - hi.astype(F32)).astype(BF16)
    return (jnp.dot(hi, ones_bd, preferred_element_type=F32)
            + jnp.dot(lo, ones_bd, preferred_element_type=F32))


def _head_rms_norm(v, w2, ones_bd):
    ss = _head_sumsq(v, ones_bd)
    return v * lax.rsqrt(ss * (1.0 / HEAD_DIM) + EPS) * w2


def _attn_kernel(sink_ref, q_ref, kvc_ref, kvp_ref, bias_ref, qw_ref, kw_ref, wg_ref, o_ref, wg_o):
    wg_o[...] = wg_ref[...].astype(BF16)

    blk = ATTN_BLOCK
    kvw = N_KV_HEADS * HEAD_DIM
    lane = lax.broadcasted_iota(I32, (1, LANES), 1)
    low_half = lane < HEAD_DIM
    r = lax.broadcasted_iota(I32, (LANES, LANES), 0) // HEAD_DIM
    c = lax.broadcasted_iota(I32, (LANES, LANES), 1) // HEAD_DIM
    ones_bd = jnp.where(r == c, 1.0, 0.0).astype(BF16)

    kv = jnp.concatenate([kvp_ref[0], kvc_ref[0]], axis=0).astype(F32)
    kw2 = kw_ref[...]
    sum_a = jnp.broadcast_to(jnp.where(low_half, 1.0, 0.0), (2 * blk, LANES))
    sum_b = jnp.broadcast_to(jnp.where(low_half, 0.0, 1.0), (2 * blk, LANES))

    k_bds, v_bds = [], []
    for g in range(N_KV_HEADS):
        chunk = g // 2
        kc = _head_rms_norm(kv[:, chunk * LANES:(chunk + 1) * LANES], kw2, ones_bd)
        vc = kv[:, kvw + chunk * LANES: kvw + (chunk + 1) * LANES]
        kr = pltpu.roll(kc, HEAD_DIM, axis=1)
        vr = pltpu.roll(vc, HEAD_DIM, axis=1)
        if g % 2 == 0:
            k_lo, k_hi, v_lo, v_hi = kc, kr, vc, vr
        else:
            k_lo, k_hi, v_lo, v_hi = kr, kc, vr, vc
        k_bds.append(jnp.concatenate([jnp.where(low_half, k_lo, 0.0),
                                      jnp.where(low_half, 0.0, k_hi)], axis=0).astype(BF16))
        v_bds.append(jnp.concatenate(
            [jnp.concatenate([jnp.where(low_half, v_lo, 0.0), sum_a], axis=1),
             jnp.concatenate([jnp.where(low_half, 0.0, v_hi), sum_b], axis=1)],
            axis=0).astype(BF16))

    pairs_per_kv = Q_PER_KV // 2
    for first in range(0, N_Q_HEADS // 2, ATTN_STAGE_PAIRS):
        pairs = range(first, first + ATTN_STAGE_PAIRS)
        scores = []
        for j in pairs:
            qp = q_ref[0, :, j * LANES:(j + 1) * LANES].astype(F32)
            qn = _head_rms_norm(qp, qw_ref[...], ones_bd).astype(BF16)
            s = lax.dot_general(qn, k_bds[j // pairs_per_kv], (((1,), (1,)), ((), ())),
                                preferred_element_type=F32)
            scores.append(s + bias_ref[0, j])
        probs = []
        for j, s in zip(pairs, scores):
            sink_a = sink_ref[2 * j]
            sink_b = sink_ref[2 * j + 1]
            m_a = jnp.maximum(jnp.max(s[:, :2 * blk], axis=-1, keepdims=True), sink_a)
            m_b = jnp.maximum(jnp.max(s[:, 2 * blk:], axis=-1, keepdims=True), sink_b)
            p = jnp.concatenate([jnp.exp2(s[:, :2 * blk] - m_a),
                                 jnp.exp2(s[:, 2 * blk:] - m_b)], axis=1).astype(BF16)
            esink = jnp.where(low_half, jnp.exp2(sink_a - m_a), jnp.exp2(sink_b - m_b))
            probs.append((p, esink))
        for j, (p, esink) in zip(pairs, probs):
            ol = jnp.dot(p, v_bds[j // pairs_per_kv], preferred_element_type=F32)
            o = ol[:, :LANES] / (ol[:, LANES:] + esink)
            o_ref[0, :, j * LANES:(j + 1) * LANES] = o.astype(o_ref.dtype)


def _attn_bias():
    blk = ATTN_BLOCK
    i = np.arange(blk)[:, None]
    j = np.arange(2 * blk)[None, :]
    dist = i + blk - j
    band = (dist >= 0) & (dist < WINDOW)
    slopes = (2.0 ** (-8.0 * np.arange(1, N_Q_HEADS + 1, dtype=np.float32) / N_Q_HEADS)).astype(np.float32)
    alibi = -slopes[:, None, None] * dist.astype(np.float32)[None] * np.float32(LOG2E)
    later = np.where(band[None], alibi, np.float32(NEG_INF))
    first = np.where((band & (j >= blk))[None], alibi, np.float32(NEG_INF))
    tab = np.stack([first, later]).astype(np.float32)
    tab = tab.reshape(2, N_Q_HEADS // 2, 2, blk, 2 * blk)
    return np.transpose(tab, (0, 1, 3, 2, 4)).reshape(2, N_Q_HEADS // 2, blk, 4 * blk)


def _attention(proj3, sinks, q_norm_w, k_norm_w, q_col, kv_col, w_gate):
    b, s, _ = proj3.shape
    aw = N_Q_HEADS * HEAD_DIM
    kvw2 = 2 * N_KV_HEADS * HEAD_DIM
    nb = s // ATTN_BLOCK
    bias = _attn_bias()
    qw2 = (jnp.tile(q_norm_w, 2) * (HEAD_DIM ** -0.5 * LOG2E)).reshape(1, LANES)
    sinks = sinks * LOG2E
    kw2 = jnp.tile(k_norm_w, 2).reshape(1, LANES)
    ne, d, ff = w_gate.shape
    steps = b * nb
    wg2 = w_gate.reshape(ne * d, ff)
    rows_in = ne * d // steps

    def w_map(bi, n, sk):
        return (bi * nb + n, 0)

    w_specs = [pl.BlockSpec((rows_in, ff), w_map)]
    grid_spec = pltpu.PrefetchScalarGridSpec(
        num_scalar_prefetch=1,
        grid=(b, nb),
        in_specs=[pl.BlockSpec((1, ATTN_BLOCK, aw), lambda bi, n, sk: (bi, n, q_col // aw)),
                  pl.BlockSpec((1, ATTN_BLOCK, kvw2), lambda bi, n, sk: (bi, n, kv_col // kvw2)),
                  pl.BlockSpec((1, ATTN_BLOCK, kvw2),
                               lambda bi, n, sk: (bi, jnp.maximum(n - 1, 0), kv_col // kvw2)),
                  pl.BlockSpec((1, N_Q_HEADS // 2, ATTN_BLOCK, 4 * ATTN_BLOCK),
                               lambda bi, n, sk: (jnp.minimum(n, 1), 0, 0, 0)),
                  pl.BlockSpec((1, LANES), lambda bi, n, sk: (0, 0)),
                  pl.BlockSpec((1, LANES), lambda bi, n, sk: (0, 0))] + w_specs,
        out_specs=tuple([pl.BlockSpec((1, ATTN_BLOCK, aw), lambda bi, n, sk: (bi, n, 0))] + w_specs),
    )
    y, wg_b = pl.pallas_call(
        _attn_kernel,
        out_shape=(jax.ShapeDtypeStruct((b, s, aw), BF16),
                   jax.ShapeDtypeStruct(wg2.shape, BF16)),
        grid_spec=grid_spec,
        compiler_params=_params(("arbitrary", "arbitrary"), 56),
    )(sinks, proj3, proj3, proj3, bias, qw2, kw2, wg2)
    return y, wg_b.reshape(ne, d, ff)


def _pool_kernel(pc_ref, ph_ref, w_ref, ps_ref, o_ref):
    n = pl.program_id(1)
    ts = pc_ref.shape[1]
    cg = w_ref.shape[1]
    cur = pc_ref[0].astype(F32)
    halo = jnp.where(n > 0, ph_ref[0].astype(F32), 0.0)
    ext = jnp.concatenate([halo, cur], axis=0)
    t1 = (lax.broadcasted_iota(I32, (ts, 1), 0) + n * ts + 1).astype(F32)
    for g, win in enumerate(POOL_WINDOWS):
        acc = ext[:, g * cg:(g + 1) * cg]
        k = 1
        while k < win:
            acc = acc + pltpu.roll(acc, k, axis=0)
            k *= 2
        mean = acc[POOL_HALO:] / jnp.minimum(t1, float(win))
        d = (mean - cur[:, g * cg:(g + 1) * cg]).astype(BF16)
        y = jnp.dot(d, w_ref[g], preferred_element_type=F32) * ps_ref[:, g * cg:(g + 1) * cg]
        o_ref[0, :, g * cg:(g + 1) * cg] = y.astype(o_ref.dtype)


def _pool(proj3, w_pool, pool_scale, p_col):
    b, s, _ = proj3.shape
    ng, cg, _ = w_pool.shape
    pw = ng * cg
    ts = _pick(s, (512, 256, 128))
    return pl.pallas_call(
        _pool_kernel,
        out_shape=jax.ShapeDtypeStruct((b, s, pw), BF16),
        grid=(b, s // ts),
        in_specs=[pl.BlockSpec((pl.Element(1), pl.Element(ts), pl.Element(pw)),
                               lambda bi, n: (bi, pl.multiple_of(n * ts, ts), p_col)),
                  pl.BlockSpec((pl.Element(1), pl.Element(POOL_HALO), pl.Element(pw)),
                               lambda bi, n: (bi, pl.multiple_of(jnp.maximum(n * ts - POOL_HALO, 0),
                                                                 POOL_HALO), p_col)),
                  pl.BlockSpec((ng, cg, cg), lambda bi, n: (0, 0, 0)),
                  pl.BlockSpec((1, pw), lambda bi, n: (0, 0))],
        out_specs=pl.BlockSpec((1, ts, pw), lambda bi, n: (bi, n, 0)),
        compiler_params=_params(("parallel", "arbitrary"), 40),
    )(proj3, proj3, w_pool, pool_scale)


def _mix_kernel(a_ref, b_ref, wa_ref, wb_ref, ga_ref, gb_ref, o_ref):
    ya = jnp.dot(a_ref[...], wa_ref[...], preferred_element_type=F32)
    yb = jnp.dot(b_ref[...], wb_ref[...], preferred_element_type=F32)
    ga = jax.nn.sigmoid(ga_ref[...].astype(F32))
    gb = jax.nn.sigmoid(gb_ref[...].astype(F32))
    o_ref[...] = (ga * ya + gb * yb).astype(o_ref.dtype)


def _mix(y_attn, y_pool, wa, wb, proj, ga_col, gb_col):
    t, aw = y_attn.shape
    pw = y_pool.shape[1]
    d = wa.shape[1]
    tm = _pick(t, (512, 256, 128))
    tn = _pick(d, (1024, 512, 256, 128))
    return pl.pallas_call(
        _mix_kernel,
        out_shape=jax.ShapeDtypeStruct((t, d), BF16),
        grid=(t // tm, d // tn),
        in_specs=[pl.BlockSpec((tm, aw), lambda i, j: (i, 0)),
                  pl.BlockSpec((tm, pw), lambda i, j: (i, 0)),
                  pl.BlockSpec((aw, tn), lambda i, j: (0, j)),
                  pl.BlockSpec((pw, tn), lambda i, j: (0, j)),
                  pl.BlockSpec((pl.Element(tm), pl.Element(tn)),
                               lambda i, j: (pl.multiple_of(i * tm, tm),
                                             pl.multiple_of(ga_col + j * tn, LANES))),
                  pl.BlockSpec((pl.Element(tm), pl.Element(tn)),
                               lambda i, j: (pl.multiple_of(i * tm, tm),
                                             pl.multiple_of(gb_col + j * tn, LANES)))],
        out_specs=pl.BlockSpec((tm, tn), lambda i, j: (i, j)),
        compiler_params=_params(("parallel", "arbitrary"), 56),
    )(y_attn, y_pool, wa, wb, proj, proj)


def _out_kernel(m_ref, w_ref, x_ref, g_ref, o_ref):
    y = jnp.dot(m_ref[...], w_ref[...], preferred_element_type=F32)
    o_ref[...] = x_ref[...] + g_ref[0] * y


def _out_proj(mixed, w, x2, gate, seq):
    t, d = x2.shape
    tm = _pick(seq, (512, 256, 128))
    tn = _pick(d, (1024, 512, 256, 128))
    rows_per_batch = seq // tm
    return pl.pallas_call(
        _out_kernel,
        out_shape=jax.ShapeDtypeStruct((t, d), F32),
        grid=(t // tm, d // tn),
        in_specs=[pl.BlockSpec((tm, d), lambda i, j: (i, 0)),
                  pl.BlockSpec((d, tn), lambda i, j: (0, j)),
                  pl.BlockSpec((tm, tn), lambda i, j: (i, j)),
                  pl.BlockSpec((1, 1, tn), lambda i, j: (i // rows_per_batch, 0, j))],
        out_specs=pl.BlockSpec((tm, tn), lambda i, j: (i, j)),
        compiler_params=_params(("parallel", "arbitrary"), 56),
    )(mixed, w, x2, gate)


def _first_argmax(v, iota, n):
    m = jnp.max(v, axis=0, keepdims=True)
    idx = jnp.min(jnp.where(v == m, iota, n), axis=0, keepdims=True)
    return m, idx


def _router_kernel(x_ref, nw_ref, sc_ref, sh_ref, w2_ref, br_ref,
                   hp_ref, mi_ref, mf_ref, cnt_ref, carry_ref, hhi_ref, hlo_ref):
    step = pl.program_id(0)
    tm, d = x_ref.shape

    @pl.when(step == 0)
    def _():
        carry_ref[...] = jnp.zeros_like(carry_ref)

    gain = nw_ref[...] * (1.0 + sc_ref[0])
    shift = sh_ref[0]

    def body(r, carry):
        rows = pl.ds(pl.multiple_of(r * NORM_ROWS, NORM_ROWS), NORM_ROWS)
        h = _modulated_norm(x_ref[rows, :], gain, shift)
        hi = h.astype(BF16)
        hhi_ref[rows, :] = hi
        hlo_ref[rows, :] = (h - hi.astype(F32)).astype(BF16)
        hp_ref[rows, :] = _pack_halves(h)
        return carry

    lax.fori_loop(0, tm // NORM_ROWS, body, 0, unroll=NORM_UNROLL)

    l2 = (jnp.dot(hhi_ref[...], w2_ref[...], preferred_element_type=F32)
          + jnp.dot(hlo_ref[...], w2_ref[...], preferred_element_type=F32))
    lt = l2[:, :LANES] + l2[:, LANES:]
    logits = lt.T[:N_ROUTER] + br_ref[...]
    iota8 = lax.broadcasted_iota(I32, (N_GROUPS, tm), 0)
    gl = logits[:N_GROUPS]
    g_max, g_sel = _first_argmax(gl, iota8, N_GROUPS)
    p_group = 1.0 / jnp.sum(jnp.exp(gl - g_max), axis=0, keepdims=True)
    e_in = jnp.zeros((EXPERTS_PER_GROUP, tm), F32)
    for g in range(N_GROUPS):
        lo = N_GROUPS + g * EXPERTS_PER_GROUP
        e_in = e_in + jnp.where(g_sel == g, logits[lo:lo + EXPERTS_PER_GROUP], 0.0)
    v0, i0 = _first_argmax(e_in, iota8, EXPERTS_PER_GROUP)
    rest = jnp.where(iota8 == i0, -jnp.inf, e_in)
    v1, i1 = _first_argmax(rest, iota8, EXPERTS_PER_GROUP)
    t = jnp.exp(v1 - v0)
    w0 = p_group / (1.0 + t)
    w1 = p_group * t / (1.0 + t)
    e0 = g_sel * EXPERTS_PER_GROUP + i0
    e1 = g_sel * EXPERTS_PER_GROUP + i1

    iota_e = lax.broadcasted_iota(I32, (N_EXPERTS, tm), 0)
    hit0 = iota_e == e0
    hit1 = iota_e == e1
    onehot = jnp.where(hit0 | hit1, 1.0, 0.0).astype(BF16)
    rr = lax.broadcasted_iota(I32, (tm, tm), 0)
    cc = lax.broadcasted_iota(I32, (tm, tm), 1)
    before = jnp.where(rr < cc, 1.0, 0.0).astype(BF16)
    prior = jnp.dot(onehot, before, preferred_element_type=F32) + carry_ref[:, :1]
    r0 = jnp.sum(jnp.where(hit0, prior, 0.0), axis=0, keepdims=True)
    r1 = jnp.sum(jnp.where(hit1, prior, 0.0), axis=0, keepdims=True)
    carry_ref[...] += jnp.dot(onehot, jnp.ones((tm, LANES), BF16), preferred_element_type=F32)

    zi = jnp.zeros((1, tm), I32)
    zf = jnp.zeros((1, tm), F32)
    mi_ref[...] = jnp.concatenate([e0, e1, r0.astype(I32), r1.astype(I32), zi, zi, zi, zi], axis=0)
    mf_ref[...] = jnp.concatenate([w0, w1, zf, zf, zf, zf, zf, zf], axis=0)
    cnt_ref[...] = carry_ref[...]


def _router(x1, nw, scale, shift, w2, br, seq):
    t, d = x1.shape
    tm = _pick(seq, (512, 256, 128))
    rows_per_batch = seq // tm
    return pl.pallas_call(
        _router_kernel,
        out_shape=(jax.ShapeDtypeStruct((t, d // 2), U32),
                   jax.ShapeDtypeStruct((8, t), I32),
                   jax.ShapeDtypeStruct((8, t), F32),
                   jax.ShapeDtypeStruct((N_EXPERTS, LANES), F32)),
        grid=(t // tm,),
        in_specs=[pl.BlockSpec((tm, d), lambda i: (i, 0)),
                  pl.BlockSpec((1, d), lambda i: (0, 0)),
                  pl.BlockSpec((1, 1, d), lambda i: (i // rows_per_batch, 0, 0)),
                  pl.BlockSpec((1, 1, d), lambda i: (i // rows_per_batch, 0, 0)),
                  pl.BlockSpec((d, 2 * LANES), lambda i: (0, 0)),
                  pl.BlockSpec((N_ROUTER, 1), lambda i: (0, 0))],
        out_specs=(pl.BlockSpec((tm, d // 2), lambda i: (i, 0)),
                   pl.BlockSpec((8, tm), lambda i: (0, i)),
                   pl.BlockSpec((8, tm), lambda i: (0, i)),
                   pl.BlockSpec((N_EXPERTS, LANES), lambda i: (0, 0))),
        scratch_shapes=[pltpu.VMEM((N_EXPERTS, LANES), F32), pltpu.VMEM((tm, d), BF16),
                        pltpu.VMEM((tm, d), BF16)],
        compiler_params=_params(("arbitrary",), 56),
    )(x1, nw, scale, shift, w2, br)


PLAN_LANES = 2048
DMA_UNROLL = 4


def _plan_kernel(mi_ref, cnt_ref, dest_ref, tab_ref):
    cnt = cnt_ref[...]
    nb = jnp.floor((cnt + (EXPERT_BLOCK - 1)) * (1.0 / EXPERT_BLOCK))
    r = lax.broadcasted_iota(I32, (N_EXPERTS, N_EXPERTS), 0)
    c = lax.broadcasted_iota(I32, (N_EXPERTS, N_EXPERTS), 1)
    incl = jnp.where(c <= r, 1.0, 0.0).astype(BF16)
    bend = jnp.dot(incl, nb.astype(BF16), preferred_element_type=F32)
    bstart = bend - nb
    first_row = bstart[:, :1] * EXPERT_BLOCK

    tl = mi_ref.shape[1]
    iota_e = lax.broadcasted_iota(I32, (N_EXPERTS, tl), 0)
    for k in range(TOP_K):
        hit = iota_e == mi_ref[k:k + 1, :]
        base = jnp.sum(jnp.where(hit, first_row, 0.0), axis=0, keepdims=True)
        dest_ref[k:k + 1, :] = base.astype(I32) + mi_ref[TOP_K + k:TOP_K + k + 1, :]

    nl = tab_ref.shape[1]
    bidx = lax.broadcasted_iota(I32, (1, nl), 1).astype(F32)
    blk_e = jnp.sum(jnp.where(bend[:, :1] <= bidx, 1.0, 0.0), axis=0, keepdims=True)
    blk_e = jnp.minimum(blk_e, N_EXPERTS - 1.0)
    hit = lax.broadcasted_iota(I32, (N_EXPERTS, nl), 0) == blk_e.astype(I32)
    cnt_b = jnp.sum(jnp.where(hit, cnt[:, :1], 0.0), axis=0, keepdims=True)
    start_b = jnp.sum(jnp.where(hit, bstart[:, :1], 0.0), axis=0, keepdims=True)
    valid = jnp.clip(cnt_b - (bidx - start_b) * EXPERT_BLOCK, 0.0, float(EXPERT_BLOCK))
    n_used = jnp.broadcast_to(jnp.max(bend[:, :1], axis=0, keepdims=True), (1, nl))
    run_end = jnp.sum(jnp.where(hit, bend[:, :1], 0.0), axis=0, keepdims=True)
    zero = jnp.zeros((1, nl), F32)
    tab_ref[...] = jnp.concatenate([blk_e, valid, n_used, run_end, zero, zero, zero, zero],
                                   axis=0).astype(I32)


def _plan(meta_i, cnt, nblk):
    t = meta_i.shape[1]
    tl = _pick(t, (PLAN_LANES, 1024, 512, 256, 128))
    nl = pl.cdiv(nblk, LANES) * LANES
    return pl.pallas_call(
        _plan_kernel,
        out_shape=(jax.ShapeDtypeStruct((TOP_K, t), I32), jax.ShapeDtypeStruct((8, nl), I32)),
        grid=(t // tl,),
        in_specs=[pl.BlockSpec((8, tl), lambda i: (0, i)),
                  pl.BlockSpec((N_EXPERTS, LANES), lambda i: (0, 0))],
        out_specs=(pl.BlockSpec((TOP_K, tl), lambda i: (0, i)),
                   pl.BlockSpec((8, nl), lambda i: (0, 0))),
        compiler_params=_params(("arbitrary",), 32),
    )(meta_i, cnt)


def _dispatch_kernel(dest_ref, hp_ref, xs_ref, sem, *, chunk, t):
    base = pl.program_id(0) * chunk

    def issue(i, carry):
        for k in range(TOP_K):
            pltpu.make_async_copy(hp_ref.at[pl.ds(i, 1)],
                                  xs_ref.at[pl.ds(dest_ref[k * t + base + i], 1)],
                                  sem).start(priority=k)
        return carry

    lax.fori_loop(0, chunk, issue, 0, unroll=DMA_UNROLL)
    for k in range(TOP_K):
        pltpu.make_async_copy(hp_ref, xs_ref.at[pl.ds(0, chunk)], sem).wait()


def _dispatch(dest_flat, hp, cap):
    t, half = hp.shape
    chunk = _pick(t, (512, 256, 128))
    grid_spec = pltpu.PrefetchScalarGridSpec(
        num_scalar_prefetch=1,
        grid=(t // chunk,),
        in_specs=[pl.BlockSpec((chunk, half), lambda i, ds: (i, 0))],
        out_specs=pl.BlockSpec(memory_space=pl.ANY),
        scratch_shapes=[pltpu.SemaphoreType.DMA(())],
    )
    return pl.pallas_call(
        functools.partial(_dispatch_kernel, chunk=chunk, t=t),
        out_shape=jax.ShapeDtypeStruct((cap, half), U32),
        grid_spec=grid_spec,
        compiler_params=pltpu.CompilerParams(dimension_semantics=("arbitrary",),
                                             has_side_effects=True, vmem_limit_bytes=32 * MIB),
    )(dest_flat, hp)


def _ffn_kernel(be_ref, nv_ref, nu_ref, re_ref, xs_ref, wg_hbm, wu_hbm, wd_hbm, o_ref,
                wg_s, wu_s, wd_s, slot_ref, sem):
    b = pl.program_id(0)

    def weight_copies(e, slot):
        return (pltpu.make_async_copy(wg_hbm.at[e], wg_s.at[slot], sem.at[slot]),
                pltpu.make_async_copy(wu_hbm.at[e], wu_s.at[slot], sem.at[slot]),
                pltpu.make_async_copy(wd_hbm.at[e], wd_s.at[slot], sem.at[slot]))

    @pl.when(b < nu_ref[0])
    def _():
        e = be_ref[b]

        @pl.when(b == 0)
        def _():
            slot_ref[0] = 0
            for c in weight_copies(e, 0):
                c.start()

        first = jnp.logical_or(b == 0, be_ref[jnp.maximum(b - 1, 0)] != e)

        @pl.when(first)
        def _():
            slot = jnp.where(b == 0, 0, 1 - slot_ref[0])
            slot_ref[0] = slot
            for c in weight_copies(e, slot):
                c.wait()
            nxt = re_ref[b]

            @pl.when(nxt < nu_ref[0])
            def _():
                for c in weight_copies(be_ref[nxt], 1 - slot):
                    c.start()

        slot = slot_ref[0]
        blk, half = xs_ref.shape
        row = lax.broadcasted_iota(I32, (blk, 1), 0)
        xp = jnp.where(row < nv_ref[b], xs_ref[...], jnp.uint32(0))
        lo, hi = _unpack_halves(xp)
        lo = lo.astype(BF16)
        hi = hi.astype(BF16)
        a = (jnp.dot(lo, wg_s[slot, :half], preferred_element_type=F32)
             + jnp.dot(hi, wg_s[slot, half:], preferred_element_type=F32))
        u = (jnp.dot(lo, wu_s[slot, :half], preferred_element_type=F32)
             + jnp.dot(hi, wu_s[slot, half:], preferred_element_type=F32))
        mid = (a * jax.nn.sigmoid(a) * u).astype(BF16)
        o_ref[...] = _pack_halves(jnp.dot(mid, wd_s[slot], preferred_element_type=F32))


def _experts(blk_e, blk_valid, n_used, run_end, xs, wg, wu, wd):
    cap, half = xs.shape
    d = 2 * half
    ff = wg.shape[2]
    nblk = cap // EXPERT_BLOCK

    def row_map(b, be, nv, nu, re):
        return (jnp.minimum(b, nu[0] - 1), 0)

    grid_spec = pltpu.PrefetchScalarGridSpec(
        num_scalar_prefetch=4,
        grid=(nblk,),
        in_specs=[pl.BlockSpec((EXPERT_BLOCK, half), row_map),
                  pl.BlockSpec(memory_space=pl.ANY),
                  pl.BlockSpec(memory_space=pl.ANY),
                  pl.BlockSpec(memory_space=pl.ANY)],
        out_specs=pl.BlockSpec((EXPERT_BLOCK, half), row_map),
        scratch_shapes=[pltpu.VMEM((2, d, ff), BF16), pltpu.VMEM((2, d, ff), BF16),
                        pltpu.VMEM((2, ff, d), BF16), pltpu.SMEM((1,), I32),
                        pltpu.SemaphoreType.DMA((2,))],
    )
    return pl.pallas_call(
        _ffn_kernel,
        out_shape=jax.ShapeDtypeStruct((cap, half), U32),
        grid_spec=grid_spec,
        compiler_params=_params(("arbitrary",), 56),
    )(blk_e, blk_valid, n_used, run_end, xs, wg, wu, wd)


def _combine_kernel(dest_ref, x_ref, g_ref, w_ref, ys_ref, o_ref, ybuf, sem, *, chunk, t):
    i = pl.program_id(0)
    n = pl.num_programs(0)

    def issue(step, slot):
        base = step * chunk

        def body(r, carry):
            for k in range(TOP_K):
                pltpu.make_async_copy(ys_ref.at[pl.ds(dest_ref[k * t + base + r], 1)],
                                      ybuf.at[slot, k, pl.ds(r, 1)],
                                      sem.at[slot]).start(priority=k)
            return carry

        lax.fori_loop(0, chunk, body, 0, unroll=DMA_UNROLL)

    @pl.when(i == 0)
    def _():
        issue(0, 0)

    slot = i % 2

    @pl.when(i + 1 < n)
    def _():
        issue(i + 1, 1 - slot)

    for k in range(TOP_K):
        pltpu.make_async_copy(ys_ref.at[pl.ds(0, chunk)], ybuf.at[slot, k], sem.at[slot]).wait()
    w = w_ref[...]
    half = ybuf.shape[-1]
    lo0, hi0 = _unpack_halves(ybuf[slot, 0])
    lo1, hi1 = _unpack_halves(ybuf[slot, 1])
    g = g_ref[0]
    o_ref[:, :half] = x_ref[:, :half] + g[:, :half] * (w[:, 0:1] * lo0 + w[:, 1:2] * lo1)
    o_ref[:, half:] = x_ref[:, half:] + g[:, half:] * (w[:, 0:1] * hi0 + w[:, 1:2] * hi1)


def _combine(dest_flat, x1, gate, w_rows, ys, seq):
    t, d = x1.shape
    chunk = _pick(seq, (256, 128))
    rows_per_batch = seq // chunk
    grid_spec = pltpu.PrefetchScalarGridSpec(
        num_scalar_prefetch=1,
        grid=(t // chunk,),
        in_specs=[pl.BlockSpec((chunk, d), lambda i, ds: (i, 0)),
                  pl.BlockSpec((1, 1, d), lambda i, ds: (i // rows_per_batch, 0, 0)),
                  pl.BlockSpec((chunk, 8), lambda i, ds: (i, 0)),
                  pl.BlockSpec(memory_space=pl.ANY)],
        out_specs=pl.BlockSpec((chunk, d), lambda i, ds: (i, 0)),
        scratch_shapes=[pltpu.VMEM((2, TOP_K, chunk, d // 2), U32),
                        pltpu.SemaphoreType.DMA((2,))],
    )
    return pl.pallas_call(
        functools.partial(_combine_kernel, chunk=chunk, t=t),
        out_shape=jax.ShapeDtypeStruct((t, d), F32),
        grid_spec=grid_spec,
        compiler_params=_params(("arbitrary",), 56),
    )(dest_flat, x1, gate, w_rows, ys)


def _layer(x, c, w_ada, b_ada, norm1_w, w_in, q_norm_w, k_norm_w, sinks, w_pool, pool_scale,
           w_attn_up, w_pool_up, w_out, norm2_w, w_router_group, b_router_group,
           w_router_expert, b_router_expert, w_gate, w_up, w_down):
    b, s, d = x.shape
    t = b * s
    aw = w_attn_up.shape[0]
    pw = w_pool_up.shape[0]
    kvw = N_KV_HEADS * HEAD_DIM
    x2 = x.reshape(t, d)

    c8 = jnp.zeros((8, d), F32).at[:b].set(c)
    ada = _ada(c8, w_ada, b_ada.reshape(1, 6 * d))[:b]
    shift1, scale1, gate1, shift2, scale2, gate2 = [a.reshape(b, 1, d) for a in jnp.split(ada, 6, axis=-1)]

    q_col, kv_col = 0, aw
    p_col = aw + 2 * kvw
    ga_col = p_col + pw
    gb_col = ga_col + d
    h = _norm(x2, norm1_w.reshape(1, d), scale1, shift1, s)
    ne, _, ff = w_gate.shape
    proj, (wu_b, wd_b, wa_b, wb_b, wo_b) = _in_proj(
        h, w_in.astype(BF16),
        [w_up.reshape(ne * d, ff), w_down.reshape(ne * ff, d), w_attn_up, w_pool_up, w_out])
    wu_b = wu_b.reshape(ne, d, ff)
    wd_b = wd_b.reshape(ne, ff, d)
    proj3 = proj.reshape(b, s, proj.shape[1])

    y_attn, wg_b = _attention(proj3, sinks, q_norm_w, k_norm_w, q_col, kv_col, w_gate)
    y_attn = y_attn.reshape(t, aw)
    y_pool = _pool(proj3, w_pool.astype(BF16), pool_scale.reshape(1, pw), p_col).reshape(t, pw)
    mixed = _mix(y_attn, y_pool, wa_b, wb_b, proj, ga_col, gb_col)
    x1 = _out_proj(mixed, wo_b, x2, gate1, s)

    wr = jnp.concatenate([w_router_group, w_router_expert,
                          jnp.zeros((d, LANES - N_ROUTER), F32)], axis=1)
    wr_hi = wr.astype(BF16)
    wr_lo = (wr - wr_hi.astype(F32)).astype(BF16)
    wr2 = jnp.concatenate([wr_hi, wr_lo], axis=1)
    br = jnp.concatenate([b_router_group, b_router_expert]).reshape(N_ROUTER, 1)
    hp, meta_i, meta_f, cnt = _router(x1, norm2_w.reshape(1, d), scale2, shift2, wr2, br, s)

    cap = TOP_K * t + N_EXPERTS * EXPERT_BLOCK
    nblk = cap // EXPERT_BLOCK
    dest, tab = _plan(meta_i, cnt, nblk)
    dest_flat = dest.reshape(TOP_K * t)
    blk_e, blk_valid, n_used, run_end = tab[0, :nblk], tab[1, :nblk], tab[2, :1], tab[3, :nblk]

    xs = _dispatch(dest_flat, hp, cap)
    ys = _experts(blk_e, blk_valid, n_used, run_end, xs, wg_b, wu_b, wd_b)
    out = _combine(dest_flat, x1, gate2, meta_f.T, ys, s)
    return out.reshape(b, s, d)


def kernel(x, c, w_ada, b_ada, norm1_w, w_in, q_norm_w, k_norm_w, sinks, w_pool, pool_scale,
           w_attn_up, w_pool_up, w_out, norm2_w, w_router_group, b_router_group,
           w_router_expert, b_router_expert, w_gate, w_up, w_down):
    for l in range(w_ada.shape[0]):
        x = _layer(x, c, w_ada[l], b_ada[l], norm1_w[l], w_in[l], q_norm_w[l], k_norm_w[l],
                   sinks[l], w_pool[l], pool_scale[l], w_attn_up[l], w_pool_up[l], w_out[l],
                   norm2_w[l], w_router_group[l], b_router_group[l], w_router_expert[l],
                   b_router_expert[l], w_gate[l], w_up[l], w_down[l])
    return x
```

```python
import functools

import jax
import jax.numpy as jnp
import numpy as np
from jax import lax
from jax.experimental import pallas as pl
from jax.experimental.pallas import tpu as pltpu

F32 = jnp.float32
BF16 = jnp.bfloat16
U32 = jnp.uint32
I32 = jnp.int32

HEAD_DIM = 64
N_Q_HEADS = 32
N_KV_HEADS = 4
Q_PER_KV = N_Q_HEADS // N_KV_HEADS
ATTN_BLOCK = 128
WINDOW = 128
POOL_WINDOWS = (2, 4, 8, 16)
POOL_HALO = 16
N_GROUPS = 8
EXPERTS_PER_GROUP = 8
N_EXPERTS = N_GROUPS * EXPERTS_PER_GROUP
N_ROUTER = N_GROUPS + N_EXPERTS
TOP_K = 2
EPS = 1e-6
NEG_INF = -1e30
LOG2E = 1.4426950408889634
ATTN_STAGE_PAIRS = 8

LANES = 128
EXPERT_BLOCK = 256
MIB = 1024 * 1024


def _params(semantics, vmem_mib):
    return pltpu.CompilerParams(dimension_semantics=semantics, vmem_limit_bytes=vmem_mib * MIB)


def _pick(n, candidates):
    for c in candidates:
        if n % c == 0:
            return c
    return n


def _ada_kernel(c_ref, w_ref, b_ref, o_ref):
    c = c_ref[...]
    s = (c * jax.nn.sigmoid(c)).astype(BF16)
    o_ref[...] = jnp.dot(s, w_ref[...].astype(BF16), preferred_element_type=F32) + b_ref[...]


def _ada(c8, w, b):
    d, n = w.shape
    tn = _pick(n, (1024, 512, 256, 128))
    return pl.pallas_call(
        _ada_kernel,
        out_shape=jax.ShapeDtypeStruct((8, n), F32),
        grid=(n // tn,),
        in_specs=[pl.BlockSpec((8, d), lambda j: (0, 0)),
                  pl.BlockSpec((d, tn), lambda j: (0, j)),
                  pl.BlockSpec((1, tn), lambda j: (0, j))],
        out_specs=pl.BlockSpec((8, tn), lambda j: (0, j)),
        compiler_params=_params(("arbitrary",), 56),
    )(c8, w, b)


def _modulated_norm(x, gain, shift):
    ms = jnp.mean(x * x, axis=-1, keepdims=True)
    return x * lax.rsqrt(ms + EPS) * gain + shift


NORM_ROWS = 16
NORM_UNROLL = 4


def _pack_halves(v):
    n = v.shape[1] // 2
    bits = lax.bitcast_convert_type(v.astype(BF16).astype(F32), U32)
    return (bits[:, :n] >> 16) | (bits[:, n:] & jnp.uint32(0xFFFF0000))


def _unpack_halves(p):
    lo = lax.bitcast_convert_type(p << 16, F32)
    hi = lax.bitcast_convert_type(p & jnp.uint32(0xFFFF0000), F32)
    return lo, hi


def _norm_kernel(x_ref, nw_ref, sc_ref, sh_ref, o_ref):
    gain = nw_ref[...] * (1.0 + sc_ref[0])
    shift = sh_ref[0]

    def body(r, carry):
        rows = pl.ds(pl.multiple_of(r * NORM_ROWS, NORM_ROWS), NORM_ROWS)
        o_ref[rows, :] = _modulated_norm(x_ref[rows, :], gain, shift).astype(o_ref.dtype)
        return carry

    lax.fori_loop(0, x_ref.shape[0] // NORM_ROWS, body, 0, unroll=NORM_UNROLL)


def _norm(x2, nw, scale, shift, seq):
    t, d = x2.shape
    tm = _pick(seq, (512, 256, 128))
    rows_per_batch = seq // tm
    return pl.pallas_call(
        _norm_kernel,
        out_shape=jax.ShapeDtypeStruct((t, d), BF16),
        grid=(t // tm,),
        in_specs=[pl.BlockSpec((tm, d), lambda i: (i, 0)),
                  pl.BlockSpec((1, d), lambda i: (0, 0)),
                  pl.BlockSpec((1, 1, d), lambda i: (i // rows_per_batch, 0, 0)),
                  pl.BlockSpec((1, 1, d), lambda i: (i // rows_per_batch, 0, 0))],
        out_specs=pl.BlockSpec((tm, d), lambda i: (i, 0)),
        compiler_params=_params(("parallel",), 40),
    )(x2, nw, scale, shift)


def _in_kernel(a_ref, w_ref, *refs):
    n_riders = (len(refs) - 1) // 2
    o_ref = refs[n_riders]
    o_ref[...] = jnp.dot(a_ref[...], w_ref[...], preferred_element_type=F32).astype(o_ref.dtype)
    for src, dst in zip(refs[:n_riders], refs[n_riders + 1:]):
        dst[...] = src[...].astype(BF16)


def _rider_steps(nrows, steps):
    n = 1 << (steps.bit_length() - 1)
    while nrows % n or (nrows // n) % 16:
        n //= 2
    return n


def _in_proj(h, w, riders):
    t, d = h.shape
    n = w.shape[1]
    tm = _pick(t, (1024, 512, 256, 128))
    tn = _pick(n, (512, 256, 128))
    nj = n // tn
    steps = (t // tm) * nj

    def rider_spec(arr):
        nsteps = _rider_steps(arr.shape[0], steps)
        return pl.BlockSpec((arr.shape[0] // nsteps, arr.shape[1]),
                            lambda i, j: (jnp.minimum(i * nj + j, nsteps - 1), 0))

    rider_specs = [rider_spec(r) for r in riders]
    outs = pl.pallas_call(
        _in_kernel,
        out_shape=tuple([jax.ShapeDtypeStruct((t, n), BF16)]
                        + [jax.ShapeDtypeStruct(r.shape, BF16) for r in riders]),
        grid=(t // tm, nj),
        in_specs=[pl.BlockSpec((tm, d), lambda i, j: (i, 0)),
                  pl.BlockSpec((d, tn), lambda i, j: (0, j))] + rider_specs,
        out_specs=tuple([pl.BlockSpec((tm, tn), lambda i, j: (i, j))] + rider_specs),
        compiler_params=_params(("arbitrary", "arbitrary"), 56),
    )(h, w, *riders)
    return outs[0], outs[1:]


def _head_sumsq(v, ones_bd):
    sq = v * v
    hi = sq.astype(BF16)
    lo = (sq - hi.astype(F32)).astype(BF16)
    return (jnp.dot(hi, ones_bd, preferred_element_type=F32)
            + jnp.dot(lo, ones_bd, preferred_element_type=F32))


def _head_rms_norm(v, w2, ones_bd):
    ss = _head_sumsq(v, ones_bd)
    return v * lax.rsqrt(ss * (1.0 / HEAD_DIM) + EPS) * w2


def _attn_kernel(sink_ref, q_ref, kvc_ref, kvp_ref, bias_ref, qw_ref, kw_ref, o_ref):
    blk = ATTN_BLOCK
    kvw = N_KV_HEADS * HEAD_DIM
    lane = lax.broadcasted_iota(I32, (1, LANES), 1)
    low_half = lane < HEAD_DIM
    r = lax.broadcasted_iota(I32, (LANES, LANES), 0) // HEAD_DIM
    c = lax.broadcasted_iota(I32, (LANES, LANES), 1) // HEAD_DIM
    ones_bd = jnp.where(r == c, 1.0, 0.0).astype(BF16)

    kv = jnp.concatenate([kvp_ref[0], kvc_ref[0]], axis=0).astype(F32)
    kw2 = kw_ref[...]
    sum_a = jnp.broadcast_to(jnp.where(low_half, 1.0, 0.0), (2 * blk, LANES))
    sum_b = jnp.broadcast_to(jnp.where(low_half, 0.0, 1.0), (2 * blk, LANES))

    k_bds, v_bds = [], []
    for g in range(N_KV_HEADS):
        chunk = g // 2
        kc = _head_rms_norm(kv[:, chunk * LANES:(chunk + 1) * LANES], kw2, ones_bd)
        vc = kv[:, kvw + chunk * LANES: kvw + (chunk + 1) * LANES]
        kr = pltpu.roll(kc, HEAD_DIM, axis=1)
        vr = pltpu.roll(vc, HEAD_DIM, axis=1)
        if g % 2 == 0:
            k_lo, k_hi, v_lo, v_hi = kc, kr, vc, vr
        else:
            k_lo, k_hi, v_lo, v_hi = kr, kc, vr, vc
        k_bds.append(jnp.concatenate([jnp.where(low_half, k_lo, 0.0),
                                      jnp.where(low_half, 0.0, k_hi)], axis=0).astype(BF16))
        v_bds.append(jnp.concatenate(
            [jnp.concatenate([jnp.where(low_half, v_lo, 0.0), sum_a], axis=1),
             jnp.concatenate([jnp.where(low_half, 0.0, v_hi), sum_b], axis=1)],
            axis=0).astype(BF16))

    pairs_per_kv = Q_PER_KV // 2
    for first in range(0, N_Q_HEADS // 2, ATTN_STAGE_PAIRS):
        pairs = range(first, first + ATTN_STAGE_PAIRS)
        scores = []
        for j in pairs:
            qp = q_ref[0, :, j * LANES:(j + 1) * LANES].astype(F32)
            qn = _head_rms_norm(qp, qw_ref[...], ones_bd).astype(BF16)
            s = lax.dot_general(qn, k_bds[j // pairs_per_kv], (((1,), (1,)), ((), ())),
                                preferred_element_type=F32)
            scores.append(s + bias_ref[0, j])
        probs = []
        for j, s in zip(pairs, scores):
            sink_a = sink_ref[2 * j]
            sink_b = sink_ref[2 * j + 1]
            m_a = jnp.maximum(jnp.max(s[:, :2 * blk], axis=-1, keepdims=True), sink_a)
            m_b = jnp.maximum(jnp.max(s[:, 2 * blk:], axis=-1, keepdims=True), sink_b)
            p = jnp.concatenate([jnp.exp2(s[:, :2 * blk] - m_a),
                                 jnp.exp2(s[:, 2 * blk:] - m_b)], axis=1).astype(BF16)
            esink = jnp.where(low_half, jnp.exp2(sink_a - m_a), jnp.exp2(sink_b - m_b))
            probs.append((p, esink))
        for j, (p, esink) in zip(pairs, probs):
            ol = jnp.dot(p, v_bds[j // pairs_per_kv], preferred_element_type=F32)
            o = ol[:, :LANES] / (ol[:, LANES:] + esink)
            o_ref[0, :, j * LANES:(j + 1) * LANES] = o.astype(o_ref.dtype)


def _attn_bias():
    blk = ATTN_BLOCK
    i = np.arange(blk)[:, None]
    j = np.arange(2 * blk)[None, :]
    dist = i + blk - j
    band = (dist >= 0) & (dist < WINDOW)
    slopes = (2.0 ** (-8.0 * np.arange(1, N_Q_HEADS + 1, dtype=np.float32) / N_Q_HEADS)).astype(np.float32)
    alibi = -slopes[:, None, None] * dist.astype(np.float32)[None] * np.float32(LOG2E)
    later = np.where(band[None], alibi, np.float32(NEG_INF))
    first = np.where((band & (j >= blk))[None], alibi, np.float32(NEG_INF))
    tab = np.stack([first, later]).astype(np.float32)
    tab = tab.reshape(2, N_Q_HEADS // 2, 2, blk, 2 * blk)
    return np.transpose(tab, (0, 1, 3, 2, 4)).reshape(2, N_Q_HEADS // 2, blk, 4 * blk)


def _attention(proj3, sinks, q_norm_w, k_norm_w, q_col, kv_col):
    b, s, _ = proj3.shape
    aw = N_Q_HEADS * HEAD_DIM
    kvw2 = 2 * N_KV_HEADS * HEAD_DIM
    nb = s // ATTN_BLOCK
    bias = _attn_bias()
    qw2 = (jnp.tile(q_norm_w, 2) * (HEAD_DIM ** -0.5 * LOG2E)).reshape(1, LANES)
    sinks = sinks * LOG2E
    kw2 = jnp.tile(k_norm_w, 2).reshape(1, LANES)
    grid_spec = pltpu.PrefetchScalarGridSpec(
        num_scalar_prefetch=1,
        grid=(b, nb),
        in_specs=[pl.BlockSpec((1, ATTN_BLOCK, aw), lambda bi, n, sk: (bi, n, q_col // aw)),
                  pl.BlockSpec((1, ATTN_BLOCK, kvw2), lambda bi, n, sk: (bi, n, kv_col // kvw2)),
                  pl.BlockSpec((1, ATTN_BLOCK, kvw2),
                               lambda bi, n, sk: (bi, jnp.maximum(n - 1, 0), kv_col // kvw2)),
                  pl.BlockSpec((1, N_Q_HEADS // 2, ATTN_BLOCK, 4 * ATTN_BLOCK),
                               lambda bi, n, sk: (jnp.minimum(n, 1), 0, 0, 0)),
                  pl.BlockSpec((1, LANES), lambda bi, n, sk: (0, 0)),
                  pl.BlockSpec((1, LANES), lambda bi, n, sk: (0, 0))],
        out_specs=pl.BlockSpec((1, ATTN_BLOCK, aw), lambda bi, n, sk: (bi, n, 0)),
    )
    return pl.pallas_call(
        _attn_kernel,
        out_shape=jax.ShapeDtypeStruct((b, s, aw), BF16),
        grid_spec=grid_spec,
        compiler_params=_params(("parallel", "arbitrary"), 40),
    )(sinks, proj3, proj3, proj3, bias, qw2, kw2)


def _pool_kernel(pc_ref, ph_ref, w_ref, ps_ref, o_ref):
    n = pl.program_id(1)
    ts = pc_ref.shape[1]
    cg = w_ref.shape[1]
    cur = pc_ref[0].astype(F32)
    halo = jnp.where(n > 0, ph_ref[0].astype(F32), 0.0)
    ext = jnp.concatenate([halo, cur], axis=0)
    t1 = (lax.broadcasted_iota(I32, (ts, 1), 0) + n * ts + 1).astype(F32)
    for g, win in enumerate(POOL_WINDOWS):
        acc = ext[:, g * cg:(g + 1) * cg]
        k = 1
        while k < win:
            acc = acc + pltpu.roll(acc, k, axis=0)
            k *= 2
        mean = acc[POOL_HALO:] / jnp.minimum(t1, float(win))
        d = (mean - cur[:, g * cg:(g + 1) * cg]).astype(BF16)
        y = jnp.dot(d, w_ref[g], preferred_element_type=F32) * ps_ref[:, g * cg:(g + 1) * cg]
        o_ref[0, :, g * cg:(g + 1) * cg] = y.astype(o_ref.dtype)


def _pool(proj3, w_pool, pool_scale, p_col):
    b, s, _ = proj3.shape
    ng, cg, _ = w_pool.shape
    pw = ng * cg
    ts = _pick(s, (512, 256, 128))
    return pl.pallas_call(
        _pool_kernel,
        out_shape=jax.ShapeDtypeStruct((b, s, pw), BF16),
        grid=(b, s // ts),
        in_specs=[pl.BlockSpec((pl.Element(1), pl.Element(ts), pl.Element(pw)),
                               lambda bi, n: (bi, pl.multiple_of(n * ts, ts), p_col)),
                  pl.BlockSpec((pl.Element(1), pl.Element(POOL_HALO), pl.Element(pw)),
                               lambda bi, n: (bi, pl.multiple_of(jnp.maximum(n * ts - POOL_HALO, 0),
                                                                 POOL_HALO), p_col)),
                  pl.BlockSpec((ng, cg, cg), lambda bi, n: (0, 0, 0)),
                  pl.BlockSpec((1, pw), lambda bi, n: (0, 0))],
        out_specs=pl.BlockSpec((1, ts, pw), lambda bi, n: (bi, n, 0)),
        compiler_params=_params(("parallel", "arbitrary"), 40),
    )(proj3, proj3, w_pool, pool_scale)


def _mix_kernel(a_ref, b_ref, wa_ref, wb_ref, ga_ref, gb_ref, o_ref):
    ya = jnp.dot(a_ref[...], wa_ref[...], preferred_element_type=F32)
    yb = jnp.dot(b_ref[...], wb_ref[...], preferred_element_type=F32)
    ga = jax.nn.sigmoid(ga_ref[...].astype(F32))
    gb = jax.nn.sigmoid(gb_ref[...].astype(F32))
    o_ref[...] = (ga * ya + gb * yb).astype(o_ref.dtype)


def _mix(y_attn, y_pool, wa, wb, proj, ga_col, gb_col):
    t, aw = y_attn.shape
    pw = y_pool.shape[1]
    d = wa.shape[1]
    tm = _pick(t, (512, 256, 128))
    tn = _pick(d, (1024, 512, 256, 128))
    return pl.pallas_call(
        _mix_kernel,
        out_shape=jax.ShapeDtypeStruct((t, d), BF16),
        grid=(t // tm, d // tn),
        in_specs=[pl.BlockSpec((tm, aw), lambda i, j: (i, 0)),
                  pl.BlockSpec((tm, pw), lambda i, j: (i, 0)),
                  pl.BlockSpec((aw, tn), lambda i, j: (0, j)),
                  pl.BlockSpec((pw, tn), lambda i, j: (0, j)),
                  pl.BlockSpec((pl.Element(tm), pl.Element(tn)),
                               lambda i, j: (pl.multiple_of(i * tm, tm),
                                             pl.multiple_of(ga_col + j * tn, LANES))),
                  pl.BlockSpec((pl.Element(tm), pl.Element(tn)),
                               lambda i, j: (pl.multiple_of(i * tm, tm),
                                             pl.multiple_of(gb_col + j * tn, LANES)))],
        out_specs=pl.BlockSpec((tm, tn), lambda i, j: (i, j)),
        compiler_params=_params(("parallel", "arbitrary"), 56),
    )(y_attn, y_pool, wa, wb, proj, proj)


def _out_kernel(m_ref, w_ref, x_ref, g_ref, o_ref):
    y = jnp.dot(m_ref[...], w_ref[...], preferred_element_type=F32)
    o_ref[...] = x_ref[...] + g_ref[0] * y


def _out_proj(mixed, w, x2, gate, seq):
    t, d = x2.shape
    tm = _pick(seq, (1024, 512, 256, 128))
    tn = _pick(d, (1024, 512, 256, 128))
    rows_per_batch = seq // tm
    return pl.pallas_call(
        _out_kernel,
        out_shape=jax.ShapeDtypeStruct((t, d), F32),
        grid=(t // tm, d // tn),
        in_specs=[pl.BlockSpec((tm, d), lambda i, j: (i, 0)),
                  pl.BlockSpec((d, tn), lambda i, j: (0, j)),
                  pl.BlockSpec((tm, tn), lambda i, j: (i, j)),
                  pl.BlockSpec((1, 1, tn), lambda i, j: (i // rows_per_batch, 0, j))],
        out_specs=pl.BlockSpec((tm, tn), lambda i, j: (i, j)),
        compiler_params=_params(("parallel", "arbitrary"), 56),
    )(mixed, w, x2, gate)


def _first_argmax(v, iota, n):
    m = jnp.max(v, axis=0, keepdims=True)
    idx = jnp.min(jnp.where(v == m, iota, n), axis=0, keepdims=True)
    return m, idx


def _router_kernel(x_ref, nw_ref, sc_ref, sh_ref, w2_ref, br_ref,
                   hp_ref, mi_ref, mf_ref, cnt_ref, carry_ref, hb_ref):
    step = pl.program_id(0)
    tm, d = x_ref.shape

    @pl.when(step == 0)
    def _():
        carry_ref[...] = jnp.zeros_like(carry_ref)

    gain = nw_ref[...] * (1.0 + sc_ref[0])
    shift = sh_ref[0]

    def body(r, carry):
        rows = pl.ds(pl.multiple_of(r * NORM_ROWS, NORM_ROWS), NORM_ROWS)
        h = _modulated_norm(x_ref[rows, :], gain, shift)
        hb_ref[rows, :] = h.astype(BF16)
        hp_ref[rows, :] = _pack_halves(h)
        return carry

    lax.fori_loop(0, tm // NORM_ROWS, body, 0, unroll=NORM_UNROLL)

    l2 = jnp.dot(hb_ref[...], w2_ref[...], preferred_element_type=F32)
    lt = l2[:, :LANES] + l2[:, LANES:]
    logits = lt.T[:N_ROUTER] + br_ref[...]
    iota8 = lax.broadcasted_iota(I32, (N_GROUPS, tm), 0)
    gl = logits[:N_GROUPS]
    g_max, g_sel = _first_argmax(gl, iota8, N_GROUPS)
    p_group = 1.0 / jnp.sum(jnp.exp(gl - g_max), axis=0, keepdims=True)
    e_in = jnp.zeros((EXPERTS_PER_GROUP, tm), F32)
    for g in range(N_GROUPS):
        lo = N_GROUPS + g * EXPERTS_PER_GROUP
        e_in = e_in + jnp.where(g_sel == g, logits[lo:lo + EXPERTS_PER_GROUP], 0.0)
    v0, i0 = _first_argmax(e_in, iota8, EXPERTS_PER_GROUP)
    rest = jnp.where(iota8 == i0, -jnp.inf, e_in)
    v1, i1 = _first_argmax(rest, iota8, EXPERTS_PER_GROUP)
    t = jnp.exp(v1 - v0)
    w0 = p_group / (1.0 + t)
    w1 = p_group * t / (1.0 + t)
    e0 = g_sel * EXPERTS_PER_GROUP + i0
    e1 = g_sel * EXPERTS_PER_GROUP + i1

    iota_e = lax.broadcasted_iota(I32, (N_EXPERTS, tm), 0)
    hit0 = iota_e == e0
    hit1 = iota_e == e1
    onehot = jnp.where(hit0 | hit1, 1.0, 0.0).astype(BF16)
    rr = lax.broadcasted_iota(I32, (tm, tm), 0)
    cc = lax.broadcasted_iota(I32, (tm, tm), 1)
    before = jnp.where(rr < cc, 1.0, 0.0).astype(BF16)
    prior = jnp.dot(onehot, before, preferred_element_type=F32) + carry_ref[:, :1]
    r0 = jnp.sum(jnp.where(hit0, prior, 0.0), axis=0, keepdims=True)
    r1 = jnp.sum(jnp.where(hit1, prior, 0.0), axis=0, keepdims=True)
    carry_ref[...] += jnp.dot(onehot, jnp.ones((tm, LANES), BF16), preferred_element_type=F32)

    zi = jnp.zeros((1, tm), I32)
    zf = jnp.zeros((1, tm), F32)
    mi_ref[...] = jnp.concatenate([e0, e1, r0.astype(I32), r1.astype(I32), zi, zi, zi, zi], axis=0)
    mf_ref[...] = jnp.concatenate([w0, w1, zf, zf, zf, zf, zf, zf], axis=0)
    cnt_ref[...] = carry_ref[...]


def _router(x1, nw, scale, shift, w2, br, seq):
    t, d = x1.shape
    tm = _pick(seq, (512, 256, 128))
    rows_per_batch = seq // tm
    return pl.pallas_call(
        _router_kernel,
        out_shape=(jax.ShapeDtypeStruct((t, d // 2), U32),
                   jax.ShapeDtypeStruct((8, t), I32),
                   jax.ShapeDtypeStruct((8, t), F32),
                   jax.ShapeDtypeStruct((N_EXPERTS, LANES), F32)),
        grid=(t // tm,),
        in_specs=[pl.BlockSpec((tm, d), lambda i: (i, 0)),
                  pl.BlockSpec((1, d), lambda i: (0, 0)),
                  pl.BlockSpec((1, 1, d), lambda i: (i // rows_per_batch, 0, 0)),
                  pl.BlockSpec((1, 1, d), lambda i: (i // rows_per_batch, 0, 0)),
                  pl.BlockSpec((d, 2 * LANES), lambda i: (0, 0)),
                  pl.BlockSpec((N_ROUTER, 1), lambda i: (0, 0))],
        out_specs=(pl.BlockSpec((tm, d // 2), lambda i: (i, 0)),
                   pl.BlockSpec((8, tm), lambda i: (0, i)),
                   pl.BlockSpec((8, tm), lambda i: (0, i)),
                   pl.BlockSpec((N_EXPERTS, LANES), lambda i: (0, 0))),
        scratch_shapes=[pltpu.VMEM((N_EXPERTS, LANES), F32), pltpu.VMEM((tm, d), BF16)],
        compiler_params=_params(("arbitrary",), 56),
    )(x1, nw, scale, shift, w2, br)


PLAN_LANES = 2048
DMA_UNROLL = 4


def _plan_kernel(mi_ref, cnt_ref, dest_ref, tab_ref):
    cnt = cnt_ref[...]
    nb = jnp.floor((cnt + (EXPERT_BLOCK - 1)) * (1.0 / EXPERT_BLOCK))
    r = lax.broadcasted_iota(I32, (N_EXPERTS, N_EXPERTS), 0)
    c = lax.broadcasted_iota(I32, (N_EXPERTS, N_EXPERTS), 1)
    incl = jnp.where(c <= r, 1.0, 0.0).astype(BF16)
    bend = jnp.dot(incl, nb.astype(BF16), preferred_element_type=F32)
    bstart = bend - nb
    first_row = bstart[:, :1] * EXPERT_BLOCK

    tl = mi_ref.shape[1]
    iota_e = lax.broadcasted_iota(I32, (N_EXPERTS, tl), 0)
    for k in range(TOP_K):
        hit = iota_e == mi_ref[k:k + 1, :]
        base = jnp.sum(jnp.where(hit, first_row, 0.0), axis=0, keepdims=True)
        dest_ref[k:k + 1, :] = base.astype(I32) + mi_ref[TOP_K + k:TOP_K + k + 1, :]

    nl = tab_ref.shape[1]
    bidx = lax.broadcasted_iota(I32, (1, nl), 1).astype(F32)
    blk_e = jnp.sum(jnp.where(bend[:, :1] <= bidx, 1.0, 0.0), axis=0, keepdims=True)
    blk_e = jnp.minimum(blk_e, N_EXPERTS - 1.0)
    hit = lax.broadcasted_iota(I32, (N_EXPERTS, nl), 0) == blk_e.astype(I32)
    cnt_b = jnp.sum(jnp.where(hit, cnt[:, :1], 0.0), axis=0, keepdims=True)
    start_b = jnp.sum(jnp.where(hit, bstart[:, :1], 0.0), axis=0, keepdims=True)
    valid = jnp.clip(cnt_b - (bidx - start_b) * EXPERT_BLOCK, 0.0, float(EXPERT_BLOCK))
    n_used = jnp.broadcast_to(jnp.max(bend[:, :1], axis=0, keepdims=True), (1, nl))
    run_end = jnp.sum(jnp.where(hit, bend[:, :1], 0.0), axis=0, keepdims=True)
    zero = jnp.zeros((1, nl), F32)
    tab_ref[...] = jnp.concatenate([blk_e, valid, n_used, run_end, zero, zero, zero, zero],
                                   axis=0).astype(I32)


def _plan(meta_i, cnt, nblk):
    t = meta_i.shape[1]
    tl = _pick(t, (PLAN_LANES, 1024, 512, 256, 128))
    nl = pl.cdiv(nblk, LANES) * LANES
    return pl.pallas_call(
        _plan_kernel,
        out_shape=(jax.ShapeDtypeStruct((TOP_K, t), I32), jax.ShapeDtypeStruct((8, nl), I32)),
        grid=(t // tl,),
        in_specs=[pl.BlockSpec((8, tl), lambda i: (0, i)),
                  pl.BlockSpec((N_EXPERTS, LANES), lambda i: (0, 0))],
        out_specs=(pl.BlockSpec((TOP_K, tl), lambda i: (0, i)),
                   pl.BlockSpec((8, nl), lambda i: (0, 0))),
        compiler_params=_params(("arbitrary",), 32),
    )(meta_i, cnt)


def _dispatch_kernel(dest_ref, hp_ref, xs_ref, sem, *, chunk, t):
    base = pl.program_id(0) * chunk

    def issue(i, carry):
        for k in range(TOP_K):
            pltpu.make_async_copy(hp_ref.at[pl.ds(i, 1)],
                                  xs_ref.at[pl.ds(dest_ref[k * t + base + i], 1)],
                                  sem).start(priority=k)
        return carry

    lax.fori_loop(0, chunk, issue, 0, unroll=DMA_UNROLL)
    for k in range(TOP_K):
        pltpu.make_async_copy(hp_ref, xs_ref.at[pl.ds(0, chunk)], sem).wait()


def _dispatch(dest_flat, hp, cap):
    t, half = hp.shape
    chunk = _pick(t, (512, 256, 128))
    grid_spec = pltpu.PrefetchScalarGridSpec(
        num_scalar_prefetch=1,
        grid=(t // chunk,),
        in_specs=[pl.BlockSpec((chunk, half), lambda i, ds: (i, 0))],
        out_specs=pl.BlockSpec(memory_space=pl.ANY),
        scratch_shapes=[pltpu.SemaphoreType.DMA(())],
    )
    return pl.pallas_call(
        functools.partial(_dispatch_kernel, chunk=chunk, t=t),
        out_shape=jax.ShapeDtypeStruct((cap, half), U32),
        grid_spec=grid_spec,
        compiler_params=pltpu.CompilerParams(dimension_semantics=("arbitrary",),
                                             has_side_effects=True, vmem_limit_bytes=32 * MIB),
    )(dest_flat, hp)


def _ffn_kernel(be_ref, nv_ref, nu_ref, re_ref, xs_ref, wg_hbm, wu_hbm, wd_hbm, o_ref,
                wg_s, wu_s, wd_s, slot_ref, sem):
    b = pl.program_id(0)

    def weight_copies(e, slot):
        return (pltpu.make_async_copy(wg_hbm.at[e], wg_s.at[slot], sem.at[slot]),
                pltpu.make_async_copy(wu_hbm.at[e], wu_s.at[slot], sem.at[slot]),
                pltpu.make_async_copy(wd_hbm.at[e], wd_s.at[slot], sem.at[slot]))

    @pl.when(b < nu_ref[0])
    def _():
        e = be_ref[b]

        @pl.when(b == 0)
        def _():
            slot_ref[0] = 0
            for c in weight_copies(e, 0):
                c.start()

        first = jnp.logical_or(b == 0, be_ref[jnp.maximum(b - 1, 0)] != e)

        @pl.when(first)
        def _():
            slot = jnp.where(b == 0, 0, 1 - slot_ref[0])
            slot_ref[0] = slot
            for c in weight_copies(e, slot):
                c.wait()
            nxt = re_ref[b]

            @pl.when(nxt < nu_ref[0])
            def _():
                for c in weight_copies(be_ref[nxt], 1 - slot):
                    c.start()

        slot = slot_ref[0]
        blk, half = xs_ref.shape
        row = lax.broadcasted_iota(I32, (blk, 1), 0)
        xp = jnp.where(row < nv_ref[b], xs_ref[...], jnp.uint32(0))
        lo, hi = _unpack_halves(xp)
        lo = lo.astype(BF16)
        hi = hi.astype(BF16)
        a = (jnp.dot(lo, wg_s[slot, :half], preferred_element_type=F32)
             + jnp.dot(hi, wg_s[slot, half:], preferred_element_type=F32))
        u = (jnp.dot(lo, wu_s[slot, :half], preferred_element_type=F32)
             + jnp.dot(hi, wu_s[slot, half:], preferred_element_type=F32))
        mid = (a * jax.nn.sigmoid(a) * u).astype(BF16)
        o_ref[...] = _pack_halves(jnp.dot(mid, wd_s[slot], preferred_element_type=F32))


def _experts(blk_e, blk_valid, n_used, run_end, xs, wg, wu, wd):
    cap, half = xs.shape
    d = 2 * half
    ff = wg.shape[2]
    nblk = cap // EXPERT_BLOCK

    def row_map(b, be, nv, nu, re):
        return (jnp.minimum(b, nu[0] - 1), 0)

    grid_spec = pltpu.PrefetchScalarGridSpec(
        num_scalar_prefetch=4,
        grid=(nblk,),
        in_specs=[pl.BlockSpec((EXPERT_BLOCK, half), row_map),
                  pl.BlockSpec(memory_space=pl.ANY),
                  pl.BlockSpec(memory_space=pl.ANY),
                  pl.BlockSpec(memory_space=pl.ANY)],
        out_specs=pl.BlockSpec((EXPERT_BLOCK, half), row_map),
        scratch_shapes=[pltpu.VMEM((2, d, ff), BF16), pltpu.VMEM((2, d, ff), BF16),
                        pltpu.VMEM((2, ff, d), BF16), pltpu.SMEM((1,), I32),
                        pltpu.SemaphoreType.DMA((2,))],
    )
    return pl.pallas_call(
        _ffn_kernel,
        out_shape=jax.ShapeDtypeStruct((cap, half), U32),
        grid_spec=grid_spec,
        compiler_params=_params(("arbitrary",), 56),
    )(blk_e, blk_valid, n_used, run_end, xs, wg, wu, wd)


def _combine_kernel(dest_ref, x_ref, g_ref, w_ref, ys_ref, o_ref, ybuf, sem, *, chunk, t):
    i = pl.program_id(0)
    n = pl.num_programs(0)

    def issue(step, slot):
        base = step * chunk

        def body(r, carry):
            for k in range(TOP_K):
                pltpu.make_async_copy(ys_ref.at[pl.ds(dest_ref[k * t + base + r], 1)],
                                      ybuf.at[slot, k, pl.ds(r, 1)],
                                      sem.at[slot]).start(priority=k)
            return carry

        lax.fori_loop(0, chunk, body, 0, unroll=DMA_UNROLL)

    @pl.when(i == 0)
    def _():
        issue(0, 0)

    slot = i % 2

    @pl.when(i + 1 < n)
    def _():
        issue(i + 1, 1 - slot)

    for k in range(TOP_K):
        pltpu.make_async_copy(ys_ref.at[pl.ds(0, chunk)], ybuf.at[slot, k], sem.at[slot]).wait()
    w = w_ref[...]
    half = ybuf.shape[-1]
    lo0, hi0 = _unpack_halves(ybuf[slot, 0])
    lo1, hi1 = _unpack_halves(ybuf[slot, 1])
    g = g_ref[0]
    o_ref[:, :half] = x_ref[:, :half] + g[:, :half] * (w[:, 0:1] * lo0 + w[:, 1:2] * lo1)
    o_ref[:, half:] = x_ref[:, half:] + g[:, half:] * (w[:, 0:1] * hi0 + w[:, 1:2] * hi1)


def _combine(dest_flat, x1, gate, w_rows, ys, seq):
    t, d = x1.shape
    chunk = _pick(seq, (256, 128))
    rows_per_batch = seq // chunk
    grid_spec = pltpu.PrefetchScalarGridSpec(
        num_scalar_prefetch=1,
        grid=(t // chunk,),
        in_specs=[pl.BlockSpec((chunk, d), lambda i, ds: (i, 0)),
                  pl.BlockSpec((1, 1, d), lambda i, ds: (i // rows_per_batch, 0, 0)),
                  pl.BlockSpec((chunk, 8), lambda i, ds: (i, 0)),
                  pl.BlockSpec(memory_space=pl.ANY)],
        out_specs=pl.BlockSpec((chunk, d), lambda i, ds: (i, 0)),
        scratch_shapes=[pltpu.VMEM((2, TOP_K, chunk, d // 2), U32),
                        pltpu.SemaphoreType.DMA((2,))],
    )
    return pl.pallas_call(
        functools.partial(_combine_kernel, chunk=chunk, t=t),
        out_shape=jax.ShapeDtypeStruct((t, d), F32),
        grid_spec=grid_spec,
        compiler_params=_params(("arbitrary",), 56),
    )(dest_flat, x1, gate, w_rows, ys)


def _layer(x, c, w_ada, b_ada, norm1_w, w_in, q_norm_w, k_norm_w, sinks, w_pool, pool_scale,
           w_attn_up, w_pool_up, w_out, norm2_w, w_router_group, b_router_group,
           w_router_expert, b_router_expert, w_gate, w_up, w_down):
    b, s, d = x.shape
    t = b * s
    aw = w_attn_up.shape[0]
    pw = w_pool_up.shape[0]
    kvw = N_KV_HEADS * HEAD_DIM
    x2 = x.reshape(t, d)

    c8 = jnp.zeros((8, d), F32).at[:b].set(c)
    ada = _ada(c8, w_ada, b_ada.reshape(1, 6 * d))[:b]
    shift1, scale1, gate1, shift2, scale2, gate2 = [a.reshape(b, 1, d) for a in jnp.split(ada, 6, axis=-1)]

    q_col, kv_col = 0, aw
    p_col = aw + 2 * kvw
    ga_col = p_col + pw
    gb_col = ga_col + d
    h = _norm(x2, norm1_w.reshape(1, d), scale1, shift1, s)
    ne, _, ff = w_gate.shape
    proj, (wg_b, wu_b, wd_b, wa_b, wb_b, wo_b) = _in_proj(
        h, w_in.astype(BF16),
        [w_gate.reshape(ne * d, ff), w_up.reshape(ne * d, ff), w_down.reshape(ne * ff, d),
         w_attn_up, w_pool_up, w_out])
    wg_b = wg_b.reshape(ne, d, ff)
    wu_b = wu_b.reshape(ne, d, ff)
    wd_b = wd_b.reshape(ne, ff, d)
    proj3 = proj.reshape(b, s, proj.shape[1])

    y_attn = _attention(proj3, sinks, q_norm_w, k_norm_w, q_col, kv_col).reshape(t, aw)
    y_pool = _pool(proj3, w_pool.astype(BF16), pool_scale.reshape(1, pw), p_col).reshape(t, pw)
    mixed = _mix(y_attn, y_pool, wa_b, wb_b, proj, ga_col, gb_col)
    x1 = _out_proj(mixed, wo_b, x2, gate1, s)

    wr = jnp.concatenate([w_router_group, w_router_expert,
                          jnp.zeros((d, LANES - N_ROUTER), F32)], axis=1)
    wr_hi = wr.astype(BF16)
    wr_lo = (wr - wr_hi.astype(F32)).astype(BF16)
    wr2 = jnp.concatenate([wr_hi, wr_lo], axis=1)
    br = jnp.concatenate([b_router_group, b_router_expert]).reshape(N_ROUTER, 1)
    hp, meta_i, meta_f, cnt = _router(x1, norm2_w.reshape(1, d), scale2, shift2, wr2, br, s)

    cap = TOP_K * t + N_EXPERTS * EXPERT_BLOCK
    nblk = cap // EXPERT_BLOCK
    dest, tab = _plan(meta_i, cnt, nblk)
    dest_flat = dest.reshape(TOP_K * t)
    blk_e, blk_valid, n_used, run_end = tab[0, :nblk], tab[1, :nblk], tab[2, :1], tab[3, :nblk]

    xs = _dispatch(dest_flat, hp, cap)
    ys = _experts(blk_e, blk_valid, n_used, run_end, xs, wg_b, wu_b, wd_b)
    out = _combine(dest_flat, x1, gate2, meta_f.T, ys, s)
    return out.reshape(b, s, d)


def kernel(x, c, w_ada, b_ada, norm1_w, w_in, q_norm_w, k_norm_w, sinks, w_pool, pool_scale,
           w_attn_up, w_pool_up, w_out, norm2_w, w_router_group, b_router_group,
           w_router_expert, b_router_expert, w_gate, w_up, w_down):
    for l in range(w_ada.shape[0]):
        x = _layer(x, c, w_ada[l], b_ada[l], norm1_w[l], w_in[l], q_norm_w[l], k_norm_w[l],
                   sinks[l], w_pool[l], pool_scale[l], w_attn_up[l], w_pool_up[l], w_out[l],
                   norm2_w[l], w_router_group[l], b_router_group[l], w_router_expert[l],
                   b_router_expert[l], w_gate[l], w_up[l], w_down[l])
    return x
```

```python
import functools

import jax
import jax.numpy as jnp
import numpy as np
from jax import lax
from jax.experimental import pallas as pl
from jax.experimental.pallas import tpu as pltpu

F32 = jnp.float32
BF16 = jnp.bfloat16
U32 = jnp.uint32
I32 = jnp.int32

HEAD_DIM = 64
N_Q_HEADS = 32
N_KV_HEADS = 4
Q_PER_KV = N_Q_HEADS // N_KV_HEADS
ATTN_BLOCK = 128
WINDOW = 128
POOL_WINDOWS = (2, 4, 8, 16)
POOL_HALO = 16
N_GROUPS = 8
EXPERTS_PER_GROUP = 8
N_EXPERTS = N_GROUPS * EXPERTS_PER_GROUP
N_ROUTER = N_GROUPS + N_EXPERTS
TOP_K = 2
EPS = 1e-6
NEG_INF = -1e30
LOG2E = 1.4426950408889634
ATTN_STAGE_PAIRS = 8

LANES = 128
EXPERT_BLOCK = 256
MIB = 1024 * 1024


def _params(semantics, vmem_mib):
    return pltpu.CompilerParams(dimension_semantics=semantics, vmem_limit_bytes=vmem_mib * MIB)


def _pick(n, candidates):
    for c in candidates:
        if n % c == 0:
            return c
    return n


def _ada_kernel(c_ref, w_ref, b_ref, o_ref):
    c = c_ref[...]
    s = (c * jax.nn.sigmoid(c)).astype(BF16)
    o_ref[...] = jnp.dot(s, w_ref[...].astype(BF16), preferred_element_type=F32) + b_ref[...]


def _ada(c8, w, b):
    d, n = w.shape
    tn = _pick(n, (1024, 512, 256, 128))
    return pl.pallas_call(
        _ada_kernel,
        out_shape=jax.ShapeDtypeStruct((8, n), F32),
        grid=(n // tn,),
        in_specs=[pl.BlockSpec((8, d), lambda j: (0, 0)),
                  pl.BlockSpec((d, tn), lambda j: (0, j)),
                  pl.BlockSpec((1, tn), lambda j: (0, j))],
        out_specs=pl.BlockSpec((8, tn), lambda j: (0, j)),
        compiler_params=_params(("arbitrary",), 56),
    )(c8, w, b)


def _modulated_norm(x, gain, shift):
    ms = jnp.mean(x * x, axis=-1, keepdims=True)
    return x * lax.rsqrt(ms + EPS) * gain + shift


NORM_ROWS = 16
NORM_UNROLL = 4


def _pack_halves(v):
    n = v.shape[1] // 2
    bits = lax.bitcast_convert_type(v.astype(BF16).astype(F32), U32)
    return (bits[:, :n] >> 16) | (bits[:, n:] & jnp.uint32(0xFFFF0000))


def _unpack_halves(p):
    lo = lax.bitcast_convert_type(p << 16, F32)
    hi = lax.bitcast_convert_type(p & jnp.uint32(0xFFFF0000), F32)
    return lo, hi


def _norm_kernel(x_ref, nw_ref, sc_ref, sh_ref, o_ref):
    gain = nw_ref[...] * (1.0 + sc_ref[0])
    shift = sh_ref[0]

    def body(r, carry):
        rows = pl.ds(pl.multiple_of(r * NORM_ROWS, NORM_ROWS), NORM_ROWS)
        o_ref[rows, :] = _modulated_norm(x_ref[rows, :], gain, shift).astype(o_ref.dtype)
        return carry

    lax.fori_loop(0, x_ref.shape[0] // NORM_ROWS, body, 0, unroll=NORM_UNROLL)


def _norm(x2, nw, scale, shift, seq):
    t, d = x2.shape
    tm = _pick(seq, (512, 256, 128))
    rows_per_batch = seq // tm
    return pl.pallas_call(
        _norm_kernel,
        out_shape=jax.ShapeDtypeStruct((t, d), BF16),
        grid=(t // tm,),
        in_specs=[pl.BlockSpec((tm, d), lambda i: (i, 0)),
                  pl.BlockSpec((1, d), lambda i: (0, 0)),
                  pl.BlockSpec((1, 1, d), lambda i: (i // rows_per_batch, 0, 0)),
                  pl.BlockSpec((1, 1, d), lambda i: (i // rows_per_batch, 0, 0))],
        out_specs=pl.BlockSpec((tm, d), lambda i: (i, 0)),
        compiler_params=_params(("parallel",), 40),
    )(x2, nw, scale, shift)


def _in_kernel(a_ref, w_ref, *refs):
    n_riders = (len(refs) - 1) // 2
    o_ref = refs[n_riders]
    o_ref[...] = jnp.dot(a_ref[...], w_ref[...], preferred_element_type=F32).astype(o_ref.dtype)
    for src, dst in zip(refs[:n_riders], refs[n_riders + 1:]):
        dst[...] = src[...].astype(BF16)


def _rider_steps(nrows, steps):
    n = 1 << (steps.bit_length() - 1)
    while nrows % n or (nrows // n) % 16:
        n //= 2
    return n


def _in_proj(h, w, riders):
    t, d = h.shape
    n = w.shape[1]
    tm = _pick(t, (1024, 512, 256, 128))
    tn = _pick(n, (512, 256, 128))
    nj = n // tn
    steps = (t // tm) * nj

    def rider_spec(arr):
        nsteps = _rider_steps(arr.shape[0], steps)
        return pl.BlockSpec((arr.shape[0] // nsteps, arr.shape[1]),
                            lambda i, j: (jnp.minimum(i * nj + j, nsteps - 1), 0))

    rider_specs = [rider_spec(r) for r in riders]
    outs = pl.pallas_call(
        _in_kernel,
        out_shape=tuple([jax.ShapeDtypeStruct((t, n), BF16)]
                        + [jax.ShapeDtypeStruct(r.shape, BF16) for r in riders]),
        grid=(t // tm, nj),
        in_specs=[pl.BlockSpec((tm, d), lambda i, j: (i, 0)),
                  pl.BlockSpec((d, tn), lambda i, j: (0, j))] + rider_specs,
        out_specs=tuple([pl.BlockSpec((tm, tn), lambda i, j: (i, j))] + rider_specs),
        compiler_params=_params(("arbitrary", "arbitrary"), 56),
    )(h, w, *riders)
    return outs[0], outs[1:]


def _head_sumsq(v, ones_bd):
    sq = v * v
    hi = sq.astype(BF16)
    lo = (sq - hi.astype(F32)).astype(BF16)
    return (jnp.dot(hi, ones_bd, preferred_element_type=F32)
            + jnp.dot(lo, ones_bd, preferred_element_type=F32))


def _head_rms_norm(v, w2, ones_bd):
    ss = _head_sumsq(v, ones_bd)
    return v * lax.rsqrt(ss * (1.0 / HEAD_DIM) + EPS) * w2


def _attn_kernel(sink_ref, q_ref, kvc_ref, kvp_ref, bias_ref, qw_ref, kw_ref, wg_ref, o_ref, wg_o):
    wg_o[...] = wg_ref[...].astype(BF16)

    blk = ATTN_BLOCK
    kvw = N_KV_HEADS * HEAD_DIM
    lane = lax.broadcasted_iota(I32, (1, LANES), 1)
    low_half = lane < HEAD_DIM
    r = lax.broadcasted_iota(I32, (LANES, LANES), 0) // HEAD_DIM
    c = lax.broadcasted_iota(I32, (LANES, LANES), 1) // HEAD_DIM
    ones_bd = jnp.where(r == c, 1.0, 0.0).astype(BF16)

    kv = jnp.concatenate([kvp_ref[0], kvc_ref[0]], axis=0).astype(F32)
    kw2 = kw_ref[...]
    sum_a = jnp.broadcast_to(jnp.where(low_half, 1.0, 0.0), (2 * blk, LANES))
    sum_b = jnp.broadcast_to(jnp.where(low_half, 0.0, 1.0), (2 * blk, LANES))

    k_bds, v_bds = [], []
    for g in range(N_KV_HEADS):
        chunk = g // 2
        kc = _head_rms_norm(kv[:, chunk * LANES:(chunk + 1) * LANES], kw2, ones_bd)
        vc = kv[:, kvw + chunk * LANES: kvw + (chunk + 1) * LANES]
        kr = pltpu.roll(kc, HEAD_DIM, axis=1)
        vr = pltpu.roll(vc, HEAD_DIM, axis=1)
        if g % 2 == 0:
            k_lo, k_hi, v_lo, v_hi = kc, kr, vc, vr
        else:
            k_lo, k_hi, v_lo, v_hi = kr, kc, vr, vc
        k_bds.append(jnp.concatenate([jnp.where(low_half, k_lo, 0.0),
                                      jnp.where(low_half, 0.0, k_hi)], axis=0).astype(BF16))
        v_bds.append(jnp.concatenate(
            [jnp.concatenate([jnp.where(low_half, v_lo, 0.0), sum_a], axis=1),
             jnp.concatenate([jnp.where(low_half, 0.0, v_hi), sum_b], axis=1)],
            axis=0).astype(BF16))

    pairs_per_kv = Q_PER_KV // 2
    for first in range(0, N_Q_HEADS // 2, ATTN_STAGE_PAIRS):
        pairs = range(first, first + ATTN_STAGE_PAIRS)
        scores = []
        for j in pairs:
            qp = q_ref[0, :, j * LANES:(j + 1) * LANES].astype(F32)
            qn = _head_rms_norm(qp, qw_ref[...], ones_bd).astype(BF16)
            s = lax.dot_general(qn, k_bds[j // pairs_per_kv], (((1,), (1,)), ((), ())),
                                preferred_element_type=F32)
            scores.append(s + bias_ref[0, j])
        probs = []
        for j, s in zip(pairs, scores):
            sink_a = sink_ref[2 * j]
            sink_b = sink_ref[2 * j + 1]
            m_a = jnp.maximum(jnp.max(s[:, :2 * blk], axis=-1, keepdims=True), sink_a)
            m_b = jnp.maximum(jnp.max(s[:, 2 * blk:], axis=-1, keepdims=True), sink_b)
            p = jnp.concatenate([jnp.exp2(s[:, :2 * blk] - m_a),
                                 jnp.exp2(s[:, 2 * blk:] - m_b)], axis=1).astype(BF16)
            esink = jnp.where(low_half, jnp.exp2(sink_a - m_a), jnp.exp2(sink_b - m_b))
            probs.append((p, esink))
        for j, (p, esink) in zip(pairs, probs):
            ol = jnp.dot(p, v_bds[j // pairs_per_kv], preferred_element_type=F32)
            o = ol[:, :LANES] / (ol[:, LANES:] + esink)
            o_ref[0, :, j * LANES:(j + 1) * LANES] = o.astype(o_ref.dtype)


def _attn_bias():
    blk = ATTN_BLOCK
    i = np.arange(blk)[:, None]
    j = np.arange(2 * blk)[None, :]
    dist = i + blk - j
    band = (dist >= 0) & (dist < WINDOW)
    slopes = (2.0 ** (-8.0 * np.arange(1, N_Q_HEADS + 1, dtype=np.float32) / N_Q_HEADS)).astype(np.float32)
    alibi = -slopes[:, None, None] * dist.astype(np.float32)[None] * np.float32(LOG2E)
    later = np.where(band[None], alibi, np.float32(NEG_INF))
    first = np.where((band & (j >= blk))[None], alibi, np.float32(NEG_INF))
    tab = np.stack([first, later]).astype(np.float32)
    tab = tab.reshape(2, N_Q_HEADS // 2, 2, blk, 2 * blk)
    return np.transpose(tab, (0, 1, 3, 2, 4)).reshape(2, N_Q_HEADS // 2, blk, 4 * blk)


def _attention(proj3, sinks, q_norm_w, k_norm_w, q_col, kv_col, rider):
    b, s, _ = proj3.shape
    aw = N_Q_HEADS * HEAD_DIM
    kvw2 = 2 * N_KV_HEADS * HEAD_DIM
    nb = s // ATTN_BLOCK
    bias = _attn_bias()
    qw2 = (jnp.tile(q_norm_w, 2) * (HEAD_DIM ** -0.5 * LOG2E)).reshape(1, LANES)
    sinks = sinks * LOG2E
    kw2 = jnp.tile(k_norm_w, 2).reshape(1, LANES)
    rider_spec = pl.BlockSpec((rider.shape[0] // (b * nb), rider.shape[1]),
                              lambda bi, n, sk: (bi * nb + n, 0))
    grid_spec = pltpu.PrefetchScalarGridSpec(
        num_scalar_prefetch=1,
        grid=(b, nb),
        in_specs=[pl.BlockSpec((1, ATTN_BLOCK, aw), lambda bi, n, sk: (bi, n, q_col // aw)),
                  pl.BlockSpec((1, ATTN_BLOCK, kvw2), lambda bi, n, sk: (bi, n, kv_col // kvw2)),
                  pl.BlockSpec((1, ATTN_BLOCK, kvw2),
                               lambda bi, n, sk: (bi, jnp.maximum(n - 1, 0), kv_col // kvw2)),
                  pl.BlockSpec((1, N_Q_HEADS // 2, ATTN_BLOCK, 4 * ATTN_BLOCK),
                               lambda bi, n, sk: (jnp.minimum(n, 1), 0, 0, 0)),
                  pl.BlockSpec((1, LANES), lambda bi, n, sk: (0, 0)),
                  pl.BlockSpec((1, LANES), lambda bi, n, sk: (0, 0)),
                  rider_spec],
        out_specs=(pl.BlockSpec((1, ATTN_BLOCK, aw), lambda bi, n, sk: (bi, n, 0)), rider_spec),
    )
    return pl.pallas_call(
        _attn_kernel,
        out_shape=(jax.ShapeDtypeStruct((b, s, aw), BF16), jax.ShapeDtypeStruct(rider.shape, BF16)),
        grid_spec=grid_spec,
        compiler_params=_params(("arbitrary", "arbitrary"), 56),
    )(sinks, proj3, proj3, proj3, bias, qw2, kw2, rider)


def _pool_kernel(pc_ref, ph_ref, w_ref, ps_ref, o_ref):
    n = pl.program_id(1)
    ts = pc_ref.shape[1]
    cg = w_ref.shape[1]
    cur = pc_ref[0].astype(F32)
    halo = jnp.where(n > 0, ph_ref[0].astype(F32), 0.0)
    ext = jnp.concatenate([halo, cur], axis=0)
    t1 = (lax.broadcasted_iota(I32, (ts, 1), 0) + n * ts + 1).astype(F32)
    for g, win in enumerate(POOL_WINDOWS):
        acc = ext[:, g * cg:(g + 1) * cg]
        k = 1
        while k < win:
            acc = acc + pltpu.roll(acc, k, axis=0)
            k *= 2
        mean = acc[POOL_HALO:] / jnp.minimum(t1, float(win))
        d = (mean - cur[:, g * cg:(g + 1) * cg]).astype(BF16)
        y = jnp.dot(d, w_ref[g], preferred_element_type=F32) * ps_ref[:, g * cg:(g + 1) * cg]
        o_ref[0, :, g * cg:(g + 1) * cg] = y.astype(o_ref.dtype)


def _pool(proj3, w_pool, pool_scale, p_col):
    b, s, _ = proj3.shape
    ng, cg, _ = w_pool.shape
    pw = ng * cg
    ts = _pick(s, (512, 256, 128))
    return pl.pallas_call(
        _pool_kernel,
        out_shape=jax.ShapeDtypeStruct((b, s, pw), BF16),
        grid=(b, s // ts),
        in_specs=[pl.BlockSpec((pl.Element(1), pl.Element(ts), pl.Element(pw)),
                               lambda bi, n: (bi, pl.multiple_of(n * ts, ts), p_col)),
                  pl.BlockSpec((pl.Element(1), pl.Element(POOL_HALO), pl.Element(pw)),
                               lambda bi, n: (bi, pl.multiple_of(jnp.maximum(n * ts - POOL_HALO, 0),
                                                                 POOL_HALO), p_col)),
                  pl.BlockSpec((ng, cg, cg), lambda bi, n: (0, 0, 0)),
                  pl.BlockSpec((1, pw), lambda bi, n: (0, 0))],
        out_specs=pl.BlockSpec((1, ts, pw), lambda bi, n: (bi, n, 0)),
        compiler_params=_params(("parallel", "arbitrary"), 40),
    )(proj3, proj3, w_pool, pool_scale)


def _mix_kernel(a_ref, b_ref, wa_ref, wb_ref, ga_ref, gb_ref, o_ref):
    ya = jnp.dot(a_ref[...], wa_ref[...], preferred_element_type=F32)
    yb = jnp.dot(b_ref[...], wb_ref[...], preferred_element_type=F32)
    ga = jax.nn.sigmoid(ga_ref[...].astype(F32))
    gb = jax.nn.sigmoid(gb_ref[...].astype(F32))
    o_ref[...] = (ga * ya + gb * yb).astype(o_ref.dtype)


def _mix(y_attn, y_pool, wa, wb, proj, ga_col, gb_col):
    t, aw = y_attn.shape
    pw = y_pool.shape[1]
    d = wa.shape[1]
    tm = _pick(t, (1024, 512, 256, 128))
    tn = _pick(d, (512, 256, 128))
    return pl.pallas_call(
        _mix_kernel,
        out_shape=jax.ShapeDtypeStruct((t, d), BF16),
        grid=(t // tm, d // tn),
        in_specs=[pl.BlockSpec((tm, aw), lambda i, j: (i, 0)),
                  pl.BlockSpec((tm, pw), lambda i, j: (i, 0)),
                  pl.BlockSpec((aw, tn), lambda i, j: (0, j)),
                  pl.BlockSpec((pw, tn), lambda i, j: (0, j)),
                  pl.BlockSpec((pl.Element(tm), pl.Element(tn)),
                               lambda i, j: (pl.multiple_of(i * tm, tm),
                                             pl.multiple_of(ga_col + j * tn, LANES))),
                  pl.BlockSpec((pl.Element(tm), pl.Element(tn)),
                               lambda i, j: (pl.multiple_of(i * tm, tm),
                                             pl.multiple_of(gb_col + j * tn, LANES)))],
        out_specs=pl.BlockSpec((tm, tn), lambda i, j: (i, j)),
        compiler_params=_params(("parallel", "arbitrary"), 56),
    )(y_attn, y_pool, wa, wb, proj, proj)


def _out_kernel(m_ref, w_ref, x_ref, g_ref, o_ref):
    y = jnp.dot(m_ref[...], w_ref[...], preferred_element_type=F32)
    o_ref[...] = x_ref[...] + g_ref[0] * y


def _out_proj(mixed, w, x2, gate, seq):
    t, d = x2.shape
    tm = _pick(seq, (1024, 512, 256, 128))
    tn = _pick(d, (1024, 512, 256, 128))
    rows_per_batch = seq // tm
    return pl.pallas_call(
        _out_kernel,
        out_shape=jax.ShapeDtypeStruct((t, d), F32),
        grid=(t // tm, d // tn),
        in_specs=[pl.BlockSpec((tm, d), lambda i, j: (i, 0)),
                  pl.BlockSpec((d, tn), lambda i, j: (0, j)),
                  pl.BlockSpec((tm, tn), lambda i, j: (i, j)),
                  pl.BlockSpec((1, 1, tn), lambda i, j: (i // rows_per_batch, 0, j))],
        out_specs=pl.BlockSpec((tm, tn), lambda i, j: (i, j)),
        compiler_params=_params(("parallel", "arbitrary"), 56),
    )(mixed, w, x2, gate)


def _first_argmax(v, iota, n):
    m = jnp.max(v, axis=0, keepdims=True)
    idx = jnp.min(jnp.where(v == m, iota, n), axis=0, keepdims=True)
    return m, idx


def _router_kernel(x_ref, nw_ref, sc_ref, sh_ref, w2_ref, br_ref,
                   hp_ref, mi_ref, mf_ref, cnt_ref, carry_ref, hb_ref):
    step = pl.program_id(0)
    tm, d = x_ref.shape

    @pl.when(step == 0)
    def _():
        carry_ref[...] = jnp.zeros_like(carry_ref)

    gain = nw_ref[...] * (1.0 + sc_ref[0])
    shift = sh_ref[0]

    def body(r, carry):
        rows = pl.ds(pl.multiple_of(r * NORM_ROWS, NORM_ROWS), NORM_ROWS)
        h = _modulated_norm(x_ref[rows, :], gain, shift)
        hb_ref[rows, :] = h.astype(BF16)
        hp_ref[rows, :] = _pack_halves(h)
        return carry

    lax.fori_loop(0, tm // NORM_ROWS, body, 0, unroll=NORM_UNROLL)

    l2 = jnp.dot(hb_ref[...], w2_ref[...], preferred_element_type=F32)
    lt = l2[:, :LANES] + l2[:, LANES:]
    logits = lt.T[:N_ROUTER] + br_ref[...]
    iota8 = lax.broadcasted_iota(I32, (N_GROUPS, tm), 0)
    gl = logits[:N_GROUPS]
    g_max, g_sel = _first_argmax(gl, iota8, N_GROUPS)
    p_group = 1.0 / jnp.sum(jnp.exp(gl - g_max), axis=0, keepdims=True)
    e_in = jnp.zeros((EXPERTS_PER_GROUP, tm), F32)
    for g in range(N_GROUPS):
        lo = N_GROUPS + g * EXPERTS_PER_GROUP
        e_in = e_in + jnp.where(g_sel == g, logits[lo:lo + EXPERTS_PER_GROUP], 0.0)
    v0, i0 = _first_argmax(e_in, iota8, EXPERTS_PER_GROUP)
    rest = jnp.where(iota8 == i0, -jnp.inf, e_in)
    v1, i1 = _first_argmax(rest, iota8, EXPERTS_PER_GROUP)
    t = jnp.exp(v1 - v0)
    w0 = p_group / (1.0 + t)
    w1 = p_group * t / (1.0 + t)
    e0 = g_sel * EXPERTS_PER_GROUP + i0
    e1 = g_sel * EXPERTS_PER_GROUP + i1

    iota_e = lax.broadcasted_iota(I32, (N_EXPERTS, tm), 0)
    hit0 = iota_e == e0
    hit1 = iota_e == e1
    onehot = jnp.where(hit0 | hit1, 1.0, 0.0).astype(BF16)
    rr = lax.broadcasted_iota(I32, (tm, tm), 0)
    cc = lax.broadcasted_iota(I32, (tm, tm), 1)
    before = jnp.where(rr < cc, 1.0, 0.0).astype(BF16)
    prior = jnp.dot(onehot, before, preferred_element_type=F32) + carry_ref[:, :1]
    r0 = jnp.sum(jnp.where(hit0, prior, 0.0), axis=0, keepdims=True)
    r1 = jnp.sum(jnp.where(hit1, prior, 0.0), axis=0, keepdims=True)
    carry_ref[...] += jnp.dot(onehot, jnp.ones((tm, LANES), BF16), preferred_element_type=F32)

    zi = jnp.zeros((1, tm), I32)
    zf = jnp.zeros((1, tm), F32)
    mi_ref[...] = jnp.concatenate([e0, e1, r0.astype(I32), r1.astype(I32), zi, zi, zi, zi], axis=0)
    mf_ref[...] = jnp.concatenate([w0, w1, zf, zf, zf, zf, zf, zf], axis=0)
    cnt_ref[...] = carry_ref[...]


def _router(x1, nw, scale, shift, w2, br, seq):
    t, d = x1.shape
    tm = _pick(seq, (512, 256, 128))
    rows_per_batch = seq // tm
    return pl.pallas_call(
        _router_kernel,
        out_shape=(jax.ShapeDtypeStruct((t, d // 2), U32),
                   jax.ShapeDtypeStruct((8, t), I32),
                   jax.ShapeDtypeStruct((8, t), F32),
                   jax.ShapeDtypeStruct((N_EXPERTS, LANES), F32)),
        grid=(t // tm,),
        in_specs=[pl.BlockSpec((tm, d), lambda i: (i, 0)),
                  pl.BlockSpec((1, d), lambda i: (0, 0)),
                  pl.BlockSpec((1, 1, d), lambda i: (i // rows_per_batch, 0, 0)),
                  pl.BlockSpec((1, 1, d), lambda i: (i // rows_per_batch, 0, 0)),
                  pl.BlockSpec((d, 2 * LANES), lambda i: (0, 0)),
                  pl.BlockSpec((N_ROUTER, 1), lambda i: (0, 0))],
        out_specs=(pl.BlockSpec((tm, d // 2), lambda i: (i, 0)),
                   pl.BlockSpec((8, tm), lambda i: (0, i)),
                   pl.BlockSpec((8, tm), lambda i: (0, i)),
                   pl.BlockSpec((N_EXPERTS, LANES), lambda i: (0, 0))),
        scratch_shapes=[pltpu.VMEM((N_EXPERTS, LANES), F32), pltpu.VMEM((tm, d), BF16)],
        compiler_params=_params(("arbitrary",), 56),
    )(x1, nw, scale, shift, w2, br)


PLAN_LANES = 2048
DMA_UNROLL = 4


def _plan_kernel(mi_ref, cnt_ref, dest_ref, tab_ref):
    cnt = cnt_ref[...]
    nb = jnp.floor((cnt + (EXPERT_BLOCK - 1)) * (1.0 / EXPERT_BLOCK))
    r = lax.broadcasted_iota(I32, (N_EXPERTS, N_EXPERTS), 0)
    c = lax.broadcasted_iota(I32, (N_EXPERTS, N_EXPERTS), 1)
    incl = jnp.where(c <= r, 1.0, 0.0).astype(BF16)
    bend = jnp.dot(incl, nb.astype(BF16), preferred_element_type=F32)
    bstart = bend - nb
    first_row = bstart[:, :1] * EXPERT_BLOCK

    tl = mi_ref.shape[1]
    iota_e = lax.broadcasted_iota(I32, (N_EXPERTS, tl), 0)
    for k in range(TOP_K):
        hit = iota_e == mi_ref[k:k + 1, :]
        base = jnp.sum(jnp.where(hit, first_row, 0.0), axis=0, keepdims=True)
        dest_ref[k:k + 1, :] = base.astype(I32) + mi_ref[TOP_K + k:TOP_K + k + 1, :]

    nl = tab_ref.shape[1]
    bidx = lax.broadcasted_iota(I32, (1, nl), 1).astype(F32)
    blk_e = jnp.sum(jnp.where(bend[:, :1] <= bidx, 1.0, 0.0), axis=0, keepdims=True)
    blk_e = jnp.minimum(blk_e, N_EXPERTS - 1.0)
    hit = lax.broadcasted_iota(I32, (N_EXPERTS, nl), 0) == blk_e.astype(I32)
    cnt_b = jnp.sum(jnp.where(hit, cnt[:, :1], 0.0), axis=0, keepdims=True)
    start_b = jnp.sum(jnp.where(hit, bstart[:, :1], 0.0), axis=0, keepdims=True)
    valid = jnp.clip(cnt_b - (bidx - start_b) * EXPERT_BLOCK, 0.0, float(EXPERT_BLOCK))
    n_used = jnp.broadcast_to(jnp.max(bend[:, :1], axis=0, keepdims=True), (1, nl))
    run_end = jnp.sum(jnp.where(hit, bend[:, :1], 0.0), axis=0, keepdims=True)
    zero = jnp.zeros((1, nl), F32)
    tab_ref[...] = jnp.concatenate([blk_e, valid, n_used, run_end, zero, zero, zero, zero],
                                   axis=0).astype(I32)


def _plan(meta_i, cnt, nblk):
    t = meta_i.shape[1]
    tl = _pick(t, (PLAN_LANES, 1024, 512, 256, 128))
    nl = pl.cdiv(nblk, LANES) * LANES
    return pl.pallas_call(
        _plan_kernel,
        out_shape=(jax.ShapeDtypeStruct((TOP_K, t), I32), jax.ShapeDtypeStruct((8, nl), I32)),
        grid=(t // tl,),
        in_specs=[pl.BlockSpec((8, tl), lambda i: (0, i)),
                  pl.BlockSpec((N_EXPERTS, LANES), lambda i: (0, 0))],
        out_specs=(pl.BlockSpec((TOP_K, tl), lambda i: (0, i)),
                   pl.BlockSpec((8, nl), lambda i: (0, 0))),
        compiler_params=_params(("arbitrary",), 32),
    )(meta_i, cnt)


def _dispatch_kernel(dest_ref, hp_ref, xs_ref, sem, *, chunk, t):
    base = pl.program_id(0) * chunk

    def issue(i, carry):
        for k in range(TOP_K):
            pltpu.make_async_copy(hp_ref.at[pl.ds(i, 1)],
                                  xs_ref.at[pl.ds(dest_ref[k * t + base + i], 1)],
                                  sem).start(priority=k)
        return carry

    lax.fori_loop(0, chunk, issue, 0, unroll=DMA_UNROLL)
    for k in range(TOP_K):
        pltpu.make_async_copy(hp_ref, xs_ref.at[pl.ds(0, chunk)], sem).wait()


def _dispatch(dest_flat, hp, cap):
    t, half = hp.shape
    chunk = _pick(t, (512, 256, 128))
    grid_spec = pltpu.PrefetchScalarGridSpec(
        num_scalar_prefetch=1,
        grid=(t // chunk,),
        in_specs=[pl.BlockSpec((chunk, half), lambda i, ds: (i, 0))],
        out_specs=pl.BlockSpec(memory_space=pl.ANY),
        scratch_shapes=[pltpu.SemaphoreType.DMA(())],
    )
    return pl.pallas_call(
        functools.partial(_dispatch_kernel, chunk=chunk, t=t),
        out_shape=jax.ShapeDtypeStruct((cap, half), U32),
        grid_spec=grid_spec,
        compiler_params=pltpu.CompilerParams(dimension_semantics=("arbitrary",),
                                             has_side_effects=True, vmem_limit_bytes=32 * MIB),
    )(dest_flat, hp)


def _ffn_kernel(be_ref, nv_ref, nu_ref, re_ref, xs_ref, wg_hbm, wu_hbm, wd_hbm, o_ref,
                wg_s, wu_s, wd_s, slot_ref, sem):
    b = pl.program_id(0)

    def weight_copies(e, slot):
        return (pltpu.make_async_copy(wg_hbm.at[e], wg_s.at[slot], sem.at[slot]),
                pltpu.make_async_copy(wu_hbm.at[e], wu_s.at[slot], sem.at[slot]),
                pltpu.make_async_copy(wd_hbm.at[e], wd_s.at[slot], sem.at[slot]))

    @pl.when(b < nu_ref[0])
    def _():
        e = be_ref[b]

        @pl.when(b == 0)
        def _():
            slot_ref[0] = 0
            for c in weight_copies(e, 0):
                c.start()

        first = jnp.logical_or(b == 0, be_ref[jnp.maximum(b - 1, 0)] != e)

        @pl.when(first)
        def _():
            slot = jnp.where(b == 0, 0, 1 - slot_ref[0])
            slot_ref[0] = slot
            for c in weight_copies(e, slot):
                c.wait()
            nxt = re_ref[b]

            @pl.when(nxt < nu_ref[0])
            def _():
                for c in weight_copies(be_ref[nxt], 1 - slot):
                    c.start()

        slot = slot_ref[0]
        blk, half = xs_ref.shape
        row = lax.broadcasted_iota(I32, (blk, 1), 0)
        xp = jnp.where(row < nv_ref[b], xs_ref[...], jnp.uint32(0))
        lo, hi = _unpack_halves(xp)
        lo = lo.astype(BF16)
        hi = hi.astype(BF16)
        a = (jnp.dot(lo, wg_s[slot, :half], preferred_element_type=F32)
             + jnp.dot(hi, wg_s[slot, half:], preferred_element_type=F32))
        u = (jnp.dot(lo, wu_s[slot, :half], preferred_element_type=F32)
             + jnp.dot(hi, wu_s[slot, half:], preferred_element_type=F32))
        mid = (a * jax.nn.sigmoid(a) * u).astype(BF16)
        o_ref[...] = _pack_halves(jnp.dot(mid, wd_s[slot], preferred_element_type=F32))


def _experts(blk_e, blk_valid, n_used, run_end, xs, wg, wu, wd):
    cap, half = xs.shape
    d = 2 * half
    ff = wg.shape[2]
    nblk = cap // EXPERT_BLOCK

    def row_map(b, be, nv, nu, re):
        return (jnp.minimum(b, nu[0] - 1), 0)

    grid_spec = pltpu.PrefetchScalarGridSpec(
        num_scalar_prefetch=4,
        grid=(nblk,),
        in_specs=[pl.BlockSpec((EXPERT_BLOCK, half), row_map),
                  pl.BlockSpec(memory_space=pl.ANY),
                  pl.BlockSpec(memory_space=pl.ANY),
                  pl.BlockSpec(memory_space=pl.ANY)],
        out_specs=pl.BlockSpec((EXPERT_BLOCK, half), row_map),
        scratch_shapes=[pltpu.VMEM((2, d, ff), BF16), pltpu.VMEM((2, d, ff), BF16),
                        pltpu.VMEM((2, ff, d), BF16), pltpu.SMEM((1,), I32),
                        pltpu.SemaphoreType.DMA((2,))],
    )
    return pl.pallas_call(
        _ffn_kernel,
        out_shape=jax.ShapeDtypeStruct((cap, half), U32),
        grid_spec=grid_spec,
        compiler_params=_params(("arbitrary",), 56),
    )(blk_e, blk_valid, n_used, run_end, xs, wg, wu, wd)


def _combine_kernel(dest_ref, x_ref, g_ref, w_ref, ys_ref, o_ref, ybuf, sem, *, chunk, t):
    i = pl.program_id(0)
    n = pl.num_programs(0)

    def issue(step, slot):
        base = step * chunk

        def body(r, carry):
            for k in range(TOP_K):
                pltpu.make_async_copy(ys_ref.at[pl.ds(dest_ref[k * t + base + r], 1)],
                                      ybuf.at[slot, k, pl.ds(r, 1)],
                                      sem.at[slot]).start(priority=k)
            return carry

        lax.fori_loop(0, chunk, body, 0, unroll=DMA_UNROLL)

    @pl.when(i == 0)
    def _():
        issue(0, 0)

    slot = i % 2

    @pl.when(i + 1 < n)
    def _():
        issue(i + 1, 1 - slot)

    for k in range(TOP_K):
        pltpu.make_async_copy(ys_ref.at[pl.ds(0, chunk)], ybuf.at[slot, k], sem.at[slot]).wait()
    w = w_ref[...]
    half = ybuf.shape[-1]
    lo0, hi0 = _unpack_halves(ybuf[slot, 0])
    lo1, hi1 = _unpack_halves(ybuf[slot, 1])
    g = g_ref[0]
    o_ref[:, :half] = x_ref[:, :half] + g[:, :half] * (w[:, 0:1] * lo0 + w[:, 1:2] * lo1)
    o_ref[:, half:] = x_ref[:, half:] + g[:, half:] * (w[:, 0:1] * hi0 + w[:, 1:2] * hi1)


def _combine(dest_flat, x1, gate, w_rows, ys, seq):
    t, d = x1.shape
    chunk = _pick(seq, (256, 128))
    rows_per_batch = seq // chunk
    grid_spec = pltpu.PrefetchScalarGridSpec(
        num_scalar_prefetch=1,
        grid=(t // chunk,),
        in_specs=[pl.BlockSpec((chunk, d), lambda i, ds: (i, 0)),
                  pl.BlockSpec((1, 1, d), lambda i, ds: (i // rows_per_batch, 0, 0)),
                  pl.BlockSpec((chunk, 8), lambda i, ds: (i, 0)),
                  pl.BlockSpec(memory_space=pl.ANY)],
        out_specs=pl.BlockSpec((chunk, d), lambda i, ds: (i, 0)),
        scratch_shapes=[pltpu.VMEM((2, TOP_K, chunk, d // 2), U32),
                        pltpu.SemaphoreType.DMA((2,))],
    )
    return pl.pallas_call(
        functools.partial(_combine_kernel, chunk=chunk, t=t),
        out_shape=jax.ShapeDtypeStruct((t, d), F32),
        grid_spec=grid_spec,
        compiler_params=_params(("arbitrary",), 56),
    )(dest_flat, x1, gate, w_rows, ys)


def _layer(x, c, w_ada, b_ada, norm1_w, w_in, q_norm_w, k_norm_w, sinks, w_pool, pool_scale,
           w_attn_up, w_pool_up, w_out, norm2_w, w_router_group, b_router_group,
           w_router_expert, b_router_expert, w_gate, w_up, w_down):
    b, s, d = x.shape
    t = b * s
    aw = w_attn_up.shape[0]
    pw = w_pool_up.shape[0]
    kvw = N_KV_HEADS * HEAD_DIM
    x2 = x.reshape(t, d)

    c8 = jnp.zeros((8, d), F32).at[:b].set(c)
    ada = _ada(c8, w_ada, b_ada.reshape(1, 6 * d))[:b]
    shift1, scale1, gate1, shift2, scale2, gate2 = [a.reshape(b, 1, d) for a in jnp.split(ada, 6, axis=-1)]

    q_col, kv_col = 0, aw
    p_col = aw + 2 * kvw
    ga_col = p_col + pw
    gb_col = ga_col + d
    h = _norm(x2, norm1_w.reshape(1, d), scale1, shift1, s)
    ne, _, ff = w_gate.shape
    proj, (wu_b, wd_b, wa_b, wb_b, wo_b) = _in_proj(
        h, w_in.astype(BF16),
        [w_up.reshape(ne * d, ff), w_down.reshape(ne * ff, d), w_attn_up, w_pool_up, w_out])
    wu_b = wu_b.reshape(ne, d, ff)
    wd_b = wd_b.reshape(ne, ff, d)
    proj3 = proj.reshape(b, s, proj.shape[1])

    y_attn, wg_b = _attention(proj3, sinks, q_norm_w, k_norm_w, q_col, kv_col,
                              w_gate.reshape(ne * d, ff))
    y_attn = y_attn.reshape(t, aw)
    wg_b = wg_b.reshape(ne, d, ff)
    y_pool = _pool(proj3, w_pool.astype(BF16), pool_scale.reshape(1, pw), p_col).reshape(t, pw)
    mixed = _mix(y_attn, y_pool, wa_b, wb_b, proj, ga_col, gb_col)
    x1 = _out_proj(mixed, wo_b, x2, gate1, s)

    wr = jnp.concatenate([w_router_group, w_router_expert,
                          jnp.zeros((d, LANES - N_ROUTER), F32)], axis=1)
    wr_hi = wr.astype(BF16)
    wr_lo = (wr - wr_hi.astype(F32)).astype(BF16)
    wr2 = jnp.concatenate([wr_hi, wr_lo], axis=1)
    br = jnp.concatenate([b_router_group, b_router_expert]).reshape(N_ROUTER, 1)
    hp, meta_i, meta_f, cnt = _router(x1, norm2_w.reshape(1, d), scale2, shift2, wr2, br, s)

    cap = TOP_K * t + N_EXPERTS * EXPERT_BLOCK
    nblk = cap // EXPERT_BLOCK
    dest, tab = _plan(meta_i, cnt, nblk)
    dest_flat = dest.reshape(TOP_K * t)
    blk_e, blk_valid, n_used, run_end = tab[0, :nblk], tab[1, :nblk], tab[2, :1], tab[3, :nblk]

    xs = _dispatch(dest_flat, hp, cap)
    ys = _experts(blk_e, blk_valid, n_used, run_end, xs, wg_b, wu_b, wd_b)
    out = _combine(dest_flat, x1, gate2, meta_f.T, ys, s)
    return out.reshape(b, s, d)


def kernel(x, c, w_ada, b_ada, norm1_w, w_in, q_norm_w, k_norm_w, sinks, w_pool, pool_scale,
           w_attn_up, w_pool_up, w_out, norm2_w, w_router_group, b_router_group,
           w_router_expert, b_router_expert, w_gate, w_up, w_down):
    for l in range(w_ada.shape[0]):
        x = _layer(x, c, w_ada[l], b_ada[l], norm1_w[l], w_in[l], q_norm_w[l], k_norm_w[l],
                   sinks[l], w_pool[l], pool_scale[l], w_attn_up[l], w_pool_up[l], w_out[l],
                   norm2_w[l], w_router_group[l], b_router_group[l], w_router_expert[l],
                   b_router_expert[l], w_gate[l], w_up[l], w_down[l])
    return x
```

```python
import functools

import jax
import jax.numpy as jnp
import numpy as np
from jax import lax
from jax.experimental import pallas as pl
from jax.experimental.pallas import tpu as pltpu

F32 = jnp.float32
BF16 = jnp.bfloat16
U32 = jnp.uint32
I32 = jnp.int32

HEAD_DIM = 64
N_Q_HEADS = 32
N_KV_HEADS = 4
Q_PER_KV = N_Q_HEADS // N_KV_HEADS
ATTN_BLOCK = 128
WINDOW = 128
POOL_WINDOWS = (2, 4, 8, 16)
POOL_HALO = 16
N_GROUPS = 8
EXPERTS_PER_GROUP = 8
N_EXPERTS = N_GROUPS * EXPERTS_PER_GROUP
N_ROUTER = N_GROUPS + N_EXPERTS
TOP_K = 2
EPS = 1e-6
NEG_INF = -1e30
LOG2E = 1.4426950408889634
ATTN_STAGE_PAIRS = 8

LANES = 128
EXPERT_BLOCK = 256
MIB = 1024 * 1024


def _params(semantics, vmem_mib):
    return pltpu.CompilerParams(dimension_semantics=semantics, vmem_limit_bytes=vmem_mib * MIB)


def _pick(n, candidates):
    for c in candidates:
        if n % c == 0:
            return c
    return n


def _ada_kernel(c_ref, w_ref, b_ref, o_ref):
    c = c_ref[...]
    s = (c * jax.nn.sigmoid(c)).astype(BF16)
    o_ref[...] = jnp.dot(s, w_ref[...].astype(BF16), preferred_element_type=F32) + b_ref[...]


def _ada(c8, w, b):
    d, n = w.shape
    tn = _pick(n, (1024, 512, 256, 128))
    return pl.pallas_call(
        _ada_kernel,
        out_shape=jax.ShapeDtypeStruct((8, n), F32),
        grid=(n // tn,),
        in_specs=[pl.BlockSpec((8, d), lambda j: (0, 0)),
                  pl.BlockSpec((d, tn), lambda j: (0, j)),
                  pl.BlockSpec((1, tn), lambda j: (0, j))],
        out_specs=pl.BlockSpec((8, tn), lambda j: (0, j)),
        compiler_params=_params(("arbitrary",), 56),
    )(c8, w, b)


def _modulated_norm(x, gain, shift):
    ms = jnp.mean(x * x, axis=-1, keepdims=True)
    return x * lax.rsqrt(ms + EPS) * gain + shift


NORM_ROWS = 16
NORM_UNROLL = 4


def _pack_halves(v):
    n = v.shape[1] // 2
    bits = lax.bitcast_convert_type(v.astype(BF16).astype(F32), U32)
    return (bits[:, :n] >> 16) | (bits[:, n:] & jnp.uint32(0xFFFF0000))


def _unpack_halves(p):
    lo = lax.bitcast_convert_type(p << 16, F32)
    hi = lax.bitcast_convert_type(p & jnp.uint32(0xFFFF0000), F32)
    return lo, hi


def _norm_kernel(x_ref, nw_ref, sc_ref, sh_ref, o_ref):
    gain = nw_ref[...] * (1.0 + sc_ref[0])
    shift = sh_ref[0]

    def body(r, carry):
        rows = pl.ds(pl.multiple_of(r * NORM_ROWS, NORM_ROWS), NORM_ROWS)
        o_ref[rows, :] = _modulated_norm(x_ref[rows, :], gain, shift).astype(o_ref.dtype)
        return carry

    lax.fori_loop(0, x_ref.shape[0] // NORM_ROWS, body, 0, unroll=NORM_UNROLL)


def _norm(x2, nw, scale, shift, seq):
    t, d = x2.shape
    tm = _pick(seq, (512, 256, 128))
    rows_per_batch = seq // tm
    return pl.pallas_call(
        _norm_kernel,
        out_shape=jax.ShapeDtypeStruct((t, d), BF16),
        grid=(t // tm,),
        in_specs=[pl.BlockSpec((tm, d), lambda i: (i, 0)),
                  pl.BlockSpec((1, d), lambda i: (0, 0)),
                  pl.BlockSpec((1, 1, d), lambda i: (i // rows_per_batch, 0, 0)),
                  pl.BlockSpec((1, 1, d), lambda i: (i // rows_per_batch, 0, 0))],
        out_specs=pl.BlockSpec((tm, d), lambda i: (i, 0)),
        compiler_params=_params(("parallel",), 40),
    )(x2, nw, scale, shift)


def _in_kernel(a_ref, w_ref, *refs):
    n_riders = (len(refs) - 1) // 2
    o_ref = refs[n_riders]
    o_ref[...] = jnp.dot(a_ref[...], w_ref[...], preferred_element_type=F32).astype(o_ref.dtype)
    for src, dst in zip(refs[:n_riders], refs[n_riders + 1:]):
        dst[...] = src[...].astype(BF16)


def _rider_steps(nrows, steps):
    n = 1 << (steps.bit_length() - 1)
    while nrows % n or (nrows // n) % 16:
        n //= 2
    return n


def _in_proj(h, w, riders):
    t, d = h.shape
    n = w.shape[1]
    tm = _pick(t, (1024, 512, 256, 128))
    tn = _pick(n, (512, 256, 128))
    nj = n // tn
    steps = (t // tm) * nj

    def rider_spec(arr):
        nsteps = _rider_steps(arr.shape[0], steps)
        return pl.BlockSpec((arr.shape[0] // nsteps, arr.shape[1]),
                            lambda i, j: (jnp.minimum(i * nj + j, nsteps - 1), 0))

    rider_specs = [rider_spec(r) for r in riders]
    outs = pl.pallas_call(
        _in_kernel,
        out_shape=tuple([jax.ShapeDtypeStruct((t, n), BF16)]
                        + [jax.ShapeDtypeStruct(r.shape, BF16) for r in riders]),
        grid=(t // tm, nj),
        in_specs=[pl.BlockSpec((tm, d), lambda i, j: (i, 0)),
                  pl.BlockSpec((d, tn), lambda i, j: (0, j))] + rider_specs,
        out_specs=tuple([pl.BlockSpec((tm, tn), lambda i, j: (i, j))] + rider_specs),
        compiler_params=_params(("arbitrary", "arbitrary"), 56),
    )(h, w, *riders)
    return outs[0], outs[1:]


def _head_sumsq(v, ones_bd):
    sq = v * v
    hi = sq.astype(BF16)
    lo = (sq - hi.astype(F32)).astype(BF16)
    return (jnp.dot(hi, ones_bd, preferred_element_type=F32)
            + jnp.dot(lo, ones_bd, preferred_element_type=F32))


def _head_rms_norm(v, w2, ones_bd):
    ss = _head_sumsq(v, ones_bd)
    return v * lax.rsqrt(ss * (1.0 / HEAD_DIM) + EPS) * w2


def _attn_kernel(sink_ref, q_ref, kvc_ref, kvp_ref, bias_ref, qw_ref, kw_ref, wg_ref, o_ref, wg_o):
    wg_o[...] = wg_ref[...].astype(BF16)

    blk = ATTN_BLOCK
    kvw = N_KV_HEADS * HEAD_DIM
    lane = lax.broadcasted_iota(I32, (1, LANES), 1)
    low_half = lane < HEAD_DIM
    r = lax.broadcasted_iota(I32, (LANES, LANES), 0) // HEAD_DIM
    c = lax.broadcasted_iota(I32, (LANES, LANES), 1) // HEAD_DIM
    ones_bd = jnp.where(r == c, 1.0, 0.0).astype(BF16)

    kv = jnp.concatenate([kvp_ref[0], kvc_ref[0]], axis=0).astype(F32)
    kw2 = kw_ref[...]
    sum_a = jnp.broadcast_to(jnp.where(low_half, 1.0, 0.0), (2 * blk, LANES))
    sum_b = jnp.broadcast_to(jnp.where(low_half, 0.0, 1.0), (2 * blk, LANES))

    k_bds, v_bds = [], []
    for g in range(N_KV_HEADS):
        chunk = g // 2
        kc = _head_rms_norm(kv[:, chunk * LANES:(chunk + 1) * LANES], kw2, ones_bd)
        vc = kv[:, kvw + chunk * LANES: kvw + (chunk + 1) * LANES]
        kr = pltpu.roll(kc, HEAD_DIM, axis=1)
        vr = pltpu.roll(vc, HEAD_DIM, axis=1)
        if g % 2 == 0:
            k_lo, k_hi, v_lo, v_hi = kc, kr, vc, vr
        else:
            k_lo, k_hi, v_lo, v_hi = kr, kc, vr, vc
        k_bds.append(jnp.concatenate([jnp.where(low_half, k_lo, 0.0),
                                      jnp.where(low_half, 0.0, k_hi)], axis=0).astype(BF16))
        v_bds.append(jnp.concatenate(
            [jnp.concatenate([jnp.where(low_half, v_lo, 0.0), sum_a], axis=1),
             jnp.concatenate([jnp.where(low_half, 0.0, v_hi), sum_b], axis=1)],
            axis=0).astype(BF16))

    pairs_per_kv = Q_PER_KV // 2
    for first in range(0, N_Q_HEADS // 2, ATTN_STAGE_PAIRS):
        pairs = range(first, first + ATTN_STAGE_PAIRS)
        scores = []
        for j in pairs:
            qp = q_ref[0, :, j * LANES:(j + 1) * LANES].astype(F32)
            qn = _head_rms_norm(qp, qw_ref[...], ones_bd).astype(BF16)
            s = lax.dot_general(qn, k_bds[j // pairs_per_kv], (((1,), (1,)), ((), ())),
                                preferred_element_type=F32)
            scores.append(s + bias_ref[0, j])
        probs = []
        for j, s in zip(pairs, scores):
            sink_a = sink_ref[2 * j]
            sink_b = sink_ref[2 * j + 1]
            m_a = jnp.maximum(jnp.max(s[:, :2 * blk], axis=-1, keepdims=True), sink_a)
            m_b = jnp.maximum(jnp.max(s[:, 2 * blk:], axis=-1, keepdims=True), sink_b)
            p = jnp.concatenate([jnp.exp2(s[:, :2 * blk] - m_a),
                                 jnp.exp2(s[:, 2 * blk:] - m_b)], axis=1).astype(BF16)
            esink = jnp.where(low_half, jnp.exp2(sink_a - m_a), jnp.exp2(sink_b - m_b))
            probs.append((p, esink))
        for j, (p, esink) in zip(pairs, probs):
            ol = jnp.dot(p, v_bds[j // pairs_per_kv], preferred_element_type=F32)
            o = ol[:, :LANES] / (ol[:, LANES:] + esink)
            o_ref[0, :, j * LANES:(j + 1) * LANES] = o.astype(o_ref.dtype)


def _attn_bias():
    blk = ATTN_BLOCK
    i = np.arange(blk)[:, None]
    j = np.arange(2 * blk)[None, :]
    dist = i + blk - j
    band = (dist >= 0) & (dist < WINDOW)
    slopes = (2.0 ** (-8.0 * np.arange(1, N_Q_HEADS + 1, dtype=np.float32) / N_Q_HEADS)).astype(np.float32)
    alibi = -slopes[:, None, None] * dist.astype(np.float32)[None] * np.float32(LOG2E)
    later = np.where(band[None], alibi, np.float32(NEG_INF))
    first = np.where((band & (j >= blk))[None], alibi, np.float32(NEG_INF))
    tab = np.stack([first, later]).astype(np.float32)
    tab = tab.reshape(2, N_Q_HEADS // 2, 2, blk, 2 * blk)
    return np.transpose(tab, (0, 1, 3, 2, 4)).reshape(2, N_Q_HEADS // 2, blk, 4 * blk)


def _attention(proj3, sinks, q_norm_w, k_norm_w, q_col, kv_col, rider):
    b, s, _ = proj3.shape
    aw = N_Q_HEADS * HEAD_DIM
    kvw2 = 2 * N_KV_HEADS * HEAD_DIM
    nb = s // ATTN_BLOCK
    bias = _attn_bias()
    qw2 = (jnp.tile(q_norm_w, 2) * (HEAD_DIM ** -0.5 * LOG2E)).reshape(1, LANES)
    sinks = sinks * LOG2E
    kw2 = jnp.tile(k_norm_w, 2).reshape(1, LANES)
    rider_spec = pl.BlockSpec((rider.shape[0] // (b * nb), rider.shape[1]),
                              lambda bi, n, sk: (bi * nb + n, 0))
    grid_spec = pltpu.PrefetchScalarGridSpec(
        num_scalar_prefetch=1,
        grid=(b, nb),
        in_specs=[pl.BlockSpec((1, ATTN_BLOCK, aw), lambda bi, n, sk: (bi, n, q_col // aw)),
                  pl.BlockSpec((1, ATTN_BLOCK, kvw2), lambda bi, n, sk: (bi, n, kv_col // kvw2)),
                  pl.BlockSpec((1, ATTN_BLOCK, kvw2),
                               lambda bi, n, sk: (bi, jnp.maximum(n - 1, 0), kv_col // kvw2)),
                  pl.BlockSpec((1, N_Q_HEADS // 2, ATTN_BLOCK, 4 * ATTN_BLOCK),
                               lambda bi, n, sk: (jnp.minimum(n, 1), 0, 0, 0)),
                  pl.BlockSpec((1, LANES), lambda bi, n, sk: (0, 0)),
                  pl.BlockSpec((1, LANES), lambda bi, n, sk: (0, 0)),
                  rider_spec],
        out_specs=(pl.BlockSpec((1, ATTN_BLOCK, aw), lambda bi, n, sk: (bi, n, 0)), rider_spec),
    )
    return pl.pallas_call(
        _attn_kernel,
        out_shape=(jax.ShapeDtypeStruct((b, s, aw), BF16), jax.ShapeDtypeStruct(rider.shape, BF16)),
        grid_spec=grid_spec,
        compiler_params=_params(("arbitrary", "arbitrary"), 56),
    )(sinks, proj3, proj3, proj3, bias, qw2, kw2, rider)


def _pool_kernel(pc_ref, ph_ref, w_ref, ps_ref, o_ref):
    n = pl.program_id(1)
    ts = pc_ref.shape[1]
    cg = w_ref.shape[1]
    cur = pc_ref[0].astype(F32)
    halo = jnp.where(n > 0, ph_ref[0].astype(F32), 0.0)
    ext = jnp.concatenate([halo, cur], axis=0)
    t1 = (lax.broadcasted_iota(I32, (ts, 1), 0) + n * ts + 1).astype(F32)
    for g, win in enumerate(POOL_WINDOWS):
        acc = ext[:, g * cg:(g + 1) * cg]
        k = 1
        while k < win:
            acc = acc + pltpu.roll(acc, k, axis=0)
            k *= 2
        mean = acc[POOL_HALO:] / jnp.minimum(t1, float(win))
        d = (mean - cur[:, g * cg:(g + 1) * cg]).astype(BF16)
        y = jnp.dot(d, w_ref[g], preferred_element_type=F32) * ps_ref[:, g * cg:(g + 1) * cg]
        o_ref[0, :, g * cg:(g + 1) * cg] = y.astype(o_ref.dtype)


def _pool(proj3, w_pool, pool_scale, p_col):
    b, s, _ = proj3.shape
    ng, cg, _ = w_pool.shape
    pw = ng * cg
    ts = _pick(s, (512, 256, 128))
    return pl.pallas_call(
        _pool_kernel,
        out_shape=jax.ShapeDtypeStruct((b, s, pw), BF16),
        grid=(b, s // ts),
        in_specs=[pl.BlockSpec((pl.Element(1), pl.Element(ts), pl.Element(pw)),
                               lambda bi, n: (bi, pl.multiple_of(n * ts, ts), p_col)),
                  pl.BlockSpec((pl.Element(1), pl.Element(POOL_HALO), pl.Element(pw)),
                               lambda bi, n: (bi, pl.multiple_of(jnp.maximum(n * ts - POOL_HALO, 0),
                                                                 POOL_HALO), p_col)),
                  pl.BlockSpec((ng, cg, cg), lambda bi, n: (0, 0, 0)),
                  pl.BlockSpec((1, pw), lambda bi, n: (0, 0))],
        out_specs=pl.BlockSpec((1, ts, pw), lambda bi, n: (bi, n, 0)),
        compiler_params=_params(("parallel", "arbitrary"), 40),
    )(proj3, proj3, w_pool, pool_scale)


def _mix_kernel(a_ref, b_ref, wa_ref, wb_ref, ga_ref, gb_ref, o_ref):
    ya = jnp.dot(a_ref[...], wa_ref[...], preferred_element_type=F32)
    yb = jnp.dot(b_ref[...], wb_ref[...], preferred_element_type=F32)
    ga = jax.nn.sigmoid(ga_ref[...].astype(F32))
    gb = jax.nn.sigmoid(gb_ref[...].astype(F32))
    o_ref[...] = (ga * ya + gb * yb).astype(o_ref.dtype)


def _mix(y_attn, y_pool, wa, wb, proj, ga_col, gb_col):
    t, aw = y_attn.shape
    pw = y_pool.shape[1]
    d = wa.shape[1]
    tm = _pick(t, (1024, 512, 256, 128))
    tn = _pick(d, (1024, 512, 256, 128))
    return pl.pallas_call(
        _mix_kernel,
        out_shape=jax.ShapeDtypeStruct((t, d), BF16),
        grid=(t // tm, d // tn),
        in_specs=[pl.BlockSpec((tm, aw), lambda i, j: (i, 0)),
                  pl.BlockSpec((tm, pw), lambda i, j: (i, 0)),
                  pl.BlockSpec((aw, tn), lambda i, j: (0, j)),
                  pl.BlockSpec((pw, tn), lambda i, j: (0, j)),
                  pl.BlockSpec((pl.Element(tm), pl.Element(tn)),
                               lambda i, j: (pl.multiple_of(i * tm, tm),
                                             pl.multiple_of(ga_col + j * tn, LANES))),
                  pl.BlockSpec((pl.Element(tm), pl.Element(tn)),
                               lambda i, j: (pl.multiple_of(i * tm, tm),
                                             pl.multiple_of(gb_col + j * tn, LANES)))],
        out_specs=pl.BlockSpec((tm, tn), lambda i, j: (i, j)),
        compiler_params=_params(("parallel", "arbitrary"), 56),
    )(y_attn, y_pool, wa, wb, proj, proj)


def _out_kernel(m_ref, w_ref, x_ref, g_ref, o_ref):
    y = jnp.dot(m_ref[...], w_ref[...], preferred_element_type=F32)
    o_ref[...] = x_ref[...] + g_ref[0] * y


def _out_proj(mixed, w, x2, gate, seq):
    t, d = x2.shape
    tm = _pick(seq, (1024, 512, 256, 128))
    tn = _pick(d, (1024, 512, 256, 128))
    rows_per_batch = seq // tm
    return pl.pallas_call(
        _out_kernel,
        out_shape=jax.ShapeDtypeStruct((t, d), F32),
        grid=(t // tm, d // tn),
        in_specs=[pl.BlockSpec((tm, d), lambda i, j: (i, 0)),
                  pl.BlockSpec((d, tn), lambda i, j: (0, j)),
                  pl.BlockSpec((tm, tn), lambda i, j: (i, j)),
                  pl.BlockSpec((1, 1, tn), lambda i, j: (i // rows_per_batch, 0, j))],
        out_specs=pl.BlockSpec((tm, tn), lambda i, j: (i, j)),
        compiler_params=_params(("parallel", "arbitrary"), 56),
    )(mixed, w, x2, gate)


def _first_argmax(v, iota, n):
    m = jnp.max(v, axis=0, keepdims=True)
    idx = jnp.min(jnp.where(v == m, iota, n), axis=0, keepdims=True)
    return m, idx


def _router_kernel(x_ref, nw_ref, sc_ref, sh_ref, w2_ref, br_ref,
                   hp_ref, mi_ref, mf_ref, cnt_ref, carry_ref, hb_ref):
    step = pl.program_id(0)
    tm, d = x_ref.shape

    @pl.when(step == 0)
    def _():
        carry_ref[...] = jnp.zeros_like(carry_ref)

    gain = nw_ref[...] * (1.0 + sc_ref[0])
    shift = sh_ref[0]

    def body(r, carry):
        rows = pl.ds(pl.multiple_of(r * NORM_ROWS, NORM_ROWS), NORM_ROWS)
        h = _modulated_norm(x_ref[rows, :], gain, shift)
        hb_ref[rows, :] = h.astype(BF16)
        hp_ref[rows, :] = _pack_halves(h)
        return carry

    lax.fori_loop(0, tm // NORM_ROWS, body, 0, unroll=NORM_UNROLL)

    l2 = jnp.dot(hb_ref[...], w2_ref[...], preferred_element_type=F32)
    lt = l2[:, :LANES] + l2[:, LANES:]
    logits = lt.T[:N_ROUTER] + br_ref[...]
    iota8 = lax.broadcasted_iota(I32, (N_GROUPS, tm), 0)
    gl = logits[:N_GROUPS]
    g_max, g_sel = _first_argmax(gl, iota8, N_GROUPS)
    p_group = 1.0 / jnp.sum(jnp.exp(gl - g_max), axis=0, keepdims=True)
    e_in = jnp.zeros((EXPERTS_PER_GROUP, tm), F32)
    for g in range(N_GROUPS):
        lo = N_GROUPS + g * EXPERTS_PER_GROUP
        e_in = e_in + jnp.where(g_sel == g, logits[lo:lo + EXPERTS_PER_GROUP], 0.0)
    v0, i0 = _first_argmax(e_in, iota8, EXPERTS_PER_GROUP)
    rest = jnp.where(iota8 == i0, -jnp.inf, e_in)
    v1, i1 = _first_argmax(rest, iota8, EXPERTS_PER_GROUP)
    t = jnp.exp(v1 - v0)
    w0 = p_group / (1.0 + t)
    w1 = p_group * t / (1.0 + t)
    e0 = g_sel * EXPERTS_PER_GROUP + i0
    e1 = g_sel * EXPERTS_PER_GROUP + i1

    iota_e = lax.broadcasted_iota(I32, (N_EXPERTS, tm), 0)
    hit0 = iota_e == e0
    hit1 = iota_e == e1
    onehot = jnp.where(hit0 | hit1, 1.0, 0.0).astype(BF16)
    rr = lax.broadcasted_iota(I32, (tm, tm), 0)
    cc = lax.broadcasted_iota(I32, (tm, tm), 1)
    before = jnp.where(rr < cc, 1.0, 0.0).astype(BF16)
    prior = jnp.dot(onehot, before, preferred_element_type=F32) + carry_ref[:, :1]
    r0 = jnp.sum(jnp.where(hit0, prior, 0.0), axis=0, keepdims=True)
    r1 = jnp.sum(jnp.where(hit1, prior, 0.0), axis=0, keepdims=True)
    carry_ref[...] += jnp.dot(onehot, jnp.ones((tm, LANES), BF16), preferred_element_type=F32)

    zi = jnp.zeros((1, tm), I32)
    zf = jnp.zeros((1, tm), F32)
    mi_ref[...] = jnp.concatenate([e0, e1, r0.astype(I32), r1.astype(I32), zi, zi, zi, zi], axis=0)
    mf_ref[...] = jnp.concatenate([w0, w1, zf, zf, zf, zf, zf, zf], axis=0)
    cnt_ref[...] = carry_ref[...]


def _router(x1, nw, scale, shift, w2, br, seq):
    t, d = x1.shape
    tm = _pick(seq, (512, 256, 128))
    rows_per_batch = seq // tm
    return pl.pallas_call(
        _router_kernel,
        out_shape=(jax.ShapeDtypeStruct((t, d // 2), U32),
                   jax.ShapeDtypeStruct((8, t), I32),
                   jax.ShapeDtypeStruct((8, t), F32),
                   jax.ShapeDtypeStruct((N_EXPERTS, LANES), F32)),
        grid=(t // tm,),
        in_specs=[pl.BlockSpec((tm, d), lambda i: (i, 0)),
                  pl.BlockSpec((1, d), lambda i: (0, 0)),
                  pl.BlockSpec((1, 1, d), lambda i: (i // rows_per_batch, 0, 0)),
                  pl.BlockSpec((1, 1, d), lambda i: (i // rows_per_batch, 0, 0)),
                  pl.BlockSpec((d, 2 * LANES), lambda i: (0, 0)),
                  pl.BlockSpec((N_ROUTER, 1), lambda i: (0, 0))],
        out_specs=(pl.BlockSpec((tm, d // 2), lambda i: (i, 0)),
                   pl.BlockSpec((8, tm), lambda i: (0, i)),
                   pl.BlockSpec((8, tm), lambda i: (0, i)),
                   pl.BlockSpec((N_EXPERTS, LANES), lambda i: (0, 0))),
        scratch_shapes=[pltpu.VMEM((N_EXPERTS, LANES), F32), pltpu.VMEM((tm, d), BF16)],
        compiler_params=_params(("arbitrary",), 56),
    )(x1, nw, scale, shift, w2, br)


PLAN_LANES = 2048
DMA_UNROLL = 4


def _plan_kernel(mi_ref, cnt_ref, dest_ref, tab_ref):
    cnt = cnt_ref[...]
    nb = jnp.floor((cnt + (EXPERT_BLOCK - 1)) * (1.0 / EXPERT_BLOCK))
    r = lax.broadcasted_iota(I32, (N_EXPERTS, N_EXPERTS), 0)
    c = lax.broadcasted_iota(I32, (N_EXPERTS, N_EXPERTS), 1)
    incl = jnp.where(c <= r, 1.0, 0.0).astype(BF16)
    bend = jnp.dot(incl, nb.astype(BF16), preferred_element_type=F32)
    bstart = bend - nb
    first_row = bstart[:, :1] * EXPERT_BLOCK

    tl = mi_ref.shape[1]
    iota_e = lax.broadcasted_iota(I32, (N_EXPERTS, tl), 0)
    for k in range(TOP_K):
        hit = iota_e == mi_ref[k:k + 1, :]
        base = jnp.sum(jnp.where(hit, first_row, 0.0), axis=0, keepdims=True)
        dest_ref[k:k + 1, :] = base.astype(I32) + mi_ref[TOP_K + k:TOP_K + k + 1, :]

    nl = tab_ref.shape[1]
    bidx = lax.broadcasted_iota(I32, (1, nl), 1).astype(F32)
    blk_e = jnp.sum(jnp.where(bend[:, :1] <= bidx, 1.0, 0.0), axis=0, keepdims=True)
    blk_e = jnp.minimum(blk_e, N_EXPERTS - 1.0)
    hit = lax.broadcasted_iota(I32, (N_EXPERTS, nl), 0) == blk_e.astype(I32)
    cnt_b = jnp.sum(jnp.where(hit, cnt[:, :1], 0.0), axis=0, keepdims=True)
    start_b = jnp.sum(jnp.where(hit, bstart[:, :1], 0.0), axis=0, keepdims=True)
    valid = jnp.clip(cnt_b - (bidx - start_b) * EXPERT_BLOCK, 0.0, float(EXPERT_BLOCK))
    n_used = jnp.broadcast_to(jnp.max(bend[:, :1], axis=0, keepdims=True), (1, nl))
    run_end = jnp.sum(jnp.where(hit, bend[:, :1], 0.0), axis=0, keepdims=True)
    zero = jnp.zeros((1, nl), F32)
    tab_ref[...] = jnp.concatenate([blk_e, valid, n_used, run_end, zero, zero, zero, zero],
                                   axis=0).astype(I32)


def _plan(meta_i, cnt, nblk):
    t = meta_i.shape[1]
    tl = _pick(t, (PLAN_LANES, 1024, 512, 256, 128))
    nl = pl.cdiv(nblk, LANES) * LANES
    return pl.pallas_call(
        _plan_kernel,
        out_shape=(jax.ShapeDtypeStruct((TOP_K, t), I32), jax.ShapeDtypeStruct((8, nl), I32)),
        grid=(t // tl,),
        in_specs=[pl.BlockSpec((8, tl), lambda i: (0, i)),
                  pl.BlockSpec((N_EXPERTS, LANES), lambda i: (0, 0))],
        out_specs=(pl.BlockSpec((TOP_K, tl), lambda i: (0, i)),
                   pl.BlockSpec((8, nl), lambda i: (0, 0))),
        compiler_params=_params(("arbitrary",), 32),
    )(meta_i, cnt)


def _dispatch_kernel(dest_ref, hp_ref, xs_ref, sem, *, chunk, t):
    base = pl.program_id(0) * chunk

    def issue(i, carry):
        for k in range(TOP_K):
            pltpu.make_async_copy(hp_ref.at[pl.ds(i, 1)],
                                  xs_ref.at[pl.ds(dest_ref[k * t + base + i], 1)],
                                  sem).start(priority=k)
        return carry

    lax.fori_loop(0, chunk, issue, 0, unroll=DMA_UNROLL)
    for k in range(TOP_K):
        pltpu.make_async_copy(hp_ref, xs_ref.at[pl.ds(0, chunk)], sem).wait()


def _dispatch(dest_flat, hp, cap):
    t, half = hp.shape
    chunk = _pick(t, (512, 256, 128))
    grid_spec = pltpu.PrefetchScalarGridSpec(
        num_scalar_prefetch=1,
        grid=(t // chunk,),
        in_specs=[pl.BlockSpec((chunk, half), lambda i, ds: (i, 0))],
        out_specs=pl.BlockSpec(memory_space=pl.ANY),
        scratch_shapes=[pltpu.SemaphoreType.DMA(())],
    )
    return pl.pallas_call(
        functools.partial(_dispatch_kernel, chunk=chunk, t=t),
        out_shape=jax.ShapeDtypeStruct((cap, half), U32),
        grid_spec=grid_spec,
        compiler_params=pltpu.CompilerParams(dimension_semantics=("arbitrary",),
                                             has_side_effects=True, vmem_limit_bytes=32 * MIB),
    )(dest_flat, hp)


def _ffn_kernel(be_ref, nv_ref, nu_ref, re_ref, xs_ref, wg_hbm, wu_hbm, wd_hbm, o_ref,
                wg_s, wu_s, wd_s, slot_ref, sem):
    b = pl.program_id(0)

    def weight_copies(e, slot):
        return (pltpu.make_async_copy(wg_hbm.at[e], wg_s.at[slot], sem.at[slot]),
                pltpu.make_async_copy(wu_hbm.at[e], wu_s.at[slot], sem.at[slot]),
                pltpu.make_async_copy(wd_hbm.at[e], wd_s.at[slot], sem.at[slot]))

    @pl.when(b < nu_ref[0])
    def _():
        e = be_ref[b]

        @pl.when(b == 0)
        def _():
            slot_ref[0] = 0
            for c in weight_copies(e, 0):
                c.start()

        first = jnp.logical_or(b == 0, be_ref[jnp.maximum(b - 1, 0)] != e)

        @pl.when(first)
        def _():
            slot = jnp.where(b == 0, 0, 1 - slot_ref[0])
            slot_ref[0] = slot
            for c in weight_copies(e, slot):
                c.wait()
            nxt = re_ref[b]

            @pl.when(nxt < nu_ref[0])
            def _():
                for c in weight_copies(be_ref[nxt], 1 - slot):
                    c.start()

        slot = slot_ref[0]
        blk, half = xs_ref.shape
        row = lax.broadcasted_iota(I32, (blk, 1), 0)
        xp = jnp.where(row < nv_ref[b], xs_ref[...], jnp.uint32(0))
        lo, hi = _unpack_halves(xp)
        lo = lo.astype(BF16)
        hi = hi.astype(BF16)
        a = (jnp.dot(lo, wg_s[slot, :half], preferred_element_type=F32)
             + jnp.dot(hi, wg_s[slot, half:], preferred_element_type=F32))
        u = (jnp.dot(lo, wu_s[slot, :half], preferred_element_type=F32)
             + jnp.dot(hi, wu_s[slot, half:], preferred_element_type=F32))
        mid = (a * jax.nn.sigmoid(a) * u).astype(BF16)
        o_ref[...] = _pack_halves(jnp.dot(mid, wd_s[slot], preferred_element_type=F32))


def _experts(blk_e, blk_valid, n_used, run_end, xs, wg, wu, wd):
    cap, half = xs.shape
    d = 2 * half
    ff = wg.shape[2]
    nblk = cap // EXPERT_BLOCK

    def row_map(b, be, nv, nu, re):
        return (jnp.minimum(b, nu[0] - 1), 0)

    grid_spec = pltpu.PrefetchScalarGridSpec(
        num_scalar_prefetch=4,
        grid=(nblk,),
        in_specs=[pl.BlockSpec((EXPERT_BLOCK, half), row_map),
                  pl.BlockSpec(memory_space=pl.ANY),
                  pl.BlockSpec(memory_space=pl.ANY),
                  pl.BlockSpec(memory_space=pl.ANY)],
        out_specs=pl.BlockSpec((EXPERT_BLOCK, half), row_map),
        scratch_shapes=[pltpu.VMEM((2, d, ff), BF16), pltpu.VMEM((2, d, ff), BF16),
                        pltpu.VMEM((2, ff, d), BF16), pltpu.SMEM((1,), I32),
                        pltpu.SemaphoreType.DMA((2,))],
    )
    return pl.pallas_call(
        _ffn_kernel,
        out_shape=jax.ShapeDtypeStruct((cap, half), U32),
        grid_spec=grid_spec,
        compiler_params=_params(("arbitrary",), 56),
    )(blk_e, blk_valid, n_used, run_end, xs, wg, wu, wd)


def _combine_kernel(dest_ref, x_ref, g_ref, w_ref, ys_ref, o_ref, ybuf, sem, *, chunk, t):
    i = pl.program_id(0)
    n = pl.num_programs(0)

    def issue(step, slot):
        base = step * chunk

        def body(r, carry):
            for k in range(TOP_K):
                pltpu.make_async_copy(ys_ref.at[pl.ds(dest_ref[k * t + base + r], 1)],
                                      ybuf.at[slot, k, pl.ds(r, 1)],
                                      sem.at[slot]).start(priority=k)
            return carry

        lax.fori_loop(0, chunk, body, 0, unroll=DMA_UNROLL)

    @pl.when(i == 0)
    def _():
        issue(0, 0)

    slot = i % 2

    @pl.when(i + 1 < n)
    def _():
        issue(i + 1, 1 - slot)

    for k in range(TOP_K):
        pltpu.make_async_copy(ys_ref.at[pl.ds(0, chunk)], ybuf.at[slot, k], sem.at[slot]).wait()
    w = w_ref[...]
    half = ybuf.shape[-1]
    lo0, hi0 = _unpack_halves(ybuf[slot, 0])
    lo1, hi1 = _unpack_halves(ybuf[slot, 1])
    g = g_ref[0]
    o_ref[:, :half] = x_ref[:, :half] + g[:, :half] * (w[:, 0:1] * lo0 + w[:, 1:2] * lo1)
    o_ref[:, half:] = x_ref[:, half:] + g[:, half:] * (w[:, 0:1] * hi0 + w[:, 1:2] * hi1)


def _combine(dest_flat, x1, gate, w_rows, ys, seq):
    t, d = x1.shape
    chunk = _pick(seq, (256, 128))
    rows_per_batch = seq // chunk
    grid_spec = pltpu.PrefetchScalarGridSpec(
        num_scalar_prefetch=1,
        grid=(t // chunk,),
        in_specs=[pl.BlockSpec((chunk, d), lambda i, ds: (i, 0)),
                  pl.BlockSpec((1, 1, d), lambda i, ds: (i // rows_per_batch, 0, 0)),
                  pl.BlockSpec((chunk, 8), lambda i, ds: (i, 0)),
                  pl.BlockSpec(memory_space=pl.ANY)],
        out_specs=pl.BlockSpec((chunk, d), lambda i, ds: (i, 0)),
        scratch_shapes=[pltpu.VMEM((2, TOP_K, chunk, d // 2), U32),
                        pltpu.SemaphoreType.DMA((2,))],
    )
    return pl.pallas_call(
        functools.partial(_combine_kernel, chunk=chunk, t=t),
        out_shape=jax.ShapeDtypeStruct((t, d), F32),
        grid_spec=grid_spec,
        compiler_params=_params(("arbitrary",), 56),
    )(dest_flat, x1, gate, w_rows, ys)


def _layer(x, c, w_ada, b_ada, norm1_w, w_in, q_norm_w, k_norm_w, sinks, w_pool, pool_scale,
           w_attn_up, w_pool_up, w_out, norm2_w, w_router_group, b_router_group,
           w_router_expert, b_router_expert, w_gate, w_up, w_down):
    b, s, d = x.shape
    t = b * s
    aw = w_attn_up.shape[0]
    pw = w_pool_up.shape[0]
    kvw = N_KV_HEADS * HEAD_DIM
    x2 = x.reshape(t, d)

    c8 = jnp.zeros((8, d), F32).at[:b].set(c)
    ada = _ada(c8, w_ada, b_ada.reshape(1, 6 * d))[:b]
    shift1, scale1, gate1, shift2, scale2, gate2 = [a.reshape(b, 1, d) for a in jnp.split(ada, 6, axis=-1)]

    q_col, kv_col = 0, aw
    p_col = aw + 2 * kvw
    ga_col = p_col + pw
    gb_col = ga_col + d
    h = _norm(x2, norm1_w.reshape(1, d), scale1, shift1, s)
    ne, _, ff = w_gate.shape
    proj, (wu_b, wd_b, wa_b, wb_b, wo_b) = _in_proj(
        h, w_in.astype(BF16),
        [w_up.reshape(ne * d, ff), w_down.reshape(ne * ff, d), w_attn_up, w_pool_up, w_out])
    wu_b = wu_b.reshape(ne, d, ff)
    wd_b = wd_b.reshape(ne, ff, d)
    proj3 = proj.reshape(b, s, proj.shape[1])

    y_attn, wg_b = _attention(proj3, sinks, q_norm_w, k_norm_w, q_col, kv_col,
                              w_gate.reshape(ne * d, ff))
    y_attn = y_attn.reshape(t, aw)
    wg_b = wg_b.reshape(ne, d, ff)
    y_pool = _pool(proj3, w_pool.astype(BF16), pool_scale.reshape(1, pw), p_col).reshape(t, pw)
    mixed = _mix(y_attn, y_pool, wa_b, wb_b, proj, ga_col, gb_col)
    x1 = _out_proj(mixed, wo_b, x2, gate1, s)

    wr = jnp.concatenate([w_router_group, w_router_expert,
                          jnp.zeros((d, LANES - N_ROUTER), F32)], axis=1)
    wr_hi = wr.astype(BF16)
    wr_lo = (wr - wr_hi.astype(F32)).astype(BF16)
    wr2 = jnp.concatenate([wr_hi, wr_lo], axis=1)
    br = jnp.concatenate([b_router_group, b_router_expert]).reshape(N_ROUTER, 1)
    hp, meta_i, meta_f, cnt = _router(x1, norm2_w.reshape(1, d), scale2, shift2, wr2, br, s)

    cap = TOP_K * t + N_EXPERTS * EXPERT_BLOCK
    nblk = cap // EXPERT_BLOCK
    dest, tab = _plan(meta_i, cnt, nblk)
    dest_flat = dest.reshape(TOP_K * t)
    blk_e, blk_valid, n_used, run_end = tab[0, :nblk], tab[1, :nblk], tab[2, :1], tab[3, :nblk]

    xs = _dispatch(dest_flat, hp, cap)
    ys = _experts(blk_e, blk_valid, n_used, run_end, xs, wg_b, wu_b, wd_b)
    out = _combine(dest_flat, x1, gate2, meta_f.T, ys, s)
    return out.reshape(b, s, d)


def kernel(x, c, w_ada, b_ada, norm1_w, w_in, q_norm_w, k_norm_w, sinks, w_pool, pool_scale,
           w_attn_up, w_pool_up, w_out, norm2_w, w_router_group, b_router_group,
           w_router_expert, b_router_expert, w_gate, w_up, w_down):
    for l in range(w_ada.shape[0]):
        x = _layer(x, c, w_ada[l], b_ada[l], norm1_w[l], w_in[l], q_norm_w[l], k_norm_w[l],
                   sinks[l], w_pool[l], pool_scale[l], w_attn_up[l], w_pool_up[l], w_out[l],
                   norm2_w[l], w_router_group[l], b_router_group[l], w_router_expert[l],
                   b_router_expert[l], w_gate[l], w_up[l], w_down[l])
    return x
```

```python
import functools

import jax
import jax.numpy as jnp
import numpy as np
from jax import lax
from jax.experimental import pallas as pl
from jax.experimental.pallas import tpu as pltpu

F32 = jnp.float32
BF16 = jnp.bfloat16
U32 = jnp.uint32
I32 = jnp.int32

HEAD_DIM = 64
N_Q_HEADS = 32
N_KV_HEADS = 4
Q_PER_KV = N_Q_HEADS // N_KV_HEADS
ATTN_BLOCK = 128
WINDOW = 128
POOL_WINDOWS = (2, 4, 8, 16)
POOL_HALO = 16
N_GROUPS = 8
EXPERTS_PER_GROUP = 8
N_EXPERTS = N_GROUPS * EXPERTS_PER_GROUP
N_ROUTER = N_GROUPS + N_EXPERTS
TOP_K = 2
EPS = 1e-6
NEG_INF = -1e30
LOG2E = 1.4426950408889634
ATTN_STAGE_PAIRS = 8

LANES = 128
EXPERT_BLOCK = 256
MIB = 1024 * 1024


def _params(semantics, vmem_mib):
    return pltpu.CompilerParams(dimension_semantics=semantics, vmem_limit_bytes=vmem_mib * MIB)


def _pick(n, candidates):
    for c in candidates:
        if n % c == 0:
            return c
    return n


def _ada_kernel(c_ref, w_ref, b_ref, o_ref):
    c = c_ref[...]
    s = (c * jax.nn.sigmoid(c)).astype(BF16)
    o_ref[...] = jnp.dot(s, w_ref[...].astype(BF16), preferred_element_type=F32) + b_ref[...]


def _ada(c8, w, b):
    d, n = w.shape
    tn = _pick(n, (1024, 512, 256, 128))
    return pl.pallas_call(
        _ada_kernel,
        out_shape=jax.ShapeDtypeStruct((8, n), F32),
        grid=(n // tn,),
        in_specs=[pl.BlockSpec((8, d), lambda j: (0, 0)),
                  pl.BlockSpec((d, tn), lambda j: (0, j)),
                  pl.BlockSpec((1, tn), lambda j: (0, j))],
        out_specs=pl.BlockSpec((8, tn), lambda j: (0, j)),
        compiler_params=_params(("arbitrary",), 56),
    )(c8, w, b)


def _modulated_norm(x, gain, shift):
    ms = jnp.mean(x * x, axis=-1, keepdims=True)
    return x * lax.rsqrt(ms + EPS) * gain + shift


NORM_ROWS = 16
NORM_UNROLL = 4


def _pack_halves(v):
    n = v.shape[1] // 2
    bits = lax.bitcast_convert_type(v.astype(BF16).astype(F32), U32)
    return (bits[:, :n] >> 16) | (bits[:, n:] & jnp.uint32(0xFFFF0000))


def _unpack_halves(p):
    lo = lax.bitcast_convert_type(p << 16, F32)
    hi = lax.bitcast_convert_type(p & jnp.uint32(0xFFFF0000), F32)
    return lo, hi


def _norm_kernel(x_ref, nw_ref, sc_ref, sh_ref, o_ref):
    gain = nw_ref[...] * (1.0 + sc_ref[0])
    shift = sh_ref[0]

    def body(r, carry):
        rows = pl.ds(pl.multiple_of(r * NORM_ROWS, NORM_ROWS), NORM_ROWS)
        o_ref[rows, :] = _modulated_norm(x_ref[rows, :], gain, shift).astype(o_ref.dtype)
        return carry

    lax.fori_loop(0, x_ref.shape[0] // NORM_ROWS, body, 0, unroll=NORM_UNROLL)


def _norm(x2, nw, scale, shift, seq):
    t, d = x2.shape
    tm = _pick(seq, (512, 256, 128))
    rows_per_batch = seq // tm
    return pl.pallas_call(
        _norm_kernel,
        out_shape=jax.ShapeDtypeStruct((t, d), BF16),
        grid=(t // tm,),
        in_specs=[pl.BlockSpec((tm, d), lambda i: (i, 0)),
                  pl.BlockSpec((1, d), lambda i: (0, 0)),
                  pl.BlockSpec((1, 1, d), lambda i: (i // rows_per_batch, 0, 0)),
                  pl.BlockSpec((1, 1, d), lambda i: (i // rows_per_batch, 0, 0))],
        out_specs=pl.BlockSpec((tm, d), lambda i: (i, 0)),
        compiler_params=_params(("parallel",), 40),
    )(x2, nw, scale, shift)


def _in_kernel(a_ref, w_ref, *refs):
    n_riders = (len(refs) - 1) // 2
    o_ref = refs[n_riders]
    o_ref[...] = jnp.dot(a_ref[...], w_ref[...], preferred_element_type=F32).astype(o_ref.dtype)
    for src, dst in zip(refs[:n_riders], refs[n_riders + 1:]):
        dst[...] = src[...].astype(BF16)


def _rider_steps(nrows, steps):
    n = 1 << (steps.bit_length() - 1)
    while nrows % n or (nrows // n) % 16:
        n //= 2
    return n


def _in_proj(h, w, riders):
    t, d = h.shape
    n = w.shape[1]
    tm = _pick(t, (1024, 512, 256, 128))
    tn = _pick(n, (512, 256, 128))
    nj = n // tn
    steps = (t // tm) * nj

    def rider_spec(arr):
        nsteps = _rider_steps(arr.shape[0], steps)
        return pl.BlockSpec((arr.shape[0] // nsteps, arr.shape[1]),
                            lambda i, j: (jnp.minimum(i * nj + j, nsteps - 1), 0))

    rider_specs = [rider_spec(r) for r in riders]
    outs = pl.pallas_call(
        _in_kernel,
        out_shape=tuple([jax.ShapeDtypeStruct((t, n), BF16)]
                        + [jax.ShapeDtypeStruct(r.shape, BF16) for r in riders]),
        grid=(t // tm, nj),
        in_specs=[pl.BlockSpec((tm, d), lambda i, j: (i, 0)),
                  pl.BlockSpec((d, tn), lambda i, j: (0, j))] + rider_specs,
        out_specs=tuple([pl.BlockSpec((tm, tn), lambda i, j: (i, j))] + rider_specs),
        compiler_params=_params(("arbitrary", "arbitrary"), 56),
    )(h, w, *riders)
    return outs[0], outs[1:]


def _head_sumsq(v, ones_bd):
    sq = v * v
    hi = sq.astype(BF16)
    lo = (sq - hi.astype(F32)).astype(BF16)
    return (jnp.dot(hi, ones_bd, preferred_element_type=F32)
            + jnp.dot(lo, ones_bd, preferred_element_type=F32))


def _head_rms_norm(v, w2, ones_bd):
    ss = _head_sumsq(v, ones_bd)
    return v * lax.rsqrt(ss * (1.0 / HEAD_DIM) + EPS) * w2


def _attn_kernel(sink_ref, q_ref, kvc_ref, kvp_ref, bias_ref, qw_ref, kw_ref, wg_ref, o_ref, wg_o):
    wg_o[...] = wg_ref[...].astype(BF16)

    blk = ATTN_BLOCK
    kvw = N_KV_HEADS * HEAD_DIM
    lane = lax.broadcasted_iota(I32, (1, LANES), 1)
    low_half = lane < HEAD_DIM
    r = lax.broadcasted_iota(I32, (LANES, LANES), 0) // HEAD_DIM
    c = lax.broadcasted_iota(I32, (LANES, LANES), 1) // HEAD_DIM
    ones_bd = jnp.where(r == c, 1.0, 0.0).astype(BF16)

    kv = jnp.concatenate([kvp_ref[0], kvc_ref[0]], axis=0).astype(F32)
    kw2 = kw_ref[...]
    sum_a = jnp.broadcast_to(jnp.where(low_half, 1.0, 0.0), (2 * blk, LANES))
    sum_b = jnp.broadcast_to(jnp.where(low_half, 0.0, 1.0), (2 * blk, LANES))

    k_bds, v_bds = [], []
    for g in range(N_KV_HEADS):
        chunk = g // 2
        kc = _head_rms_norm(kv[:, chunk * LANES:(chunk + 1) * LANES], kw2, ones_bd)
        vc = kv[:, kvw + chunk * LANES: kvw + (chunk + 1) * LANES]
        kr = pltpu.roll(kc, HEAD_DIM, axis=1)
        vr = pltpu.roll(vc, HEAD_DIM, axis=1)
        if g % 2 == 0:
            k_lo, k_hi, v_lo, v_hi = kc, kr, vc, vr
        else:
            k_lo, k_hi, v_lo, v_hi = kr, kc, vr, vc
        k_bds.append(jnp.concatenate([jnp.where(low_half, k_lo, 0.0),
                                      jnp.where(low_half, 0.0, k_hi)], axis=0).astype(BF16))
        v_bds.append(jnp.concatenate(
            [jnp.concatenate([jnp.where(low_half, v_lo, 0.0), sum_a], axis=1),
             jnp.concatenate([jnp.where(low_half, 0.0, v_hi), sum_b], axis=1)],
            axis=0).astype(BF16))

    pairs_per_kv = Q_PER_KV // 2
    for first in range(0, N_Q_HEADS // 2, ATTN_STAGE_PAIRS):
        pairs = range(first, first + ATTN_STAGE_PAIRS)
        scores = []
        for j in pairs:
            qp = q_ref[0, :, j * LANES:(j + 1) * LANES].astype(F32)
            qn = _head_rms_norm(qp, qw_ref[...], ones_bd).astype(BF16)
            s = lax.dot_general(qn, k_bds[j // pairs_per_kv], (((1,), (1,)), ((), ())),
                                preferred_element_type=F32)
            scores.append(s + bias_ref[0, j])
        probs = []
        for j, s in zip(pairs, scores):
            sink_a = sink_ref[2 * j]
            sink_b = sink_ref[2 * j + 1]
            m_a = jnp.maximum(jnp.max(s[:, :2 * blk], axis=-1, keepdims=True), sink_a)
            m_b = jnp.maximum(jnp.max(s[:, 2 * blk:], axis=-1, keepdims=True), sink_b)
            p = jnp.concatenate([jnp.exp2(s[:, :2 * blk] - m_a),
                                 jnp.exp2(s[:, 2 * blk:] - m_b)], axis=1).astype(BF16)
            esink = jnp.where(low_half, jnp.exp2(sink_a - m_a), jnp.exp2(sink_b - m_b))
            probs.append((p, esink))
        for j, (p, esink) in zip(pairs, probs):
            ol = jnp.dot(p, v_bds[j // pairs_per_kv], preferred_element_type=F32)
            o = ol[:, :LANES] / (ol[:, LANES:] + esink)
            o_ref[0, :, j * LANES:(j + 1) * LANES] = o.astype(o_ref.dtype)


def _attn_bias():
    blk = ATTN_BLOCK
    i = np.arange(blk)[:, None]
    j = np.arange(2 * blk)[None, :]
    dist = i + blk - j
    band = (dist >= 0) & (dist < WINDOW)
    slopes = (2.0 ** (-8.0 * np.arange(1, N_Q_HEADS + 1, dtype=np.float32) / N_Q_HEADS)).astype(np.float32)
    alibi = -slopes[:, None, None] * dist.astype(np.float32)[None] * np.float32(LOG2E)
    later = np.where(band[None], alibi, np.float32(NEG_INF))
    first = np.where((band & (j >= blk))[None], alibi, np.float32(NEG_INF))
    tab = np.stack([first, later]).astype(np.float32)
    tab = tab.reshape(2, N_Q_HEADS // 2, 2, blk, 2 * blk)
    return np.transpose(tab, (0, 1, 3, 2, 4)).reshape(2, N_Q_HEADS // 2, blk, 4 * blk)


def _attention(proj3, sinks, q_norm_w, k_norm_w, q_col, kv_col, rider):
    b, s, _ = proj3.shape
    aw = N_Q_HEADS * HEAD_DIM
    kvw2 = 2 * N_KV_HEADS * HEAD_DIM
    nb = s // ATTN_BLOCK
    bias = _attn_bias()
    qw2 = (jnp.tile(q_norm_w, 2) * (HEAD_DIM ** -0.5 * LOG2E)).reshape(1, LANES)
    sinks = sinks * LOG2E
    kw2 = jnp.tile(k_norm_w, 2).reshape(1, LANES)
    rider_spec = pl.BlockSpec((rider.shape[0] // (b * nb), rider.shape[1]),
                              lambda bi, n, sk: (bi * nb + n, 0))
    grid_spec = pltpu.PrefetchScalarGridSpec(
        num_scalar_prefetch=1,
        grid=(b, nb),
        in_specs=[pl.BlockSpec((1, ATTN_BLOCK, aw), lambda bi, n, sk: (bi, n, q_col // aw)),
                  pl.BlockSpec((1, ATTN_BLOCK, kvw2), lambda bi, n, sk: (bi, n, kv_col // kvw2)),
                  pl.BlockSpec((1, ATTN_BLOCK, kvw2),
                               lambda bi, n, sk: (bi, jnp.maximum(n - 1, 0), kv_col // kvw2)),
                  pl.BlockSpec((1, N_Q_HEADS // 2, ATTN_BLOCK, 4 * ATTN_BLOCK),
                               lambda bi, n, sk: (jnp.minimum(n, 1), 0, 0, 0)),
                  pl.BlockSpec((1, LANES), lambda bi, n, sk: (0, 0)),
                  pl.BlockSpec((1, LANES), lambda bi, n, sk: (0, 0)),
                  rider_spec],
        out_specs=(pl.BlockSpec((1, ATTN_BLOCK, aw), lambda bi, n, sk: (bi, n, 0)), rider_spec),
    )
    return pl.pallas_call(
        _attn_kernel,
        out_shape=(jax.ShapeDtypeStruct((b, s, aw), BF16), jax.ShapeDtypeStruct(rider.shape, BF16)),
        grid_spec=grid_spec,
        compiler_params=_params(("arbitrary", "arbitrary"), 56),
    )(sinks, proj3, proj3, proj3, bias, qw2, kw2, rider)


def _pool_kernel(pc_ref, ph_ref, w_ref, ps_ref, o_ref):
    n = pl.program_id(1)
    ts = pc_ref.shape[1]
    cg = w_ref.shape[1]
    cur = pc_ref[0].astype(F32)
    halo = jnp.where(n > 0, ph_ref[0].astype(F32), 0.0)
    ext = jnp.concatenate([halo, cur], axis=0)
    t1 = (lax.broadcasted_iota(I32, (ts, 1), 0) + n * ts + 1).astype(F32)
    for g, win in enumerate(POOL_WINDOWS):
        acc = ext[:, g * cg:(g + 1) * cg]
        k = 1
        while k < win:
            acc = acc + pltpu.roll(acc, k, axis=0)
            k *= 2
        mean = acc[POOL_HALO:] / jnp.minimum(t1, float(win))
        d = (mean - cur[:, g * cg:(g + 1) * cg]).astype(BF16)
        y = jnp.dot(d, w_ref[g], preferred_element_type=F32) * ps_ref[:, g * cg:(g + 1) * cg]
        o_ref[0, :, g * cg:(g + 1) * cg] = y.astype(o_ref.dtype)


def _pool(proj3, w_pool, pool_scale, p_col):
    b, s, _ = proj3.shape
    ng, cg, _ = w_pool.shape
    pw = ng * cg
    ts = _pick(s, (512, 256, 128))
    return pl.pallas_call(
        _pool_kernel,
        out_shape=jax.ShapeDtypeStruct((b, s, pw), BF16),
        grid=(b, s // ts),
        in_specs=[pl.BlockSpec((pl.Element(1), pl.Element(ts), pl.Element(pw)),
                               lambda bi, n: (bi, pl.multiple_of(n * ts, ts), p_col)),
                  pl.BlockSpec((pl.Element(1), pl.Element(POOL_HALO), pl.Element(pw)),
                               lambda bi, n: (bi, pl.multiple_of(jnp.maximum(n * ts - POOL_HALO, 0),
                                                                 POOL_HALO), p_col)),
                  pl.BlockSpec((ng, cg, cg), lambda bi, n: (0, 0, 0)),
                  pl.BlockSpec((1, pw), lambda bi, n: (0, 0))],
        out_specs=pl.BlockSpec((1, ts, pw), lambda bi, n: (bi, n, 0)),
        compiler_params=_params(("parallel", "arbitrary"), 40),
    )(proj3, proj3, w_pool, pool_scale)


def _mix_kernel(a_ref, b_ref, wa_ref, wb_ref, ga_ref, gb_ref, o_ref):
    ya = jnp.dot(a_ref[...], wa_ref[...], preferred_element_type=F32)
    yb = jnp.dot(b_ref[...], wb_ref[...], preferred_element_type=F32)
    ga = jax.nn.sigmoid(ga_ref[...].astype(F32))
    gb = jax.nn.sigmoid(gb_ref[...].astype(F32))
    o_ref[...] = (ga * ya + gb * yb).astype(o_ref.dtype)


def _mix(y_attn, y_pool, wa, wb, proj, ga_col, gb_col):
    t, aw = y_attn.shape
    pw = y_pool.shape[1]
    d = wa.shape[1]
    tm = _pick(t, (1024, 512, 256, 128))
    tn = _pick(d, (1024, 512, 256, 128))
    return pl.pallas_call(
        _mix_kernel,
        out_shape=jax.ShapeDtypeStruct((t, d), BF16),
        grid=(t // tm, d // tn),
        in_specs=[pl.BlockSpec((tm, aw), lambda i, j: (i, 0)),
                  pl.BlockSpec((tm, pw), lambda i, j: (i, 0)),
                  pl.BlockSpec((aw, tn), lambda i, j: (0, j)),
                  pl.BlockSpec((pw, tn), lambda i, j: (0, j)),
                  pl.BlockSpec((pl.Element(tm), pl.Element(tn)),
                               lambda i, j: (pl.multiple_of(i * tm, tm),
                                             pl.multiple_of(ga_col + j * tn, LANES))),
                  pl.BlockSpec((pl.Element(tm), pl.Element(tn)),
                               lambda i, j: (pl.multiple_of(i * tm, tm),
                                             pl.multiple_of(gb_col + j * tn, LANES)))],
        out_specs=pl.BlockSpec((tm, tn), lambda i, j: (i, j)),
        compiler_params=_params(("parallel", "arbitrary"), 56),
    )(y_attn, y_pool, wa, wb, proj, proj)


def _out_kernel(m_ref, w_ref, x_ref, g_ref, o_ref):
    y = jnp.dot(m_ref[...], w_ref[...], preferred_element_type=F32)
    o_ref[...] = x_ref[...] + g_ref[0] * y


def _out_proj(mixed, w, x2, gate, seq):
    t, d = x2.shape
    tm = _pick(seq, (1024, 512, 256, 128))
    tn = _pick(d, (1024, 512, 256, 128))
    rows_per_batch = seq // tm
    return pl.pallas_call(
        _out_kernel,
        out_shape=jax.ShapeDtypeStruct((t, d), F32),
        grid=(t // tm, d // tn),
        in_specs=[pl.BlockSpec((tm, d), lambda i, j: (i, 0)),
                  pl.BlockSpec((d, tn), lambda i, j: (0, j)),
                  pl.BlockSpec((tm, tn), lambda i, j: (i, j)),
                  pl.BlockSpec((1, 1, tn), lambda i, j: (i // rows_per_batch, 0, j))],
        out_specs=pl.BlockSpec((tm, tn), lambda i, j: (i, j)),
        compiler_params=_params(("parallel", "arbitrary"), 56),
    )(mixed, w, x2, gate)


def _first_argmax(v, iota, n):
    m = jnp.max(v, axis=0, keepdims=True)
    idx = jnp.min(jnp.where(v == m, iota, n), axis=0, keepdims=True)
    return m, idx


def _router_kernel(x_ref, nw_ref, sc_ref, sh_ref, w2_ref, br_ref,
                   hp_ref, mi_ref, mf_ref, cnt_ref, carry_ref, hb_ref):
    step = pl.program_id(0)
    tm, d = x_ref.shape

    @pl.when(step == 0)
    def _():
        carry_ref[...] = jnp.zeros_like(carry_ref)

    gain = nw_ref[...] * (1.0 + sc_ref[0])
    shift = sh_ref[0]

    def body(r, carry):
        rows = pl.ds(pl.multiple_of(r * NORM_ROWS, NORM_ROWS), NORM_ROWS)
        h = _modulated_norm(x_ref[rows, :], gain, shift)
        hb_ref[rows, :] = h.astype(BF16)
        hp_ref[rows, :] = _pack_halves(h)
        return carry

    lax.fori_loop(0, tm // NORM_ROWS, body, 0, unroll=NORM_UNROLL)

    l2 = jnp.dot(hb_ref[...], w2_ref[...], preferred_element_type=F32)
    lt = l2[:, :LANES] + l2[:, LANES:]
    logits = lt.T[:N_ROUTER] + br_ref[...]
    iota8 = lax.broadcasted_iota(I32, (N_GROUPS, tm), 0)
    gl = logits[:N_GROUPS]
    g_max, g_sel = _first_argmax(gl, iota8, N_GROUPS)
    p_group = 1.0 / jnp.sum(jnp.exp(gl - g_max), axis=0, keepdims=True)
    e_in = jnp.zeros((EXPERTS_PER_GROUP, tm), F32)
    for g in range(N_GROUPS):
        lo = N_GROUPS + g * EXPERTS_PER_GROUP
        e_in = e_in + jnp.where(g_sel == g, logits[lo:lo + EXPERTS_PER_GROUP], 0.0)
    v0, i0 = _first_argmax(e_in, iota8, EXPERTS_PER_GROUP)
    rest = jnp.where(iota8 == i0, -jnp.inf, e_in)
    v1, i1 = _first_argmax(rest, iota8, EXPERTS_PER_GROUP)
    t = jnp.exp(v1 - v0)
    w0 = p_group / (1.0 + t)
    w1 = p_group * t / (1.0 + t)
    e0 = g_sel * EXPERTS_PER_GROUP + i0
    e1 = g_sel * EXPERTS_PER_GROUP + i1

    iota_e = lax.broadcasted_iota(I32, (N_EXPERTS, tm), 0)
    hit0 = iota_e == e0
    hit1 = iota_e == e1
    onehot = jnp.where(hit0 | hit1, 1.0, 0.0).astype(BF16)
    rr = lax.broadcasted_iota(I32, (tm, tm), 0)
    cc = lax.broadcasted_iota(I32, (tm, tm), 1)
    before = jnp.where(rr < cc, 1.0, 0.0).astype(BF16)
    prior = jnp.dot(onehot, before, preferred_element_type=F32) + carry_ref[:, :1]
    r0 = jnp.sum(jnp.where(hit0, prior, 0.0), axis=0, keepdims=True)
    r1 = jnp.sum(jnp.where(hit1, prior, 0.0), axis=0, keepdims=True)
    carry_ref[...] += jnp.dot(onehot, jnp.ones((tm, LANES), BF16), preferred_element_type=F32)

    zi = jnp.zeros((1, tm), I32)
    zf = jnp.zeros((1, tm), F32)
    mi_ref[...] = jnp.concatenate([e0, e1, r0.astype(I32), r1.astype(I32), zi, zi, zi, zi], axis=0)
    mf_ref[...] = jnp.concatenate([w0, w1, zf, zf, zf, zf, zf, zf], axis=0)
    cnt_ref[...] = carry_ref[...]


def _router(x1, nw, scale, shift, w2, br, seq):
    t, d = x1.shape
    tm = _pick(seq, (512, 256, 128))
    rows_per_batch = seq // tm
    return pl.pallas_call(
        _router_kernel,
        out_shape=(jax.ShapeDtypeStruct((t, d // 2), U32),
                   jax.ShapeDtypeStruct((8, t), I32),
                   jax.ShapeDtypeStruct((8, t), F32),
                   jax.ShapeDtypeStruct((N_EXPERTS, LANES), F32)),
        grid=(t // tm,),
        in_specs=[pl.BlockSpec((tm, d), lambda i: (i, 0)),
                  pl.BlockSpec((1, d), lambda i: (0, 0)),
                  pl.BlockSpec((1, 1, d), lambda i: (i // rows_per_batch, 0, 0)),
                  pl.BlockSpec((1, 1, d), lambda i: (i // rows_per_batch, 0, 0)),
                  pl.BlockSpec((d, 2 * LANES), lambda i: (0, 0)),
                  pl.BlockSpec((N_ROUTER, 1), lambda i: (0, 0))],
        out_specs=(pl.BlockSpec((tm, d // 2), lambda i: (i, 0)),
                   pl.BlockSpec((8, tm), lambda i: (0, i)),
                   pl.BlockSpec((8, tm), lambda i: (0, i)),
                   pl.BlockSpec((N_EXPERTS, LANES), lambda i: (0, 0))),
        scratch_shapes=[pltpu.VMEM((N_EXPERTS, LANES), F32), pltpu.VMEM((tm, d), BF16)],
        compiler_params=_params(("arbitrary",), 56),
    )(x1, nw, scale, shift, w2, br)


COMBINE_ROWS = 8
PLAN_LANES = 2048
DMA_UNROLL = 4


def _plan_kernel(mi_ref, cnt_ref, dest_ref, tab_ref):
    cnt = cnt_ref[...]
    nb = jnp.floor((cnt + (EXPERT_BLOCK - 1)) * (1.0 / EXPERT_BLOCK))
    r = lax.broadcasted_iota(I32, (N_EXPERTS, N_EXPERTS), 0)
    c = lax.broadcasted_iota(I32, (N_EXPERTS, N_EXPERTS), 1)
    incl = jnp.where(c <= r, 1.0, 0.0).astype(BF16)
    bend = jnp.dot(incl, nb.astype(BF16), preferred_element_type=F32)
    bstart = bend - nb
    first_row = bstart[:, :1] * EXPERT_BLOCK

    tl = mi_ref.shape[1]
    iota_e = lax.broadcasted_iota(I32, (N_EXPERTS, tl), 0)
    for k in range(TOP_K):
        hit = iota_e == mi_ref[k:k + 1, :]
        base = jnp.sum(jnp.where(hit, first_row, 0.0), axis=0, keepdims=True)
        dest_ref[k:k + 1, :] = base.astype(I32) + mi_ref[TOP_K + k:TOP_K + k + 1, :]

    nl = tab_ref.shape[1]
    bidx = lax.broadcasted_iota(I32, (1, nl), 1).astype(F32)
    blk_e = jnp.sum(jnp.where(bend[:, :1] <= bidx, 1.0, 0.0), axis=0, keepdims=True)
    blk_e = jnp.minimum(blk_e, N_EXPERTS - 1.0)
    hit = lax.broadcasted_iota(I32, (N_EXPERTS, nl), 0) == blk_e.astype(I32)
    cnt_b = jnp.sum(jnp.where(hit, cnt[:, :1], 0.0), axis=0, keepdims=True)
    start_b = jnp.sum(jnp.where(hit, bstart[:, :1], 0.0), axis=0, keepdims=True)
    valid = jnp.clip(cnt_b - (bidx - start_b) * EXPERT_BLOCK, 0.0, float(EXPERT_BLOCK))
    n_used = jnp.broadcast_to(jnp.max(bend[:, :1], axis=0, keepdims=True), (1, nl))
    run_end = jnp.sum(jnp.where(hit, bend[:, :1], 0.0), axis=0, keepdims=True)
    zero = jnp.zeros((1, nl), F32)
    tab_ref[...] = jnp.concatenate([blk_e, valid, n_used, run_end, zero, zero, zero, zero],
                                   axis=0).astype(I32)


def _plan(meta_i, cnt, nblk):
    t = meta_i.shape[1]
    tl = _pick(t, (PLAN_LANES, 1024, 512, 256, 128))
    nl = pl.cdiv(nblk, LANES) * LANES
    return pl.pallas_call(
        _plan_kernel,
        out_shape=(jax.ShapeDtypeStruct((TOP_K, t), I32), jax.ShapeDtypeStruct((8, nl), I32)),
        grid=(t // tl,),
        in_specs=[pl.BlockSpec((8, tl), lambda i: (0, i)),
                  pl.BlockSpec((N_EXPERTS, LANES), lambda i: (0, 0))],
        out_specs=(pl.BlockSpec((TOP_K, tl), lambda i: (0, i)),
                   pl.BlockSpec((8, nl), lambda i: (0, 0))),
        compiler_params=_params(("arbitrary",), 32),
    )(meta_i, cnt)


def _dispatch_kernel(dest_ref, hp_ref, xs_ref, sem, *, chunk, t):
    base = pl.program_id(0) * chunk

    def issue(i, carry):
        for k in range(TOP_K):
            pltpu.make_async_copy(hp_ref.at[pl.ds(i, 1)],
                                  xs_ref.at[pl.ds(dest_ref[k * t + base + i], 1)],
                                  sem).start(priority=k)
        return carry

    lax.fori_loop(0, chunk, issue, 0, unroll=DMA_UNROLL)
    for k in range(TOP_K):
        pltpu.make_async_copy(hp_ref, xs_ref.at[pl.ds(0, chunk)], sem).wait()


def _dispatch(dest_flat, hp, cap):
    t, half = hp.shape
    chunk = _pick(t, (512, 256, 128))
    grid_spec = pltpu.PrefetchScalarGridSpec(
        num_scalar_prefetch=1,
        grid=(t // chunk,),
        in_specs=[pl.BlockSpec((chunk, half), lambda i, ds: (i, 0))],
        out_specs=pl.BlockSpec(memory_space=pl.ANY),
        scratch_shapes=[pltpu.SemaphoreType.DMA(())],
    )
    return pl.pallas_call(
        functools.partial(_dispatch_kernel, chunk=chunk, t=t),
        out_shape=jax.ShapeDtypeStruct((cap, half), U32),
        grid_spec=grid_spec,
        compiler_params=pltpu.CompilerParams(dimension_semantics=("arbitrary",),
                                             has_side_effects=True, vmem_limit_bytes=32 * MIB),
    )(dest_flat, hp)


def _ffn_kernel(be_ref, nv_ref, nu_ref, re_ref, xs_ref, wg_hbm, wu_hbm, wd_hbm, o_ref,
                wg_s, wu_s, wd_s, slot_ref, sem):
    b = pl.program_id(0)

    def weight_copies(e, slot):
        return (pltpu.make_async_copy(wg_hbm.at[e], wg_s.at[slot], sem.at[slot]),
                pltpu.make_async_copy(wu_hbm.at[e], wu_s.at[slot], sem.at[slot]),
                pltpu.make_async_copy(wd_hbm.at[e], wd_s.at[slot], sem.at[slot]))

    @pl.when(b < nu_ref[0])
    def _():
        e = be_ref[b]

        @pl.when(b == 0)
        def _():
            slot_ref[0] = 0
            for c in weight_copies(e, 0):
                c.start()

        first = jnp.logical_or(b == 0, be_ref[jnp.maximum(b - 1, 0)] != e)

        @pl.when(first)
        def _():
            slot = jnp.where(b == 0, 0, 1 - slot_ref[0])
            slot_ref[0] = slot
            for c in weight_copies(e, slot):
                c.wait()
            nxt = re_ref[b]

            @pl.when(nxt < nu_ref[0])
            def _():
                for c in weight_copies(be_ref[nxt], 1 - slot):
                    c.start()

        slot = slot_ref[0]
        blk, half = xs_ref.shape
        row = lax.broadcasted_iota(I32, (blk, 1), 0)
        xp = jnp.where(row < nv_ref[b], xs_ref[...], jnp.uint32(0))
        lo, hi = _unpack_halves(xp)
        lo = lo.astype(BF16)
        hi = hi.astype(BF16)
        a = (jnp.dot(lo, wg_s[slot, :half], preferred_element_type=F32)
             + jnp.dot(hi, wg_s[slot, half:], preferred_element_type=F32))
        u = (jnp.dot(lo, wu_s[slot, :half], preferred_element_type=F32)
             + jnp.dot(hi, wu_s[slot, half:], preferred_element_type=F32))
        mid = (a * jax.nn.sigmoid(a) * u).astype(BF16)
        o_ref[...] = _pack_halves(jnp.dot(mid, wd_s[slot], preferred_element_type=F32))


def _experts(blk_e, blk_valid, n_used, run_end, xs, wg, wu, wd):
    cap, half = xs.shape
    d = 2 * half
    ff = wg.shape[2]
    nblk = cap // EXPERT_BLOCK

    def row_map(b, be, nv, nu, re):
        return (jnp.minimum(b, nu[0] - 1), 0)

    grid_spec = pltpu.PrefetchScalarGridSpec(
        num_scalar_prefetch=4,
        grid=(nblk,),
        in_specs=[pl.BlockSpec((EXPERT_BLOCK, half), row_map),
                  pl.BlockSpec(memory_space=pl.ANY),
                  pl.BlockSpec(memory_space=pl.ANY),
                  pl.BlockSpec(memory_space=pl.ANY)],
        out_specs=pl.BlockSpec((EXPERT_BLOCK, half), row_map),
        scratch_shapes=[pltpu.VMEM((2, d, ff), BF16), pltpu.VMEM((2, d, ff), BF16),
                        pltpu.VMEM((2, ff, d), BF16), pltpu.SMEM((1,), I32),
                        pltpu.SemaphoreType.DMA((2,))],
    )
    return pl.pallas_call(
        _ffn_kernel,
        out_shape=jax.ShapeDtypeStruct((cap, half), U32),
        grid_spec=grid_spec,
        compiler_params=_params(("arbitrary",), 56),
    )(blk_e, blk_valid, n_used, run_end, xs, wg, wu, wd)


def _combine_kernel(dest_ref, x_ref, g_ref, w_ref, ys_ref, o_ref, ybuf, sem, *, chunk, t):
    i = pl.program_id(0)
    n = pl.num_programs(0)

    def issue(step, slot):
        base = step * chunk

        def body(r, carry):
            for k in range(TOP_K):
                pltpu.make_async_copy(ys_ref.at[pl.ds(dest_ref[k * t + base + r], 1)],
                                      ybuf.at[slot, k, pl.ds(r, 1)],
                                      sem.at[slot]).start(priority=k)
            return carry

        lax.fori_loop(0, chunk, body, 0, unroll=DMA_UNROLL)

    @pl.when(i == 0)
    def _():
        issue(0, 0)

    slot = i % 2

    @pl.when(i + 1 < n)
    def _():
        issue(i + 1, 1 - slot)

    for k in range(TOP_K):
        pltpu.make_async_copy(ys_ref.at[pl.ds(0, chunk)], ybuf.at[slot, k], sem.at[slot]).wait()
    half = ybuf.shape[-1]
    g = jnp.broadcast_to(g_ref[0], (COMBINE_ROWS, 2 * half))

    def widen(v):
        return jnp.concatenate([v] * (half // LANES), axis=1)

    def body(c, carry):
        rows = pl.ds(pl.multiple_of(c * COMBINE_ROWS, COMBINE_ROWS), COMBINE_ROWS)
        w0 = widen(w_ref[rows, :LANES])
        w1 = widen(w_ref[rows, LANES:])
        lo0, hi0 = _unpack_halves(ybuf[slot, 0, rows, :])
        lo1, hi1 = _unpack_halves(ybuf[slot, 1, rows, :])
        o_ref[rows, :half] = x_ref[rows, :half] + g[:, :half] * (w0 * lo0 + w1 * lo1)
        o_ref[rows, half:] = x_ref[rows, half:] + g[:, half:] * (w0 * hi0 + w1 * hi1)
        return carry

    lax.fori_loop(0, chunk // COMBINE_ROWS, body, 0, unroll=4)


def _combine(dest_flat, x1, gate, w_rows, ys, seq):
    t, d = x1.shape
    chunk = _pick(seq, (256, 128))
    rows_per_batch = seq // chunk
    grid_spec = pltpu.PrefetchScalarGridSpec(
        num_scalar_prefetch=1,
        grid=(t // chunk,),
        in_specs=[pl.BlockSpec((chunk, d), lambda i, ds: (i, 0)),
                  pl.BlockSpec((1, 1, d), lambda i, ds: (i // rows_per_batch, 0, 0)),
                  pl.BlockSpec((chunk, TOP_K * LANES), lambda i, ds: (i, 0)),
                  pl.BlockSpec(memory_space=pl.ANY)],
        out_specs=pl.BlockSpec((chunk, d), lambda i, ds: (i, 0)),
        scratch_shapes=[pltpu.VMEM((2, TOP_K, chunk, d // 2), U32),
                        pltpu.SemaphoreType.DMA((2,))],
    )
    return pl.pallas_call(
        functools.partial(_combine_kernel, chunk=chunk, t=t),
        out_shape=jax.ShapeDtypeStruct((t, d), F32),
        grid_spec=grid_spec,
        compiler_params=_params(("arbitrary",), 56),
    )(dest_flat, x1, gate, w_rows, ys)


def _layer(x, c, w_ada, b_ada, norm1_w, w_in, q_norm_w, k_norm_w, sinks, w_pool, pool_scale,
           w_attn_up, w_pool_up, w_out, norm2_w, w_router_group, b_router_group,
           w_router_expert, b_router_expert, w_gate, w_up, w_down):
    b, s, d = x.shape
    t = b * s
    aw = w_attn_up.shape[0]
    pw = w_pool_up.shape[0]
    kvw = N_KV_HEADS * HEAD_DIM
    x2 = x.reshape(t, d)

    c8 = jnp.zeros((8, d), F32).at[:b].set(c)
    ada = _ada(c8, w_ada, b_ada.reshape(1, 6 * d))[:b]
    shift1, scale1, gate1, shift2, scale2, gate2 = [a.reshape(b, 1, d) for a in jnp.split(ada, 6, axis=-1)]

    q_col, kv_col = 0, aw
    p_col = aw + 2 * kvw
    ga_col = p_col + pw
    gb_col = ga_col + d
    h = _norm(x2, norm1_w.reshape(1, d), scale1, shift1, s)
    ne, _, ff = w_gate.shape
    proj, (wu_b, wd_b, wa_b, wb_b, wo_b) = _in_proj(
        h, w_in.astype(BF16),
        [w_up.reshape(ne * d, ff), w_down.reshape(ne * ff, d), w_attn_up, w_pool_up, w_out])
    wu_b = wu_b.reshape(ne, d, ff)
    wd_b = wd_b.reshape(ne, ff, d)
    proj3 = proj.reshape(b, s, proj.shape[1])

    y_attn, wg_b = _attention(proj3, sinks, q_norm_w, k_norm_w, q_col, kv_col,
                              w_gate.reshape(ne * d, ff))
    y_attn = y_attn.reshape(t, aw)
    wg_b = wg_b.reshape(ne, d, ff)
    y_pool = _pool(proj3, w_pool.astype(BF16), pool_scale.reshape(1, pw), p_col).reshape(t, pw)
    mixed = _mix(y_attn, y_pool, wa_b, wb_b, proj, ga_col, gb_col)
    x1 = _out_proj(mixed, wo_b, x2, gate1, s)

    wr = jnp.concatenate([w_router_group, w_router_expert,
                          jnp.zeros((d, LANES - N_ROUTER), F32)], axis=1)
    wr_hi = wr.astype(BF16)
    wr_lo = (wr - wr_hi.astype(F32)).astype(BF16)
    wr2 = jnp.concatenate([wr_hi, wr_lo], axis=1)
    br = jnp.concatenate([b_router_group, b_router_expert]).reshape(N_ROUTER, 1)
    hp, meta_i, meta_f, cnt = _router(x1, norm2_w.reshape(1, d), scale2, shift2, wr2, br, s)

    cap = TOP_K * t + N_EXPERTS * EXPERT_BLOCK
    nblk = cap // EXPERT_BLOCK
    dest, tab = _plan(meta_i, cnt, nblk)
    dest_flat = dest.reshape(TOP_K * t)
    blk_e, blk_valid, n_used, run_end = tab[0, :nblk], tab[1, :nblk], tab[2, :1], tab[3, :nblk]

    xs = _dispatch(dest_flat, hp, cap)
    ys = _experts(blk_e, blk_valid, n_used, run_end, xs, wg_b, wu_b, wd_b)
    w_lanes = jnp.broadcast_to(meta_f[:TOP_K].T[:, :, None], (t, TOP_K, LANES)).reshape(t, TOP_K * LANES)
    out = _combine(dest_flat, x1, gate2, w_lanes, ys, s)
    return out.reshape(b, s, d)


def kernel(x, c, w_ada, b_ada, norm1_w, w_in, q_norm_w, k_norm_w, sinks, w_pool, pool_scale,
           w_attn_up, w_pool_up, w_out, norm2_w, w_router_group, b_router_group,
           w_router_expert, b_router_expert, w_gate, w_up, w_down):
    for l in range(w_ada.shape[0]):
        x = _layer(x, c, w_ada[l], b_ada[l], norm1_w[l], w_in[l], q_norm_w[l], k_norm_w[l],
                   sinks[l], w_pool[l], pool_scale[l], w_attn_up[l], w_pool_up[l], w_out[l],
                   norm2_w[l], w_router_group[l], b_router_group[l], w_router_expert[l],
                   b_router_expert[l], w_gate[l], w_up[l], w_down[l])
    return x
```

```python
import functools

import jax
import jax.numpy as jnp
import numpy as np
from jax import lax
from jax.experimental import pallas as pl
from jax.experimental.pallas import tpu as pltpu

F32 = jnp.float32
BF16 = jnp.bfloat16
U32 = jnp.uint32
I32 = jnp.int32

HEAD_DIM = 64
N_Q_HEADS = 32
N_KV_HEADS = 4
Q_PER_KV = N_Q_HEADS // N_KV_HEADS
ATTN_BLOCK = 128
WINDOW = 128
POOL_WINDOWS = (2, 4, 8, 16)
POOL_HALO = 16
N_GROUPS = 8
EXPERTS_PER_GROUP = 8
N_EXPERTS = N_GROUPS * EXPERTS_PER_GROUP
N_ROUTER = N_GROUPS + N_EXPERTS
TOP_K = 2
EPS = 1e-6
NEG_INF = -1e30
LOG2E = 1.4426950408889634
ATTN_STAGE_PAIRS = 16

LANES = 128
EXPERT_BLOCK = 256
MIB = 1024 * 1024


def _params(semantics, vmem_mib):
    return pltpu.CompilerParams(dimension_semantics=semantics, vmem_limit_bytes=vmem_mib * MIB)


def _pick(n, candidates):
    for c in candidates:
        if n % c == 0:
            return c
    return n


def _ada_kernel(c_ref, w_ref, b_ref, o_ref):
    c = c_ref[...]
    s = (c * jax.nn.sigmoid(c)).astype(BF16)
    o_ref[...] = jnp.dot(s, w_ref[...].astype(BF16), preferred_element_type=F32) + b_ref[...]


def _ada(c8, w, b):
    d, n = w.shape
    tn = _pick(n, (1024, 512, 256, 128))
    return pl.pallas_call(
        _ada_kernel,
        out_shape=jax.ShapeDtypeStruct((8, n), F32),
        grid=(n // tn,),
        in_specs=[pl.BlockSpec((8, d), lambda j: (0, 0)),
                  pl.BlockSpec((d, tn), lambda j: (0, j)),
                  pl.BlockSpec((1, tn), lambda j: (0, j))],
        out_specs=pl.BlockSpec((8, tn), lambda j: (0, j)),
        compiler_params=_params(("arbitrary",), 56),
    )(c8, w, b)


def _modulated_norm(x, gain, shift):
    ms = jnp.mean(x * x, axis=-1, keepdims=True)
    return x * lax.rsqrt(ms + EPS) * gain + shift


NORM_ROWS = 16
NORM_UNROLL = 4


def _pack_halves(v):
    n = v.shape[1] // 2
    bits = lax.bitcast_convert_type(v.astype(BF16).astype(F32), U32)
    return (bits[:, :n] >> 16) | (bits[:, n:] & jnp.uint32(0xFFFF0000))


def _unpack_halves(p):
    lo = lax.bitcast_convert_type(p << 16, F32)
    hi = lax.bitcast_convert_type(p & jnp.uint32(0xFFFF0000), F32)
    return lo, hi


def _norm_kernel(x_ref, nw_ref, sc_ref, sh_ref, o_ref):
    gain = nw_ref[...] * (1.0 + sc_ref[0])
    shift = sh_ref[0]

    def body(r, carry):
        rows = pl.ds(pl.multiple_of(r * NORM_ROWS, NORM_ROWS), NORM_ROWS)
        o_ref[rows, :] = _modulated_norm(x_ref[rows, :], gain, shift).astype(o_ref.dtype)
        return carry

    lax.fori_loop(0, x_ref.shape[0] // NORM_ROWS, body, 0, unroll=NORM_UNROLL)


def _norm(x2, nw, scale, shift, seq):
    t, d = x2.shape
    tm = _pick(seq, (512, 256, 128))
    rows_per_batch = seq // tm
    return pl.pallas_call(
        _norm_kernel,
        out_shape=jax.ShapeDtypeStruct((t, d), BF16),
        grid=(t // tm,),
        in_specs=[pl.BlockSpec((tm, d), lambda i: (i, 0)),
                  pl.BlockSpec((1, d), lambda i: (0, 0)),
                  pl.BlockSpec((1, 1, d), lambda i: (i // rows_per_batch, 0, 0)),
                  pl.BlockSpec((1, 1, d), lambda i: (i // rows_per_batch, 0, 0))],
        out_specs=pl.BlockSpec((tm, d), lambda i: (i, 0)),
        compiler_params=_params(("parallel",), 40),
    )(x2, nw, scale, shift)


def _in_kernel(a_ref, w_ref, *refs):
    n_riders = (len(refs) - 1) // 2
    o_ref = refs[n_riders]
    o_ref[...] = jnp.dot(a_ref[...], w_ref[...], preferred_element_type=F32).astype(o_ref.dtype)
    for src, dst in zip(refs[:n_riders], refs[n_riders + 1:]):
        dst[...] = src[...].astype(BF16)


def _rider_steps(nrows, steps):
    n = 1 << (steps.bit_length() - 1)
    while nrows % n or (nrows // n) % 16:
        n //= 2
    return n


def _in_proj(h, w, riders):
    t, d = h.shape
    n = w.shape[1]
    tm = _pick(t, (1024, 512, 256, 128))
    tn = _pick(n, (512, 256, 128))
    nj = n // tn
    steps = (t // tm) * nj

    def rider_spec(arr):
        nsteps = _rider_steps(arr.shape[0], steps)
        return pl.BlockSpec((arr.shape[0] // nsteps, arr.shape[1]),
                            lambda i, j: (jnp.minimum(i * nj + j, nsteps - 1), 0))

    rider_specs = [rider_spec(r) for r in riders]
    outs = pl.pallas_call(
        _in_kernel,
        out_shape=tuple([jax.ShapeDtypeStruct((t, n), BF16)]
                        + [jax.ShapeDtypeStruct(r.shape, BF16) for r in riders]),
        grid=(t // tm, nj),
        in_specs=[pl.BlockSpec((tm, d), lambda i, j: (i, 0)),
                  pl.BlockSpec((d, tn), lambda i, j: (0, j))] + rider_specs,
        out_specs=tuple([pl.BlockSpec((tm, tn), lambda i, j: (i, j))] + rider_specs),
        compiler_params=_params(("arbitrary", "arbitrary"), 56),
    )(h, w, *riders)
    return outs[0], outs[1:]


def _head_sumsq(v, ones_bd):
    sq = v * v
    hi = sq.astype(BF16)
    lo = (sq - hi.astype(F32)).astype(BF16)
    return (jnp.dot(hi, ones_bd, preferred_element_type=F32)
            + jnp.dot(lo, ones_bd, preferred_element_type=F32))


def _head_rms_norm(v, w2, ones_bd):
    ss = _head_sumsq(v, ones_bd)
    return v * lax.rsqrt(ss * (1.0 / HEAD_DIM) + EPS) * w2


def _attn_kernel(sink_ref, q_ref, kvc_ref, kvp_ref, bias_ref, qw_ref, kw_ref, wg_ref, o_ref, wg_o):
    wg_o[...] = wg_ref[...].astype(BF16)

    blk = ATTN_BLOCK
    kvw = N_KV_HEADS * HEAD_DIM
    lane = lax.broadcasted_iota(I32, (1, LANES), 1)
    low_half = lane < HEAD_DIM
    r = lax.broadcasted_iota(I32, (LANES, LANES), 0) // HEAD_DIM
    c = lax.broadcasted_iota(I32, (LANES, LANES), 1) // HEAD_DIM
    ones_bd = jnp.where(r == c, 1.0, 0.0).astype(BF16)

    kv = jnp.concatenate([kvp_ref[0], kvc_ref[0]], axis=0).astype(F32)
    kw2 = kw_ref[...]
    sum_a = jnp.broadcast_to(jnp.where(low_half, 1.0, 0.0), (2 * blk, LANES))
    sum_b = jnp.broadcast_to(jnp.where(low_half, 0.0, 1.0), (2 * blk, LANES))

    k_bds, v_bds = [], []
    for g in range(N_KV_HEADS):
        chunk = g // 2
        kc = _head_rms_norm(kv[:, chunk * LANES:(chunk + 1) * LANES], kw2, ones_bd)
        vc = kv[:, kvw + chunk * LANES: kvw + (chunk + 1) * LANES]
        kr = pltpu.roll(kc, HEAD_DIM, axis=1)
        vr = pltpu.roll(vc, HEAD_DIM, axis=1)
        if g % 2 == 0:
            k_lo, k_hi, v_lo, v_hi = kc, kr, vc, vr
        else:
            k_lo, k_hi, v_lo, v_hi = kr, kc, vr, vc
        k_bds.append(jnp.concatenate([jnp.where(low_half, k_lo, 0.0),
                                      jnp.where(low_half, 0.0, k_hi)], axis=0).astype(BF16))
        v_bds.append(jnp.concatenate(
            [jnp.concatenate([jnp.where(low_half, v_lo, 0.0), sum_a], axis=1),
             jnp.concatenate([jnp.where(low_half, 0.0, v_hi), sum_b], axis=1)],
            axis=0).astype(BF16))

    pairs_per_kv = Q_PER_KV // 2
    for first in range(0, N_Q_HEADS // 2, ATTN_STAGE_PAIRS):
        pairs = range(first, first + ATTN_STAGE_PAIRS)
        scores = []
        for j in pairs:
            qp = q_ref[0, :, j * LANES:(j + 1) * LANES].astype(F32)
            qn = _head_rms_norm(qp, qw_ref[...], ones_bd).astype(BF16)
            s = lax.dot_general(qn, k_bds[j // pairs_per_kv], (((1,), (1,)), ((), ())),
                                preferred_element_type=F32)
            scores.append(s + bias_ref[0, j])
        probs = []
        for j, s in zip(pairs, scores):
            sink_a = sink_ref[2 * j]
            sink_b = sink_ref[2 * j + 1]
            m_a = jnp.maximum(jnp.max(s[:, :2 * blk], axis=-1, keepdims=True), sink_a)
            m_b = jnp.maximum(jnp.max(s[:, 2 * blk:], axis=-1, keepdims=True), sink_b)
            p = jnp.concatenate([jnp.exp2(s[:, :2 * blk] - m_a),
                                 jnp.exp2(s[:, 2 * blk:] - m_b)], axis=1).astype(BF16)
            esink = jnp.where(low_half, jnp.exp2(sink_a - m_a), jnp.exp2(sink_b - m_b))
            probs.append((p, esink))
        for j, (p, esink) in zip(pairs, probs):
            ol = jnp.dot(p, v_bds[j // pairs_per_kv], preferred_element_type=F32)
            o = ol[:, :LANES] / (ol[:, LANES:] + esink)
            o_ref[0, :, j * LANES:(j + 1) * LANES] = o.astype(o_ref.dtype)


def _attn_bias():
    blk = ATTN_BLOCK
    i = np.arange(blk)[:, None]
    j = np.arange(2 * blk)[None, :]
    dist = i + blk - j
    band = (dist >= 0) & (dist < WINDOW)
    slopes = (2.0 ** (-8.0 * np.arange(1, N_Q_HEADS + 1, dtype=np.float32) / N_Q_HEADS)).astype(np.float32)
    alibi = -slopes[:, None, None] * dist.astype(np.float32)[None] * np.float32(LOG2E)
    later = np.where(band[None], alibi, np.float32(NEG_INF))
    first = np.where((band & (j >= blk))[None], alibi, np.float32(NEG_INF))
    tab = np.stack([first, later]).astype(np.float32)
    tab = tab.reshape(2, N_Q_HEADS // 2, 2, blk, 2 * blk)
    return np.transpose(tab, (0, 1, 3, 2, 4)).reshape(2, N_Q_HEADS // 2, blk, 4 * blk)


def _attention(proj3, sinks, q_norm_w, k_norm_w, q_col, kv_col, rider):
    b, s, _ = proj3.shape
    aw = N_Q_HEADS * HEAD_DIM
    kvw2 = 2 * N_KV_HEADS * HEAD_DIM
    nb = s // ATTN_BLOCK
    bias = _attn_bias()
    qw2 = (jnp.tile(q_norm_w, 2) * (HEAD_DIM ** -0.5 * LOG2E)).reshape(1, LANES)
    sinks = sinks * LOG2E
    kw2 = jnp.tile(k_norm_w, 2).reshape(1, LANES)
    rider_spec = pl.BlockSpec((rider.shape[0] // (b * nb), rider.shape[1]),
                              lambda bi, n, sk: (bi * nb + n, 0))
    grid_spec = pltpu.PrefetchScalarGridSpec(
        num_scalar_prefetch=1,
        grid=(b, nb),
        in_specs=[pl.BlockSpec((1, ATTN_BLOCK, aw), lambda bi, n, sk: (bi, n, q_col // aw)),
                  pl.BlockSpec((1, ATTN_BLOCK, kvw2), lambda bi, n, sk: (bi, n, kv_col // kvw2)),
                  pl.BlockSpec((1, ATTN_BLOCK, kvw2),
                               lambda bi, n, sk: (bi, jnp.maximum(n - 1, 0), kv_col // kvw2)),
                  pl.BlockSpec((1, N_Q_HEADS // 2, ATTN_BLOCK, 4 * ATTN_BLOCK),
                               lambda bi, n, sk: (jnp.minimum(n, 1), 0, 0, 0)),
                  pl.BlockSpec((1, LANES), lambda bi, n, sk: (0, 0)),
                  pl.BlockSpec((1, LANES), lambda bi, n, sk: (0, 0)),
                  rider_spec],
        out_specs=(pl.BlockSpec((1, ATTN_BLOCK, aw), lambda bi, n, sk: (bi, n, 0)), rider_spec),
    )
    return pl.pallas_call(
        _attn_kernel,
        out_shape=(jax.ShapeDtypeStruct((b, s, aw), BF16), jax.ShapeDtypeStruct(rider.shape, BF16)),
        grid_spec=grid_spec,
        compiler_params=_params(("arbitrary", "arbitrary"), 56),
    )(sinks, proj3, proj3, proj3, bias, qw2, kw2, rider)


def _pool_kernel(pc_ref, ph_ref, w_ref, ps_ref, o_ref):
    n = pl.program_id(1)
    ts = pc_ref.shape[1]
    cg = w_ref.shape[1]
    cur = pc_ref[0].astype(F32)
    halo = jnp.where(n > 0, ph_ref[0].astype(F32), 0.0)
    ext = jnp.concatenate([halo, cur], axis=0)
    t1 = (lax.broadcasted_iota(I32, (ts, 1), 0) + n * ts + 1).astype(F32)
    for g, win in enumerate(POOL_WINDOWS):
        acc = ext[:, g * cg:(g + 1) * cg]
        k = 1
        while k < win:
            acc = acc + pltpu.roll(acc, k, axis=0)
            k *= 2
        mean = acc[POOL_HALO:] / jnp.minimum(t1, float(win))
        d = (mean - cur[:, g * cg:(g + 1) * cg]).astype(BF16)
        y = jnp.dot(d, w_ref[g], preferred_element_type=F32) * ps_ref[:, g * cg:(g + 1) * cg]
        o_ref[0, :, g * cg:(g + 1) * cg] = y.astype(o_ref.dtype)


def _pool(proj3, w_pool, pool_scale, p_col):
    b, s, _ = proj3.shape
    ng, cg, _ = w_pool.shape
    pw = ng * cg
    ts = _pick(s, (512, 256, 128))
    return pl.pallas_call(
        _pool_kernel,
        out_shape=jax.ShapeDtypeStruct((b, s, pw), BF16),
        grid=(b, s // ts),
        in_specs=[pl.BlockSpec((pl.Element(1), pl.Element(ts), pl.Element(pw)),
                               lambda bi, n: (bi, pl.multiple_of(n * ts, ts), p_col)),
                  pl.BlockSpec((pl.Element(1), pl.Element(POOL_HALO), pl.Element(pw)),
                               lambda bi, n: (bi, pl.multiple_of(jnp.maximum(n * ts - POOL_HALO, 0),
                                                                 POOL_HALO), p_col)),
                  pl.BlockSpec((ng, cg, cg), lambda bi, n: (0, 0, 0)),
                  pl.BlockSpec((1, pw), lambda bi, n: (0, 0))],
        out_specs=pl.BlockSpec((1, ts, pw), lambda bi, n: (bi, n, 0)),
        compiler_params=_params(("parallel", "arbitrary"), 40),
    )(proj3, proj3, w_pool, pool_scale)


def _mix_kernel(a_ref, b_ref, wa_ref, wb_ref, ga_ref, gb_ref, o_ref):
    ya = jnp.dot(a_ref[...], wa_ref[...], preferred_element_type=F32)
    yb = jnp.dot(b_ref[...], wb_ref[...], preferred_element_type=F32)
    ga = jax.nn.sigmoid(ga_ref[...].astype(F32))
    gb = jax.nn.sigmoid(gb_ref[...].astype(F32))
    o_ref[...] = (ga * ya + gb * yb).astype(o_ref.dtype)


def _mix(y_attn, y_pool, wa, wb, proj, ga_col, gb_col):
    t, aw = y_attn.shape
    pw = y_pool.shape[1]
    d = wa.shape[1]
    tm = _pick(t, (1024, 512, 256, 128))
    tn = _pick(d, (1024, 512, 256, 128))
    return pl.pallas_call(
        _mix_kernel,
        out_shape=jax.ShapeDtypeStruct((t, d), BF16),
        grid=(t // tm, d // tn),
        in_specs=[pl.BlockSpec((tm, aw), lambda i, j: (i, 0)),
                  pl.BlockSpec((tm, pw), lambda i, j: (i, 0)),
                  pl.BlockSpec((aw, tn), lambda i, j: (0, j)),
                  pl.BlockSpec((pw, tn), lambda i, j: (0, j)),
                  pl.BlockSpec((pl.Element(tm), pl.Element(tn)),
                               lambda i, j: (pl.multiple_of(i * tm, tm),
                                             pl.multiple_of(ga_col + j * tn, LANES))),
                  pl.BlockSpec((pl.Element(tm), pl.Element(tn)),
                               lambda i, j: (pl.multiple_of(i * tm, tm),
                                             pl.multiple_of(gb_col + j * tn, LANES)))],
        out_specs=pl.BlockSpec((tm, tn), lambda i, j: (i, j)),
        compiler_params=_params(("parallel", "arbitrary"), 56),
    )(y_attn, y_pool, wa, wb, proj, proj)


def _out_kernel(m_ref, w_ref, x_ref, g_ref, o_ref):
    y = jnp.dot(m_ref[...], w_ref[...], preferred_element_type=F32)
    o_ref[...] = x_ref[...] + g_ref[0] * y


def _out_proj(mixed, w, x2, gate, seq):
    t, d = x2.shape
    tm = _pick(seq, (1024, 512, 256, 128))
    tn = _pick(d, (1024, 512, 256, 128))
    rows_per_batch = seq // tm
    return pl.pallas_call(
        _out_kernel,
        out_shape=jax.ShapeDtypeStruct((t, d), F32),
        grid=(t // tm, d // tn),
        in_specs=[pl.BlockSpec((tm, d), lambda i, j: (i, 0)),
                  pl.BlockSpec((d, tn), lambda i, j: (0, j)),
                  pl.BlockSpec((tm, tn), lambda i, j: (i, j)),
                  pl.BlockSpec((1, 1, tn), lambda i, j: (i // rows_per_batch, 0, j))],
        out_specs=pl.BlockSpec((tm, tn), lambda i, j: (i, j)),
        compiler_params=_params(("parallel", "arbitrary"), 56),
    )(mixed, w, x2, gate)


def _first_argmax(v, iota, n):
    m = jnp.max(v, axis=0, keepdims=True)
    idx = jnp.min(jnp.where(v == m, iota, n), axis=0, keepdims=True)
    return m, idx


def _router_kernel(x_ref, nw_ref, sc_ref, sh_ref, w2_ref, br_ref,
                   hp_ref, mi_ref, mf_ref, cnt_ref, carry_ref, hb_ref):
    step = pl.program_id(0)
    tm, d = x_ref.shape

    @pl.when(step == 0)
    def _():
        carry_ref[...] = jnp.zeros_like(carry_ref)

    gain = nw_ref[...] * (1.0 + sc_ref[0])
    shift = sh_ref[0]

    def body(r, carry):
        rows = pl.ds(pl.multiple_of(r * NORM_ROWS, NORM_ROWS), NORM_ROWS)
        h = _modulated_norm(x_ref[rows, :], gain, shift)
        hb_ref[rows, :] = h.astype(BF16)
        hp_ref[rows, :] = _pack_halves(h)
        return carry

    lax.fori_loop(0, tm // NORM_ROWS, body, 0, unroll=NORM_UNROLL)

    l2 = jnp.dot(hb_ref[...], w2_ref[...], preferred_element_type=F32)
    lt = l2[:, :LANES] + l2[:, LANES:]
    logits = lt.T[:N_ROUTER] + br_ref[...]
    iota8 = lax.broadcasted_iota(I32, (N_GROUPS, tm), 0)
    gl = logits[:N_GROUPS]
    g_max, g_sel = _first_argmax(gl, iota8, N_GROUPS)
    p_group = 1.0 / jnp.sum(jnp.exp(gl - g_max), axis=0, keepdims=True)
    e_in = jnp.zeros((EXPERTS_PER_GROUP, tm), F32)
    for g in range(N_GROUPS):
        lo = N_GROUPS + g * EXPERTS_PER_GROUP
        e_in = e_in + jnp.where(g_sel == g, logits[lo:lo + EXPERTS_PER_GROUP], 0.0)
    v0, i0 = _first_argmax(e_in, iota8, EXPERTS_PER_GROUP)
    rest = jnp.where(iota8 == i0, -jnp.inf, e_in)
    v1, i1 = _first_argmax(rest, iota8, EXPERTS_PER_GROUP)
    t = jnp.exp(v1 - v0)
    w0 = p_group / (1.0 + t)
    w1 = p_group * t / (1.0 + t)
    e0 = g_sel * EXPERTS_PER_GROUP + i0
    e1 = g_sel * EXPERTS_PER_GROUP + i1

    iota_e = lax.broadcasted_iota(I32, (N_EXPERTS, tm), 0)
    hit0 = iota_e == e0
    hit1 = iota_e == e1
    onehot = jnp.where(hit0 | hit1, 1.0, 0.0).astype(BF16)
    rr = lax.broadcasted_iota(I32, (tm, tm), 0)
    cc = lax.broadcasted_iota(I32, (tm, tm), 1)
    before = jnp.where(rr < cc, 1.0, 0.0).astype(BF16)
    prior = jnp.dot(onehot, before, preferred_element_type=F32) + carry_ref[:, :1]
    r0 = jnp.sum(jnp.where(hit0, prior, 0.0), axis=0, keepdims=True)
    r1 = jnp.sum(jnp.where(hit1, prior, 0.0), axis=0, keepdims=True)
    carry_ref[...] += jnp.dot(onehot, jnp.ones((tm, LANES), BF16), preferred_element_type=F32)

    zi = jnp.zeros((1, tm), I32)
    mi_ref[...] = jnp.concatenate([e0, e1, r0.astype(I32), r1.astype(I32), zi, zi, zi, zi], axis=0)
    row = lax.broadcasted_iota(I32, (LANES, tm), 0)
    wt = jnp.where(row == 0, w0, jnp.where(row == 1, w1, 0.0)).T
    mf_ref[:, :LANES] = jnp.broadcast_to(wt[:, 0:1], (tm, LANES))
    mf_ref[:, LANES:] = jnp.broadcast_to(wt[:, 1:2], (tm, LANES))
    cnt_ref[...] = carry_ref[...]


def _router(x1, nw, scale, shift, w2, br, seq):
    t, d = x1.shape
    tm = _pick(seq, (512, 256, 128))
    rows_per_batch = seq // tm
    return pl.pallas_call(
        _router_kernel,
        out_shape=(jax.ShapeDtypeStruct((t, d // 2), U32),
                   jax.ShapeDtypeStruct((8, t), I32),
                   jax.ShapeDtypeStruct((t, TOP_K * LANES), F32),
                   jax.ShapeDtypeStruct((N_EXPERTS, LANES), F32)),
        grid=(t // tm,),
        in_specs=[pl.BlockSpec((tm, d), lambda i: (i, 0)),
                  pl.BlockSpec((1, d), lambda i: (0, 0)),
                  pl.BlockSpec((1, 1, d), lambda i: (i // rows_per_batch, 0, 0)),
                  pl.BlockSpec((1, 1, d), lambda i: (i // rows_per_batch, 0, 0)),
                  pl.BlockSpec((d, 2 * LANES), lambda i: (0, 0)),
                  pl.BlockSpec((N_ROUTER, 1), lambda i: (0, 0))],
        out_specs=(pl.BlockSpec((tm, d // 2), lambda i: (i, 0)),
                   pl.BlockSpec((8, tm), lambda i: (0, i)),
                   pl.BlockSpec((tm, TOP_K * LANES), lambda i: (i, 0)),
                   pl.BlockSpec((N_EXPERTS, LANES), lambda i: (0, 0))),
        scratch_shapes=[pltpu.VMEM((N_EXPERTS, LANES), F32), pltpu.VMEM((tm, d), BF16)],
        compiler_params=_params(("arbitrary",), 56),
    )(x1, nw, scale, shift, w2, br)


COMBINE_ROWS = 8
PLAN_LANES = 2048
DMA_UNROLL = 8


def _plan_kernel(mi_ref, cnt_ref, dest_ref, tab_ref):
    cnt = cnt_ref[...]
    nb = jnp.floor((cnt + (EXPERT_BLOCK - 1)) * (1.0 / EXPERT_BLOCK))
    r = lax.broadcasted_iota(I32, (N_EXPERTS, N_EXPERTS), 0)
    c = lax.broadcasted_iota(I32, (N_EXPERTS, N_EXPERTS), 1)
    incl = jnp.where(c <= r, 1.0, 0.0).astype(BF16)
    bend = jnp.dot(incl, nb.astype(BF16), preferred_element_type=F32)
    bstart = bend - nb
    first_row = bstart[:, :1] * EXPERT_BLOCK

    tl = mi_ref.shape[1]
    iota_e = lax.broadcasted_iota(I32, (N_EXPERTS, tl), 0)
    for k in range(TOP_K):
        hit = iota_e == mi_ref[k:k + 1, :]
        base = jnp.sum(jnp.where(hit, first_row, 0.0), axis=0, keepdims=True)
        dest_ref[k:k + 1, :] = base.astype(I32) + mi_ref[TOP_K + k:TOP_K + k + 1, :]

    nl = tab_ref.shape[1]
    bidx = lax.broadcasted_iota(I32, (1, nl), 1).astype(F32)
    blk_e = jnp.sum(jnp.where(bend[:, :1] <= bidx, 1.0, 0.0), axis=0, keepdims=True)
    blk_e = jnp.minimum(blk_e, N_EXPERTS - 1.0)
    hit = lax.broadcasted_iota(I32, (N_EXPERTS, nl), 0) == blk_e.astype(I32)
    cnt_b = jnp.sum(jnp.where(hit, cnt[:, :1], 0.0), axis=0, keepdims=True)
    start_b = jnp.sum(jnp.where(hit, bstart[:, :1], 0.0), axis=0, keepdims=True)
    valid = jnp.clip(cnt_b - (bidx - start_b) * EXPERT_BLOCK, 0.0, float(EXPERT_BLOCK))
    n_used = jnp.broadcast_to(jnp.max(bend[:, :1], axis=0, keepdims=True), (1, nl))
    run_end = jnp.sum(jnp.where(hit, bend[:, :1], 0.0), axis=0, keepdims=True)
    zero = jnp.zeros((1, nl), F32)
    tab_ref[...] = jnp.concatenate([blk_e, valid, n_used, run_end, zero, zero, zero, zero],
                                   axis=0).astype(I32)


def _plan(meta_i, cnt, nblk):
    t = meta_i.shape[1]
    tl = _pick(t, (PLAN_LANES, 1024, 512, 256, 128))
    nl = pl.cdiv(nblk, LANES) * LANES
    return pl.pallas_call(
        _plan_kernel,
        out_shape=(jax.ShapeDtypeStruct((TOP_K, t), I32), jax.ShapeDtypeStruct((8, nl), I32)),
        grid=(t // tl,),
        in_specs=[pl.BlockSpec((8, tl), lambda i: (0, i)),
                  pl.BlockSpec((N_EXPERTS, LANES), lambda i: (0, 0))],
        out_specs=(pl.BlockSpec((TOP_K, tl), lambda i: (0, i)),
                   pl.BlockSpec((8, nl), lambda i: (0, 0))),
        compiler_params=_params(("arbitrary",), 32),
    )(meta_i, cnt)


def _dispatch_kernel(dest_ref, hp_ref, xs_ref, sem, *, chunk, t):
    base = pl.program_id(0) * chunk

    def issue(g, carry):
        row0 = pl.multiple_of(g * DMA_UNROLL, DMA_UNROLL)
        for u in range(DMA_UNROLL):
            for k in range(TOP_K):
                pltpu.make_async_copy(hp_ref.at[pl.ds(row0 + u, 1)],
                                      xs_ref.at[pl.ds(dest_ref[k * t + base + row0 + u], 1)],
                                      sem).start(priority=k)
        return carry

    lax.fori_loop(0, chunk // DMA_UNROLL, issue, 0)
    for k in range(TOP_K):
        pltpu.make_async_copy(hp_ref, xs_ref.at[pl.ds(0, chunk)], sem).wait()


def _dispatch(dest_flat, hp, cap):
    t, half = hp.shape
    chunk = _pick(t, (512, 256, 128))
    grid_spec = pltpu.PrefetchScalarGridSpec(
        num_scalar_prefetch=1,
        grid=(t // chunk,),
        in_specs=[pl.BlockSpec((chunk, half), lambda i, ds: (i, 0))],
        out_specs=pl.BlockSpec(memory_space=pl.ANY),
        scratch_shapes=[pltpu.SemaphoreType.DMA(())],
    )
    return pl.pallas_call(
        functools.partial(_dispatch_kernel, chunk=chunk, t=t),
        out_shape=jax.ShapeDtypeStruct((cap, half), U32),
        grid_spec=grid_spec,
        compiler_params=pltpu.CompilerParams(dimension_semantics=("arbitrary",),
                                             has_side_effects=True, vmem_limit_bytes=32 * MIB),
    )(dest_flat, hp)


def _ffn_kernel(be_ref, nv_ref, nu_ref, re_ref, xs_ref, wg_hbm, wu_hbm, wd_hbm, o_ref,
                wg_s, wu_s, wd_s, slot_ref, sem):
    b = pl.program_id(0)

    def weight_copies(e, slot):
        return (pltpu.make_async_copy(wg_hbm.at[e], wg_s.at[slot], sem.at[slot]),
                pltpu.make_async_copy(wu_hbm.at[e], wu_s.at[slot], sem.at[slot]),
                pltpu.make_async_copy(wd_hbm.at[e], wd_s.at[slot], sem.at[slot]))

    @pl.when(b < nu_ref[0])
    def _():
        e = be_ref[b]

        @pl.when(b == 0)
        def _():
            slot_ref[0] = 0
            for c in weight_copies(e, 0):
                c.start()

        first = jnp.logical_or(b == 0, be_ref[jnp.maximum(b - 1, 0)] != e)

        @pl.when(first)
        def _():
            slot = jnp.where(b == 0, 0, 1 - slot_ref[0])
            slot_ref[0] = slot
            for c in weight_copies(e, slot):
                c.wait()
            nxt = re_ref[b]

            @pl.when(nxt < nu_ref[0])
            def _():
                for c in weight_copies(be_ref[nxt], 1 - slot):
                    c.start()

        slot = slot_ref[0]
        blk, half = xs_ref.shape
        row = lax.broadcasted_iota(I32, (blk, 1), 0)
        xp = jnp.where(row < nv_ref[b], xs_ref[...], jnp.uint32(0))
        lo, hi = _unpack_halves(xp)
        lo = lo.astype(BF16)
        hi = hi.astype(BF16)
        a = (jnp.dot(lo, wg_s[slot, :half], preferred_element_type=F32)
             + jnp.dot(hi, wg_s[slot, half:], preferred_element_type=F32))
        u = (jnp.dot(lo, wu_s[slot, :half], preferred_element_type=F32)
             + jnp.dot(hi, wu_s[slot, half:], preferred_element_type=F32))
        mid = (a * jax.nn.sigmoid(a) * u).astype(BF16)
        o_ref[...] = _pack_halves(jnp.dot(mid, wd_s[slot], preferred_element_type=F32))


def _experts(blk_e, blk_valid, n_used, run_end, xs, wg, wu, wd):
    cap, half = xs.shape
    d = 2 * half
    ff = wg.shape[2]
    nblk = cap // EXPERT_BLOCK

    def row_map(b, be, nv, nu, re):
        return (jnp.minimum(b, nu[0] - 1), 0)

    grid_spec = pltpu.PrefetchScalarGridSpec(
        num_scalar_prefetch=4,
        grid=(nblk,),
        in_specs=[pl.BlockSpec((EXPERT_BLOCK, half), row_map),
                  pl.BlockSpec(memory_space=pl.ANY),
                  pl.BlockSpec(memory_space=pl.ANY),
                  pl.BlockSpec(memory_space=pl.ANY)],
        out_specs=pl.BlockSpec((EXPERT_BLOCK, half), row_map),
        scratch_shapes=[pltpu.VMEM((2, d, ff), BF16), pltpu.VMEM((2, d, ff), BF16),
                        pltpu.VMEM((2, ff, d), BF16), pltpu.SMEM((1,), I32),
                        pltpu.SemaphoreType.DMA((2,))],
    )
    return pl.pallas_call(
        _ffn_kernel,
        out_shape=jax.ShapeDtypeStruct((cap, half), U32),
        grid_spec=grid_spec,
        compiler_params=_params(("arbitrary",), 56),
    )(blk_e, blk_valid, n_used, run_end, xs, wg, wu, wd)


def _combine_kernel(dest_ref, x_ref, g_ref, w_ref, ys_ref, o_ref, ybuf, sem, *, chunk, t):
    i = pl.program_id(0)
    n = pl.num_programs(0)

    def issue(step, slot):
        base = step * chunk

        def body(g, carry):
            row0 = pl.multiple_of(g * DMA_UNROLL, DMA_UNROLL)
            for u in range(DMA_UNROLL):
                for k in range(TOP_K):
                    pltpu.make_async_copy(ys_ref.at[pl.ds(dest_ref[k * t + base + row0 + u], 1)],
                                          ybuf.at[slot, k, pl.ds(row0 + u, 1)],
                                          sem.at[slot]).start(priority=k)
            return carry

        lax.fori_loop(0, chunk // DMA_UNROLL, body, 0)

    @pl.when(i == 0)
    def _():
        issue(0, 0)

    slot = i % 2

    @pl.when(i + 1 < n)
    def _():
        issue(i + 1, 1 - slot)

    for k in range(TOP_K):
        pltpu.make_async_copy(ys_ref.at[pl.ds(0, chunk)], ybuf.at[slot, k], sem.at[slot]).wait()
    half = ybuf.shape[-1]
    g = jnp.broadcast_to(g_ref[0], (COMBINE_ROWS, 2 * half))

    def widen(v):
        return jnp.concatenate([v] * (half // LANES), axis=1)

    def body(c, carry):
        rows = pl.ds(pl.multiple_of(c * COMBINE_ROWS, COMBINE_ROWS), COMBINE_ROWS)
        w0 = widen(w_ref[rows, :LANES])
        w1 = widen(w_ref[rows, LANES:])
        lo0, hi0 = _unpack_halves(ybuf[slot, 0, rows, :])
        lo1, hi1 = _unpack_halves(ybuf[slot, 1, rows, :])
        o_ref[rows, :half] = x_ref[rows, :half] + g[:, :half] * (w0 * lo0 + w1 * lo1)
        o_ref[rows, half:] = x_ref[rows, half:] + g[:, half:] * (w0 * hi0 + w1 * hi1)
        return carry

    lax.fori_loop(0, chunk // COMBINE_ROWS, body, 0, unroll=4)


def _combine(dest_flat, x1, gate, w_rows, ys, seq):
    t, d = x1.shape
    chunk = _pick(seq, (256, 128))
    rows_per_batch = seq // chunk
    grid_spec = pltpu.PrefetchScalarGridSpec(
        num_scalar_prefetch=1,
        grid=(t // chunk,),
        in_specs=[pl.BlockSpec((chunk, d), lambda i, ds: (i, 0)),
                  pl.BlockSpec((1, 1, d), lambda i, ds: (i // rows_per_batch, 0, 0)),
                  pl.BlockSpec((chunk, TOP_K * LANES), lambda i, ds: (i, 0)),
                  pl.BlockSpec(memory_space=pl.ANY)],
        out_specs=pl.BlockSpec((chunk, d), lambda i, ds: (i, 0)),
        scratch_shapes=[pltpu.VMEM((2, TOP_K, chunk, d // 2), U32),
                        pltpu.SemaphoreType.DMA((2,))],
    )
    return pl.pallas_call(
        functools.partial(_combine_kernel, chunk=chunk, t=t),
        out_shape=jax.ShapeDtypeStruct((t, d), F32),
        grid_spec=grid_spec,
        compiler_params=_params(("arbitrary",), 56),
    )(dest_flat, x1, gate, w_rows, ys)


def _layer(x, c, w_ada, b_ada, norm1_w, w_in, q_norm_w, k_norm_w, sinks, w_pool, pool_scale,
           w_attn_up, w_pool_up, w_out, norm2_w, w_router_group, b_router_group,
           w_router_expert, b_router_expert, w_gate, w_up, w_down):
    b, s, d = x.shape
    t = b * s
    aw = w_attn_up.shape[0]
    pw = w_pool_up.shape[0]
    kvw = N_KV_HEADS * HEAD_DIM
    x2 = x.reshape(t, d)

    c8 = jnp.zeros((8, d), F32).at[:b].set(c)
    ada = _ada(c8, w_ada, b_ada.reshape(1, 6 * d))[:b]
    shift1, scale1, gate1, shift2, scale2, gate2 = [a.reshape(b, 1, d) for a in jnp.split(ada, 6, axis=-1)]

    q_col, kv_col = 0, aw
    p_col = aw + 2 * kvw
    ga_col = p_col + pw
    gb_col = ga_col + d
    h = _norm(x2, norm1_w.reshape(1, d), scale1, shift1, s)
    ne, _, ff = w_gate.shape
    proj, (wu_b, wd_b, wa_b, wb_b, wo_b) = _in_proj(
        h, w_in.astype(BF16),
        [w_up.reshape(ne * d, ff), w_down.reshape(ne * ff, d), w_attn_up, w_pool_up, w_out])
    wu_b = wu_b.reshape(ne, d, ff)
    wd_b = wd_b.reshape(ne, ff, d)
    proj3 = proj.reshape(b, s, proj.shape[1])

    y_attn, wg_b = _attention(proj3, sinks, q_norm_w, k_norm_w, q_col, kv_col,
                              w_gate.reshape(ne * d, ff))
    y_attn = y_attn.reshape(t, aw)
    wg_b = wg_b.reshape(ne, d, ff)
    y_pool = _pool(proj3, w_pool.astype(BF16), pool_scale.reshape(1, pw), p_col).reshape(t, pw)
    mixed = _mix(y_attn, y_pool, wa_b, wb_b, proj, ga_col, gb_col)
    x1 = _out_proj(mixed, wo_b, x2, gate1, s)

    wr = jnp.concatenate([w_router_group, w_router_expert,
                          jnp.zeros((d, LANES - N_ROUTER), F32)], axis=1)
    wr_hi = wr.astype(BF16)
    wr_lo = (wr - wr_hi.astype(F32)).astype(BF16)
    wr2 = jnp.concatenate([wr_hi, wr_lo], axis=1)
    br = jnp.concatenate([b_router_group, b_router_expert]).reshape(N_ROUTER, 1)
    hp, meta_i, w_lanes, cnt = _router(x1, norm2_w.reshape(1, d), scale2, shift2, wr2, br, s)

    cap = TOP_K * t + N_EXPERTS * EXPERT_BLOCK
    nblk = cap // EXPERT_BLOCK
    dest, tab = _plan(meta_i, cnt, nblk)
    dest_flat = dest.reshape(TOP_K * t)
    blk_e, blk_valid, n_used, run_end = tab[0, :nblk], tab[1, :nblk], tab[2, :1], tab[3, :nblk]

    xs = _dispatch(dest_flat, hp, cap)
    ys = _experts(blk_e, blk_valid, n_used, run_end, xs, wg_b, wu_b, wd_b)
    out = _combine(dest_flat, x1, gate2, w_lanes, ys, s)
    return out.reshape(b, s, d)


def kernel(x, c, w_ada, b_ada, norm1_w, w_in, q_norm_w, k_norm_w, sinks, w_pool, pool_scale,
           w_attn_up, w_pool_up, w_out, norm2_w, w_router_group, b_router_group,
           w_router_expert, b_router_expert, w_gate, w_up, w_down):
    for l in range(w_ada.shape[0]):
        x = _layer(x, c, w_ada[l], b_ada[l], norm1_w[l], w_in[l], q_norm_w[l], k_norm_w[l],
                   sinks[l], w_pool[l], pool_scale[l], w_attn_up[l], w_pool_up[l], w_out[l],
                   norm2_w[l], w_router_group[l], b_router_group[l], w_router_expert[l],
                   b_router_expert[l], w_gate[l], w_up[l], w_down[l])
    return x
```

```python
import functools

import jax
import jax.numpy as jnp
import numpy as np
from jax import lax
from jax.experimental import pallas as pl
from jax.experimental.pallas import tpu as pltpu

F32 = jnp.float32
BF16 = jnp.bfloat16
U32 = jnp.uint32
I32 = jnp.int32

HEAD_DIM = 64
N_Q_HEADS = 32
N_KV_HEADS = 4
Q_PER_KV = N_Q_HEADS // N_KV_HEADS
ATTN_BLOCK = 128
WINDOW = 128
POOL_WINDOWS = (2, 4, 8, 16)
POOL_HALO = 16
N_GROUPS = 8
EXPERTS_PER_GROUP = 8
N_EXPERTS = N_GROUPS * EXPERTS_PER_GROUP
N_ROUTER = N_GROUPS + N_EXPERTS
TOP_K = 2
EPS = 1e-6
NEG_INF = -1e30
LOG2E = 1.4426950408889634
ATTN_STAGE_PAIRS = 16

LANES = 128
EXPERT_BLOCK = 256
MIB = 1024 * 1024


def _params(semantics, vmem_mib):
    return pltpu.CompilerParams(dimension_semantics=semantics, vmem_limit_bytes=vmem_mib * MIB)


def _pick(n, candidates):
    for c in candidates:
        if n % c == 0:
            return c
    return n


def _ada_kernel(c_ref, w_ref, b_ref, o_ref):
    c = c_ref[...]
    s = (c * jax.nn.sigmoid(c)).astype(BF16)
    o_ref[...] = jnp.dot(s, w_ref[...].astype(BF16), preferred_element_type=F32) + b_ref[...]


def _ada(c8, w, b):
    d, n = w.shape
    tn = _pick(n, (1024, 512, 256, 128))
    return pl.pallas_call(
        _ada_kernel,
        out_shape=jax.ShapeDtypeStruct((8, n), F32),
        grid=(n // tn,),
        in_specs=[pl.BlockSpec((8, d), lambda j: (0, 0)),
                  pl.BlockSpec((d, tn), lambda j: (0, j)),
                  pl.BlockSpec((1, tn), lambda j: (0, j))],
        out_specs=pl.BlockSpec((8, tn), lambda j: (0, j)),
        compiler_params=_params(("arbitrary",), 56),
    )(c8, w, b)


def _modulated_norm(x, gain, shift):
    ms = jnp.mean(x * x, axis=-1, keepdims=True)
    return x * lax.rsqrt(ms + EPS) * gain + shift


NORM_ROWS = 16
NORM_UNROLL = 4


def _pack_halves(v):
    n = v.shape[1] // 2
    bits = lax.bitcast_convert_type(v.astype(BF16).astype(F32), U32)
    return (bits[:, :n] >> 16) | (bits[:, n:] & jnp.uint32(0xFFFF0000))


def _unpack_halves(p):
    lo = lax.bitcast_convert_type(p << 16, F32)
    hi = lax.bitcast_convert_type(p & jnp.uint32(0xFFFF0000), F32)
    return lo, hi


def _norm_kernel(x_ref, nw_ref, sc_ref, sh_ref, o_ref):
    gain = nw_ref[...] * (1.0 + sc_ref[0])
    shift = sh_ref[0]

    def body(r, carry):
        rows = pl.ds(pl.multiple_of(r * NORM_ROWS, NORM_ROWS), NORM_ROWS)
        o_ref[rows, :] = _modulated_norm(x_ref[rows, :], gain, shift).astype(o_ref.dtype)
        return carry

    lax.fori_loop(0, x_ref.shape[0] // NORM_ROWS, body, 0, unroll=NORM_UNROLL)


def _norm(x2, nw, scale, shift, seq):
    t, d = x2.shape
    tm = _pick(seq, (512, 256, 128))
    rows_per_batch = seq // tm
    return pl.pallas_call(
        _norm_kernel,
        out_shape=jax.ShapeDtypeStruct((t, d), BF16),
        grid=(t // tm,),
        in_specs=[pl.BlockSpec((tm, d), lambda i: (i, 0)),
                  pl.BlockSpec((1, d), lambda i: (0, 0)),
                  pl.BlockSpec((1, 1, d), lambda i: (i // rows_per_batch, 0, 0)),
                  pl.BlockSpec((1, 1, d), lambda i: (i // rows_per_batch, 0, 0))],
        out_specs=pl.BlockSpec((tm, d), lambda i: (i, 0)),
        compiler_params=_params(("parallel",), 40),
    )(x2, nw, scale, shift)


def _in_kernel(a_ref, w_ref, *refs):
    n_riders = (len(refs) - 1) // 2
    o_ref = refs[n_riders]
    o_ref[...] = jnp.dot(a_ref[...], w_ref[...], preferred_element_type=F32).astype(o_ref.dtype)
    for src, dst in zip(refs[:n_riders], refs[n_riders + 1:]):
        dst[...] = src[...].astype(BF16)


def _rider_steps(nrows, steps):
    n = 1 << (steps.bit_length() - 1)
    while nrows % n or (nrows // n) % 16:
        n //= 2
    return n


def _in_proj(h, w, riders):
    t, d = h.shape
    n = w.shape[1]
    tm = _pick(t, (1024, 512, 256, 128))
    tn = _pick(n, (512, 256, 128))
    nj = n // tn
    steps = (t // tm) * nj

    def rider_spec(arr):
        nsteps = _rider_steps(arr.shape[0], steps)
        return pl.BlockSpec((arr.shape[0] // nsteps, arr.shape[1]),
                            lambda i, j: (jnp.minimum(i * nj + j, nsteps - 1), 0))

    rider_specs = [rider_spec(r) for r in riders]
    outs = pl.pallas_call(
        _in_kernel,
        out_shape=tuple([jax.ShapeDtypeStruct((t, n), BF16)]
                        + [jax.ShapeDtypeStruct(r.shape, BF16) for r in riders]),
        grid=(t // tm, nj),
        in_specs=[pl.BlockSpec((tm, d), lambda i, j: (i, 0)),
                  pl.BlockSpec((d, tn), lambda i, j: (0, j))] + rider_specs,
        out_specs=tuple([pl.BlockSpec((tm, tn), lambda i, j: (i, j))] + rider_specs),
        compiler_params=_params(("arbitrary", "arbitrary"), 56),
    )(h, w, *riders)
    return outs[0], outs[1:]


def _head_sumsq(v, ones_bd):
    sq = v * v
    hi = sq.astype(BF16)
    lo = (sq - hi.astype(F32)).astype(BF16)
    return (jnp.dot(hi, ones_bd, preferred_element_type=F32)
            + jnp.dot(lo, ones_bd, preferred_element_type=F32))


def _head_rms_norm(v, w2, ones_bd):
    ss = _head_sumsq(v, ones_bd)
    return v * lax.rsqrt(ss * (1.0 / HEAD_DIM) + EPS) * w2


def _attn_kernel(sink_ref, q_ref, kvc_ref, kvp_ref, bias_ref, qw_ref, kw_ref, wg_ref, o_ref, wg_o):
    wg_o[...] = wg_ref[...].astype(BF16)

    blk = ATTN_BLOCK
    kvw = N_KV_HEADS * HEAD_DIM
    lane = lax.broadcasted_iota(I32, (1, LANES), 1)
    low_half = lane < HEAD_DIM
    r = lax.broadcasted_iota(I32, (LANES, LANES), 0) // HEAD_DIM
    c = lax.broadcasted_iota(I32, (LANES, LANES), 1) // HEAD_DIM
    ones_bd = jnp.where(r == c, 1.0, 0.0).astype(BF16)

    kv = jnp.concatenate([kvp_ref[0], kvc_ref[0]], axis=0).astype(F32)
    kw2 = kw_ref[...]
    sum_a = jnp.broadcast_to(jnp.where(low_half, 1.0, 0.0), (2 * blk, LANES))
    sum_b = jnp.broadcast_to(jnp.where(low_half, 0.0, 1.0), (2 * blk, LANES))

    k_bds, v_bds = [], []
    for g in range(N_KV_HEADS):
        chunk = g // 2
        kc = _head_rms_norm(kv[:, chunk * LANES:(chunk + 1) * LANES], kw2, ones_bd)
        vc = kv[:, kvw + chunk * LANES: kvw + (chunk + 1) * LANES]
        kr = pltpu.roll(kc, HEAD_DIM, axis=1)
        vr = pltpu.roll(vc, HEAD_DIM, axis=1)
        if g % 2 == 0:
            k_lo, k_hi, v_lo, v_hi = kc, kr, vc, vr
        else:
            k_lo, k_hi, v_lo, v_hi = kr, kc, vr, vc
        k_bds.append(jnp.concatenate([jnp.where(low_half, k_lo, 0.0),
                                      jnp.where(low_half, 0.0, k_hi)], axis=0).astype(BF16))
        v_bds.append(jnp.concatenate(
            [jnp.concatenate([jnp.where(low_half, v_lo, 0.0), sum_a], axis=1),
             jnp.concatenate([jnp.where(low_half, 0.0, v_hi), sum_b], axis=1)],
            axis=0).astype(BF16))

    pairs_per_kv = Q_PER_KV // 2
    for first in range(0, N_Q_HEADS // 2, ATTN_STAGE_PAIRS):
        pairs = range(first, first + ATTN_STAGE_PAIRS)
        scores = []
        for j in pairs:
            qp = q_ref[0, :, j * LANES:(j + 1) * LANES].astype(F32)
            qn = _head_rms_norm(qp, qw_ref[...], ones_bd).astype(BF16)
            s = lax.dot_general(qn, k_bds[j // pairs_per_kv], (((1,), (1,)), ((), ())),
                                preferred_element_type=F32)
            scores.append(s + bias_ref[0, j])
        probs = []
        for j, s in zip(pairs, scores):
            sink_a = sink_ref[2 * j]
            sink_b = sink_ref[2 * j + 1]
            m_a = jnp.maximum(jnp.max(s[:, :2 * blk], axis=-1, keepdims=True), sink_a)
            m_b = jnp.maximum(jnp.max(s[:, 2 * blk:], axis=-1, keepdims=True), sink_b)
            p = jnp.concatenate([jnp.exp2(s[:, :2 * blk] - m_a),
                                 jnp.exp2(s[:, 2 * blk:] - m_b)], axis=1).astype(BF16)
            esink = jnp.where(low_half, jnp.exp2(sink_a - m_a), jnp.exp2(sink_b - m_b))
            probs.append((p, esink))
        for j, (p, esink) in zip(pairs, probs):
            ol = jnp.dot(p, v_bds[j // pairs_per_kv], preferred_element_type=F32)
            o = ol[:, :LANES] / (ol[:, LANES:] + esink)
            o_ref[0, :, j * LANES:(j + 1) * LANES] = o.astype(o_ref.dtype)


def _attn_bias():
    blk = ATTN_BLOCK
    i = np.arange(blk)[:, None]
    j = np.arange(2 * blk)[None, :]
    dist = i + blk - j
    band = (dist >= 0) & (dist < WINDOW)
    slopes = (2.0 ** (-8.0 * np.arange(1, N_Q_HEADS + 1, dtype=np.float32) / N_Q_HEADS)).astype(np.float32)
    alibi = -slopes[:, None, None] * dist.astype(np.float32)[None] * np.float32(LOG2E)
    later = np.where(band[None], alibi, np.float32(NEG_INF))
    first = np.where((band & (j >= blk))[None], alibi, np.float32(NEG_INF))
    tab = np.stack([first, later]).astype(np.float32)
    tab = tab.reshape(2, N_Q_HEADS // 2, 2, blk, 2 * blk)
    return np.transpose(tab, (0, 1, 3, 2, 4)).reshape(2, N_Q_HEADS // 2, blk, 4 * blk)


def _attention(proj3, sinks, q_norm_w, k_norm_w, q_col, kv_col, rider):
    b, s, _ = proj3.shape
    aw = N_Q_HEADS * HEAD_DIM
    kvw2 = 2 * N_KV_HEADS * HEAD_DIM
    nb = s // ATTN_BLOCK
    bias = _attn_bias()
    qw2 = (jnp.tile(q_norm_w, 2) * (HEAD_DIM ** -0.5 * LOG2E)).reshape(1, LANES)
    sinks = sinks * LOG2E
    kw2 = jnp.tile(k_norm_w, 2).reshape(1, LANES)
    rider_spec = pl.BlockSpec((rider.shape[0] // (b * nb), rider.shape[1]),
                              lambda bi, n, sk: (bi * nb + n, 0))
    grid_spec = pltpu.PrefetchScalarGridSpec(
        num_scalar_prefetch=1,
        grid=(b, nb),
        in_specs=[pl.BlockSpec((1, ATTN_BLOCK, aw), lambda bi, n, sk: (bi, n, q_col // aw)),
                  pl.BlockSpec((1, ATTN_BLOCK, kvw2), lambda bi, n, sk: (bi, n, kv_col // kvw2)),
                  pl.BlockSpec((1, ATTN_BLOCK, kvw2),
                               lambda bi, n, sk: (bi, jnp.maximum(n - 1, 0), kv_col // kvw2)),
                  pl.BlockSpec((1, N_Q_HEADS // 2, ATTN_BLOCK, 4 * ATTN_BLOCK),
                               lambda bi, n, sk: (jnp.minimum(n, 1), 0, 0, 0)),
                  pl.BlockSpec((1, LANES), lambda bi, n, sk: (0, 0)),
                  pl.BlockSpec((1, LANES), lambda bi, n, sk: (0, 0)),
                  rider_spec],
        out_specs=(pl.BlockSpec((1, ATTN_BLOCK, aw), lambda bi, n, sk: (bi, n, 0)), rider_spec),
    )
    return pl.pallas_call(
        _attn_kernel,
        out_shape=(jax.ShapeDtypeStruct((b, s, aw), BF16), jax.ShapeDtypeStruct(rider.shape, BF16)),
        grid_spec=grid_spec,
        compiler_params=_params(("arbitrary", "arbitrary"), 56),
    )(sinks, proj3, proj3, proj3, bias, qw2, kw2, rider)


def _pool_kernel(pc_ref, ph_ref, w_ref, ps_ref, o_ref):
    n = pl.program_id(1)
    ts = pc_ref.shape[1]
    cg = w_ref.shape[1]
    cur = pc_ref[0].astype(F32)
    halo = jnp.where(n > 0, ph_ref[0].astype(F32), 0.0)
    ext = jnp.concatenate([halo, cur], axis=0)
    t1 = (lax.broadcasted_iota(I32, (ts, 1), 0) + n * ts + 1).astype(F32)
    for g, win in enumerate(POOL_WINDOWS):
        acc = ext[:, g * cg:(g + 1) * cg]
        k = 1
        while k < win:
            acc = acc + pltpu.roll(acc, k, axis=0)
            k *= 2
        mean = acc[POOL_HALO:] / jnp.minimum(t1, float(win))
        d = (mean - cur[:, g * cg:(g + 1) * cg]).astype(BF16)
        y = jnp.dot(d, w_ref[g], preferred_element_type=F32) * ps_ref[:, g * cg:(g + 1) * cg]
        o_ref[0, :, g * cg:(g + 1) * cg] = y.astype(o_ref.dtype)


def _pool(proj3, w_pool, pool_scale, p_col):
    b, s, _ = proj3.shape
    ng, cg, _ = w_pool.shape
    pw = ng * cg
    ts = _pick(s, (512, 256, 128))
    return pl.pallas_call(
        _pool_kernel,
        out_shape=jax.ShapeDtypeStruct((b, s, pw), BF16),
        grid=(b, s // ts),
        in_specs=[pl.BlockSpec((pl.Element(1), pl.Element(ts), pl.Element(pw)),
                               lambda bi, n: (bi, pl.multiple_of(n * ts, ts), p_col)),
                  pl.BlockSpec((pl.Element(1), pl.Element(POOL_HALO), pl.Element(pw)),
                               lambda bi, n: (bi, pl.multiple_of(jnp.maximum(n * ts - POOL_HALO, 0),
                                                                 POOL_HALO), p_col)),
                  pl.BlockSpec((ng, cg, cg), lambda bi, n: (0, 0, 0)),
                  pl.BlockSpec((1, pw), lambda bi, n: (0, 0))],
        out_specs=pl.BlockSpec((1, ts, pw), lambda bi, n: (bi, n, 0)),
        compiler_params=_params(("parallel", "arbitrary"), 40),
    )(proj3, proj3, w_pool, pool_scale)


def _mix_kernel(a_ref, b_ref, wa_ref, wb_ref, ga_ref, gb_ref, o_ref):
    ya = jnp.dot(a_ref[...], wa_ref[...], preferred_element_type=F32)
    yb = jnp.dot(b_ref[...], wb_ref[...], preferred_element_type=F32)
    ga = jax.nn.sigmoid(ga_ref[...].astype(F32))
    gb = jax.nn.sigmoid(gb_ref[...].astype(F32))
    o_ref[...] = (ga * ya + gb * yb).astype(o_ref.dtype)


def _mix(y_attn, y_pool, wa, wb, proj, ga_col, gb_col):
    t, aw = y_attn.shape
    pw = y_pool.shape[1]
    d = wa.shape[1]
    tm = _pick(t, (1024, 512, 256, 128))
    tn = _pick(d, (1024, 512, 256, 128))
    return pl.pallas_call(
        _mix_kernel,
        out_shape=jax.ShapeDtypeStruct((t, d), BF16),
        grid=(t // tm, d // tn),
        in_specs=[pl.BlockSpec((tm, aw), lambda i, j: (i, 0)),
                  pl.BlockSpec((tm, pw), lambda i, j: (i, 0)),
                  pl.BlockSpec((aw, tn), lambda i, j: (0, j)),
                  pl.BlockSpec((pw, tn), lambda i, j: (0, j)),
                  pl.BlockSpec((pl.Element(tm), pl.Element(tn)),
                               lambda i, j: (pl.multiple_of(i * tm, tm),
                                             pl.multiple_of(ga_col + j * tn, LANES))),
                  pl.BlockSpec((pl.Element(tm), pl.Element(tn)),
                               lambda i, j: (pl.multiple_of(i * tm, tm),
                                             pl.multiple_of(gb_col + j * tn, LANES)))],
        out_specs=pl.BlockSpec((tm, tn), lambda i, j: (i, j)),
        compiler_params=_params(("parallel", "arbitrary"), 56),
    )(y_attn, y_pool, wa, wb, proj, proj)


def _out_kernel(m_ref, w_ref, x_ref, g_ref, o_ref):
    y = jnp.dot(m_ref[...], w_ref[...], preferred_element_type=F32)
    o_ref[...] = x_ref[...] + g_ref[0] * y


def _out_proj(mixed, w, x2, gate, seq):
    t, d = x2.shape
    tm = _pick(seq, (1024, 512, 256, 128))
    tn = _pick(d, (1024, 512, 256, 128))
    rows_per_batch = seq // tm
    return pl.pallas_call(
        _out_kernel,
        out_shape=jax.ShapeDtypeStruct((t, d), F32),
        grid=(t // tm, d // tn),
        in_specs=[pl.BlockSpec((tm, d), lambda i, j: (i, 0)),
                  pl.BlockSpec((d, tn), lambda i, j: (0, j)),
                  pl.BlockSpec((tm, tn), lambda i, j: (i, j)),
                  pl.BlockSpec((1, 1, tn), lambda i, j: (i // rows_per_batch, 0, j))],
        out_specs=pl.BlockSpec((tm, tn), lambda i, j: (i, j)),
        compiler_params=_params(("parallel", "arbitrary"), 56),
    )(mixed, w, x2, gate)


def _first_argmax(v, iota, n):
    m = jnp.max(v, axis=0, keepdims=True)
    idx = jnp.min(jnp.where(v == m, iota, n), axis=0, keepdims=True)
    return m, idx


def _router_kernel(x_ref, nw_ref, sc_ref, sh_ref, w2_ref, br_ref,
                   hp_ref, mi_ref, mf_ref, cnt_ref, carry_ref, hb_ref):
    step = pl.program_id(0)
    tm, d = x_ref.shape

    @pl.when(step == 0)
    def _():
        carry_ref[...] = jnp.zeros_like(carry_ref)

    gain = nw_ref[...] * (1.0 + sc_ref[0])
    shift = sh_ref[0]

    def body(r, carry):
        rows = pl.ds(pl.multiple_of(r * NORM_ROWS, NORM_ROWS), NORM_ROWS)
        h = _modulated_norm(x_ref[rows, :], gain, shift)
        hb_ref[rows, :] = h.astype(BF16)
        hp_ref[rows, :] = _pack_halves(h)
        return carry

    lax.fori_loop(0, tm // NORM_ROWS, body, 0, unroll=NORM_UNROLL)

    l2 = jnp.dot(hb_ref[...], w2_ref[...], preferred_element_type=F32)
    lt = l2[:, :LANES] + l2[:, LANES:]
    logits = lt.T[:N_ROUTER] + br_ref[...]
    iota8 = lax.broadcasted_iota(I32, (N_GROUPS, tm), 0)
    gl = logits[:N_GROUPS]
    g_max, g_sel = _first_argmax(gl, iota8, N_GROUPS)
    p_group = 1.0 / jnp.sum(jnp.exp(gl - g_max), axis=0, keepdims=True)
    e_in = jnp.zeros((EXPERTS_PER_GROUP, tm), F32)
    for g in range(N_GROUPS):
        lo = N_GROUPS + g * EXPERTS_PER_GROUP
        e_in = e_in + jnp.where(g_sel == g, logits[lo:lo + EXPERTS_PER_GROUP], 0.0)
    v0, i0 = _first_argmax(e_in, iota8, EXPERTS_PER_GROUP)
    rest = jnp.where(iota8 == i0, -jnp.inf, e_in)
    v1, i1 = _first_argmax(rest, iota8, EXPERTS_PER_GROUP)
    t = jnp.exp(v1 - v0)
    w0 = p_group / (1.0 + t)
    w1 = p_group * t / (1.0 + t)
    e0 = g_sel * EXPERTS_PER_GROUP + i0
    e1 = g_sel * EXPERTS_PER_GROUP + i1

    iota_e = lax.broadcasted_iota(I32, (N_EXPERTS, tm), 0)
    hit0 = iota_e == e0
    hit1 = iota_e == e1
    onehot = jnp.where(hit0 | hit1, 1.0, 0.0).astype(BF16)
    rr = lax.broadcasted_iota(I32, (tm, tm), 0)
    cc = lax.broadcasted_iota(I32, (tm, tm), 1)
    before = jnp.where(rr < cc, 1.0, 0.0).astype(BF16)
    prior = jnp.dot(onehot, before, preferred_element_type=F32) + carry_ref[:, :1]
    r0 = jnp.sum(jnp.where(hit0, prior, 0.0), axis=0, keepdims=True)
    r1 = jnp.sum(jnp.where(hit1, prior, 0.0), axis=0, keepdims=True)
    carry_ref[...] += jnp.dot(onehot, jnp.ones((tm, LANES), BF16), preferred_element_type=F32)

    zi = jnp.zeros((1, tm), I32)
    mi_ref[...] = jnp.concatenate([e0, e1, r0.astype(I32), r1.astype(I32), zi, zi, zi, zi], axis=0)
    row = lax.broadcasted_iota(I32, (LANES, tm), 0)
    wt = jnp.where(row == 0, w0, jnp.where(row == 1, w1, 0.0)).T
    mf_ref[:, :LANES] = jnp.broadcast_to(wt[:, 0:1], (tm, LANES))
    mf_ref[:, LANES:] = jnp.broadcast_to(wt[:, 1:2], (tm, LANES))
    cnt_ref[...] = carry_ref[...]


def _router(x1, nw, scale, shift, w2, br, seq):
    t, d = x1.shape
    tm = _pick(seq, (512, 256, 128))
    rows_per_batch = seq // tm
    return pl.pallas_call(
        _router_kernel,
        out_shape=(jax.ShapeDtypeStruct((t, d // 2), U32),
                   jax.ShapeDtypeStruct((8, t), I32),
                   jax.ShapeDtypeStruct((t, TOP_K * LANES), F32),
                   jax.ShapeDtypeStruct((N_EXPERTS, LANES), F32)),
        grid=(t // tm,),
        in_specs=[pl.BlockSpec((tm, d), lambda i: (i, 0)),
                  pl.BlockSpec((1, d), lambda i: (0, 0)),
                  pl.BlockSpec((1, 1, d), lambda i: (i // rows_per_batch, 0, 0)),
                  pl.BlockSpec((1, 1, d), lambda i: (i // rows_per_batch, 0, 0)),
                  pl.BlockSpec((d, 2 * LANES), lambda i: (0, 0)),
                  pl.BlockSpec((N_ROUTER, 1), lambda i: (0, 0))],
        out_specs=(pl.BlockSpec((tm, d // 2), lambda i: (i, 0)),
                   pl.BlockSpec((8, tm), lambda i: (0, i)),
                   pl.BlockSpec((tm, TOP_K * LANES), lambda i: (i, 0)),
                   pl.BlockSpec((N_EXPERTS, LANES), lambda i: (0, 0))),
        scratch_shapes=[pltpu.VMEM((N_EXPERTS, LANES), F32), pltpu.VMEM((tm, d), BF16)],
        compiler_params=_params(("arbitrary",), 56),
    )(x1, nw, scale, shift, w2, br)


COMBINE_ROWS = 8
PLAN_LANES = 2048
DMA_UNROLL = 8


def _plan_kernel(mi_ref, cnt_ref, dest_ref, tab_ref):
    cnt = cnt_ref[...]
    nb = jnp.floor((cnt + (EXPERT_BLOCK - 1)) * (1.0 / EXPERT_BLOCK))
    r = lax.broadcasted_iota(I32, (N_EXPERTS, N_EXPERTS), 0)
    c = lax.broadcasted_iota(I32, (N_EXPERTS, N_EXPERTS), 1)
    incl = jnp.where(c <= r, 1.0, 0.0).astype(BF16)
    bend = jnp.dot(incl, nb.astype(BF16), preferred_element_type=F32)
    bstart = bend - nb
    first_row = bstart[:, :1] * EXPERT_BLOCK

    tl = mi_ref.shape[1]
    iota_e = lax.broadcasted_iota(I32, (N_EXPERTS, tl), 0)
    for k in range(TOP_K):
        hit = iota_e == mi_ref[k:k + 1, :]
        base = jnp.sum(jnp.where(hit, first_row, 0.0), axis=0, keepdims=True)
        dest_ref[k:k + 1, :] = base.astype(I32) + mi_ref[TOP_K + k:TOP_K + k + 1, :]

    nl = tab_ref.shape[1]
    bidx = lax.broadcasted_iota(I32, (1, nl), 1).astype(F32)
    blk_e = jnp.sum(jnp.where(bend[:, :1] <= bidx, 1.0, 0.0), axis=0, keepdims=True)
    blk_e = jnp.minimum(blk_e, N_EXPERTS - 1.0)
    hit = lax.broadcasted_iota(I32, (N_EXPERTS, nl), 0) == blk_e.astype(I32)
    cnt_b = jnp.sum(jnp.where(hit, cnt[:, :1], 0.0), axis=0, keepdims=True)
    start_b = jnp.sum(jnp.where(hit, bstart[:, :1], 0.0), axis=0, keepdims=True)
    valid = jnp.clip(cnt_b - (bidx - start_b) * EXPERT_BLOCK, 0.0, float(EXPERT_BLOCK))
    n_used = jnp.broadcast_to(jnp.max(bend[:, :1], axis=0, keepdims=True), (1, nl))
    run_end = jnp.sum(jnp.where(hit, bend[:, :1], 0.0), axis=0, keepdims=True)
    zero = jnp.zeros((1, nl), F32)
    tab_ref[...] = jnp.concatenate([blk_e, valid, n_used, run_end, zero, zero, zero, zero],
                                   axis=0).astype(I32)


def _plan(meta_i, cnt, nblk):
    t = meta_i.shape[1]
    tl = _pick(t, (PLAN_LANES, 1024, 512, 256, 128))
    nl = pl.cdiv(nblk, LANES) * LANES
    return pl.pallas_call(
        _plan_kernel,
        out_shape=(jax.ShapeDtypeStruct((TOP_K, t), I32), jax.ShapeDtypeStruct((8, nl), I32)),
        grid=(t // tl,),
        in_specs=[pl.BlockSpec((8, tl), lambda i: (0, i)),
                  pl.BlockSpec((N_EXPERTS, LANES), lambda i: (0, 0))],
        out_specs=(pl.BlockSpec((TOP_K, tl), lambda i: (0, i)),
                   pl.BlockSpec((8, nl), lambda i: (0, 0))),
        compiler_params=_params(("arbitrary",), 32),
    )(meta_i, cnt)


def _dispatch_kernel(dest_ref, hp_ref, xs_ref, sem, *, chunk, t):
    base = pl.program_id(0) * chunk

    def issue(g, carry):
        row0 = pl.multiple_of(g * DMA_UNROLL, DMA_UNROLL)
        for u in range(DMA_UNROLL):
            for k in range(TOP_K):
                pltpu.make_async_copy(hp_ref.at[pl.ds(row0 + u, 1)],
                                      xs_ref.at[pl.ds(dest_ref[k * t + base + row0 + u], 1)],
                                      sem).start(priority=k)
        return carry

    lax.fori_loop(0, chunk // DMA_UNROLL, issue, 0)
    for k in range(TOP_K):
        pltpu.make_async_copy(hp_ref, xs_ref.at[pl.ds(0, chunk)], sem).wait()


def _dispatch(dest_flat, hp, cap):
    t, half = hp.shape
    chunk = _pick(t, (512, 256, 128))
    grid_spec = pltpu.PrefetchScalarGridSpec(
        num_scalar_prefetch=1,
        grid=(t // chunk,),
        in_specs=[pl.BlockSpec((chunk, half), lambda i, ds: (i, 0))],
        out_specs=pl.BlockSpec(memory_space=pl.ANY),
        scratch_shapes=[pltpu.SemaphoreType.DMA(())],
    )
    return pl.pallas_call(
        functools.partial(_dispatch_kernel, chunk=chunk, t=t),
        out_shape=jax.ShapeDtypeStruct((cap, half), U32),
        grid_spec=grid_spec,
        compiler_params=pltpu.CompilerParams(dimension_semantics=("arbitrary",),
                                             has_side_effects=True, vmem_limit_bytes=32 * MIB),
    )(dest_flat, hp)


def _ffn_kernel(be_ref, nv_ref, nu_ref, re_ref, xs_ref, wg_hbm, wu_hbm, wd_hbm, o_ref,
                wg_s, wu_s, wd_s, slot_ref, sem):
    b = pl.program_id(0)

    def weight_copies(e, slot):
        return (pltpu.make_async_copy(wg_hbm.at[e], wg_s.at[slot], sem.at[slot]),
                pltpu.make_async_copy(wu_hbm.at[e], wu_s.at[slot], sem.at[slot]),
                pltpu.make_async_copy(wd_hbm.at[e], wd_s.at[slot], sem.at[slot]))

    @pl.when(b < nu_ref[0])
    def _():
        e = be_ref[b]

        @pl.when(b == 0)
        def _():
            slot_ref[0] = 0
            for c in weight_copies(e, 0):
                c.start()

        first = jnp.logical_or(b == 0, be_ref[jnp.maximum(b - 1, 0)] != e)

        @pl.when(first)
        def _():
            slot = jnp.where(b == 0, 0, 1 - slot_ref[0])
            slot_ref[0] = slot
            for c in weight_copies(e, slot):
                c.wait()
            nxt = re_ref[b]

            @pl.when(nxt < nu_ref[0])
            def _():
                for c in weight_copies(be_ref[nxt], 1 - slot):
                    c.start()

        slot = slot_ref[0]
        blk, half = xs_ref.shape
        row = lax.broadcasted_iota(I32, (blk, 1), 0)
        xp = jnp.where(row < nv_ref[b], xs_ref[...], jnp.uint32(0))
        lo, hi = _unpack_halves(xp)
        lo = lo.astype(BF16)
        hi = hi.astype(BF16)
        a = (jnp.dot(lo, wg_s[slot, :half], preferred_element_type=F32)
             + jnp.dot(hi, wg_s[slot, half:], preferred_element_type=F32))
        u = (jnp.dot(lo, wu_s[slot, :half], preferred_element_type=F32)
             + jnp.dot(hi, wu_s[slot, half:], preferred_element_type=F32))
        mid = (a * jax.nn.sigmoid(a) * u).astype(BF16)
        o_ref[...] = _pack_halves(jnp.dot(mid, wd_s[slot], preferred_element_type=F32))


def _experts(blk_e, blk_valid, n_used, run_end, xs, wg, wu, wd):
    cap, half = xs.shape
    d = 2 * half
    ff = wg.shape[2]
    nblk = cap // EXPERT_BLOCK

    def row_map(b, be, nv, nu, re):
        return (jnp.minimum(b, nu[0] - 1), 0)

    grid_spec = pltpu.PrefetchScalarGridSpec(
        num_scalar_prefetch=4,
        grid=(nblk,),
        in_specs=[pl.BlockSpec((EXPERT_BLOCK, half), row_map),
                  pl.BlockSpec(memory_space=pl.ANY),
                  pl.BlockSpec(memory_space=pl.ANY),
                  pl.BlockSpec(memory_space=pl.ANY)],
        out_specs=pl.BlockSpec((EXPERT_BLOCK, half), row_map),
        scratch_shapes=[pltpu.VMEM((2, d, ff), BF16), pltpu.VMEM((2, d, ff), BF16),
                        pltpu.VMEM((2, ff, d), BF16), pltpu.SMEM((1,), I32),
                        pltpu.SemaphoreType.DMA((2,))],
    )
    return pl.pallas_call(
        _ffn_kernel,
        out_shape=jax.ShapeDtypeStruct((cap, half), U32),
        grid_spec=grid_spec,
        compiler_params=_params(("arbitrary",), 56),
    )(blk_e, blk_valid, n_used, run_end, xs, wg, wu, wd)


def _combine_kernel(dest_ref, x_ref, g_ref, w_ref, ys_ref, o_ref, ybuf, sem, *, chunk, t):
    i = pl.program_id(0)
    n = pl.num_programs(0)

    def issue(step, slot):
        base = step * chunk

        def body(g, carry):
            row0 = pl.multiple_of(g * DMA_UNROLL, DMA_UNROLL)
            for u in range(DMA_UNROLL):
                for k in range(TOP_K):
                    pltpu.make_async_copy(ys_ref.at[pl.ds(dest_ref[k * t + base + row0 + u], 1)],
                                          ybuf.at[slot, k, pl.ds(row0 + u, 1)],
                                          sem.at[slot]).start(priority=k)
            return carry

        lax.fori_loop(0, chunk // DMA_UNROLL, body, 0)

    @pl.when(i == 0)
    def _():
        issue(0, 0)

    slot = i % 2

    @pl.when(i + 1 < n)
    def _():
        issue(i + 1, 1 - slot)

    for k in range(TOP_K):
        pltpu.make_async_copy(ys_ref.at[pl.ds(0, chunk)], ybuf.at[slot, k], sem.at[slot]).wait()
    half = ybuf.shape[-1]
    g = jnp.broadcast_to(g_ref[0], (COMBINE_ROWS, 2 * half))

    def widen(v):
        return jnp.concatenate([v] * (half // LANES), axis=1)

    def body(c, carry):
        rows = pl.ds(pl.multiple_of(c * COMBINE_ROWS, COMBINE_ROWS), COMBINE_ROWS)
        w0 = widen(w_ref[rows, :LANES])
        w1 = widen(w_ref[rows, LANES:])
        lo0, hi0 = _unpack_halves(ybuf[slot, 0, rows, :])
        lo1, hi1 = _unpack_halves(ybuf[slot, 1, rows, :])
        o_ref[rows, :half] = x_ref[rows, :half] + g[:, :half] * (w0 * lo0 + w1 * lo1)
        o_ref[rows, half:] = x_ref[rows, half:] + g[:, half:] * (w0 * hi0 + w1 * hi1)
        return carry

    lax.fori_loop(0, chunk // COMBINE_ROWS, body, 0, unroll=4)


def _combine(dest_flat, x1, gate, w_rows, ys, seq):
    t, d = x1.shape
    chunk = _pick(seq, (512, 256, 128))
    rows_per_batch = seq // chunk
    grid_spec = pltpu.PrefetchScalarGridSpec(
        num_scalar_prefetch=1,
        grid=(t // chunk,),
        in_specs=[pl.BlockSpec((chunk, d), lambda i, ds: (i, 0)),
                  pl.BlockSpec((1, 1, d), lambda i, ds: (i // rows_per_batch, 0, 0)),
                  pl.BlockSpec((chunk, TOP_K * LANES), lambda i, ds: (i, 0)),
                  pl.BlockSpec(memory_space=pl.ANY)],
        out_specs=pl.BlockSpec((chunk, d), lambda i, ds: (i, 0)),
        scratch_shapes=[pltpu.VMEM((2, TOP_K, chunk, d // 2), U32),
                        pltpu.SemaphoreType.DMA((2,))],
    )
    return pl.pallas_call(
        functools.partial(_combine_kernel, chunk=chunk, t=t),
        out_shape=jax.ShapeDtypeStruct((t, d), F32),
        grid_spec=grid_spec,
        compiler_params=_params(("arbitrary",), 56),
    )(dest_flat, x1, gate, w_rows, ys)


def _layer(x, c, w_ada, b_ada, norm1_w, w_in, q_norm_w, k_norm_w, sinks, w_pool, pool_scale,
           w_attn_up, w_pool_up, w_out, norm2_w, w_router_group, b_router_group,
           w_router_expert, b_router_expert, w_gate, w_up, w_down):
    b, s, d = x.shape
    t = b * s
    aw = w_attn_up.shape[0]
    pw = w_pool_up.shape[0]
    kvw = N_KV_HEADS * HEAD_DIM
    x2 = x.reshape(t, d)

    c8 = jnp.zeros((8, d), F32).at[:b].set(c)
    ada = _ada(c8, w_ada, b_ada.reshape(1, 6 * d))[:b]
    shift1, scale1, gate1, shift2, scale2, gate2 = [a.reshape(b, 1, d) for a in jnp.split(ada, 6, axis=-1)]

    q_col, kv_col = 0, aw
    p_col = aw + 2 * kvw
    ga_col = p_col + pw
    gb_col = ga_col + d
    h = _norm(x2, norm1_w.reshape(1, d), scale1, shift1, s)
    ne, _, ff = w_gate.shape
    proj, (wu_b, wd_b, wa_b, wb_b, wo_b) = _in_proj(
        h, w_in.astype(BF16),
        [w_up.reshape(ne * d, ff), w_down.reshape(ne * ff, d), w_attn_up, w_pool_up, w_out])
    wu_b = wu_b.reshape(ne, d, ff)
    wd_b = wd_b.reshape(ne, ff, d)
    proj3 = proj.reshape(b, s, proj.shape[1])

    y_attn, wg_b = _attention(proj3, sinks, q_norm_w, k_norm_w, q_col, kv_col,
                              w_gate.reshape(ne * d, ff))
    y_attn = y_attn.reshape(t, aw)
    wg_b = wg_b.reshape(ne, d, ff)
    y_pool = _pool(proj3, w_pool.astype(BF16), pool_scale.reshape(1, pw), p_col).reshape(t, pw)
    mixed = _mix(y_attn, y_pool, wa_b, wb_b, proj, ga_col, gb_col)
    x1 = _out_proj(mixed, wo_b, x2, gate1, s)

    wr = jnp.concatenate([w_router_group, w_router_expert,
                          jnp.zeros((d, LANES - N_ROUTER), F32)], axis=1)
    wr_hi = wr.astype(BF16)
    wr_lo = (wr - wr_hi.astype(F32)).astype(BF16)
    wr2 = jnp.concatenate([wr_hi, wr_lo], axis=1)
    br = jnp.concatenate([b_router_group, b_router_expert]).reshape(N_ROUTER, 1)
    hp, meta_i, w_lanes, cnt = _router(x1, norm2_w.reshape(1, d), scale2, shift2, wr2, br, s)

    cap = TOP_K * t + N_EXPERTS * EXPERT_BLOCK
    nblk = cap // EXPERT_BLOCK
    dest, tab = _plan(meta_i, cnt, nblk)
    dest_flat = dest.reshape(TOP_K * t)
    blk_e, blk_valid, n_used, run_end = tab[0, :nblk], tab[1, :nblk], tab[2, :1], tab[3, :nblk]

    xs = _dispatch(dest_flat, hp, cap)
    ys = _experts(blk_e, blk_valid, n_used, run_end, xs, wg_b, wu_b, wd_b)
    out = _combine(dest_flat, x1, gate2, w_lanes, ys, s)
    return out.reshape(b, s, d)


def kernel(x, c, w_ada, b_ada, norm1_w, w_in, q_norm_w, k_norm_w, sinks, w_pool, pool_scale,
           w_attn_up, w_pool_up, w_out, norm2_w, w_router_group, b_router_group,
           w_router_expert, b_router_expert, w_gate, w_up, w_down):
    for l in range(w_ada.shape[0]):
        x = _layer(x, c, w_ada[l], b_ada[l], norm1_w[l], w_in[l], q_norm_w[l], k_norm_w[l],
                   sinks[l], w_pool[l], pool_scale[l], w_attn_up[l], w_pool_up[l], w_out[l],
                   norm2_w[l], w_router_group[l], b_router_group[l], w_router_expert[l],
                   b_router_expert[l], w_gate[l], w_up[l], w_down[l])
    return x
```
